```python
import jax, jax.numpy as jnp
from jax import lax
import numpy as np


D_MODEL = 1024
BATCH = 8
SEQ = 2048
DEPTH = 1

CHUNK = 64
Q_BLOCK = 128
D_A = D_MODEL // 2
HEAD_A = 64
H_A = D_A // HEAD_A
DECAY_RANK = 64
AAA_RANK = 64
D_B = D_MODEL // 2
HEAD_B = 64
H_B = D_B // HEAD_B
RWKV_COLS = 4 * D_A + DECAY_RANK + AAA_RANK
FOX_COLS = 4 * D_B + H_B
GATE_COLS = 2 * D_MODEL
IN_COLS = RWKV_COLS + FOX_COLS + GATE_COLS
RMS_EPS = 1e-6
LNX_EPS = 64e-5

kernel_name = 'hybrid_rwkv7_fox_gated_block'


def _rmsnorm(x, g):
    xf = x.astype(jnp.float32)
    y = xf * lax.rsqrt(jnp.mean(xf * xf, axis=-1, keepdims=True) + RMS_EPS)
    return (y * g.astype(jnp.float32)).astype(x.dtype)


def _token_shift(u):
    return jnp.pad(u, ((0, 0), (1, 0), (0, 0)))[:, :-1]


def _wkv7_scan(r, decay, k, v, a_vec, b_vec):
    B, S, H, N = r.shape

    def step(state, inp):
        r_t, w_t, k_t, v_t, a_t, b_t = inp
        sa = jnp.einsum('bhij,bhj->bhi', state, a_t)
        state = (state * w_t[:, :, None, :]
                 + sa[..., None] * b_t[:, :, None, :]
                 + v_t[..., None] * k_t[:, :, None, :])
        return state, jnp.einsum('bhij,bhj->bhi', state, r_t)

    xs = (jnp.moveaxis(r, 1, 0), jnp.moveaxis(decay, 1, 0), jnp.moveaxis(k, 1, 0),
          jnp.moveaxis(v, 1, 0), jnp.moveaxis(a_vec, 1, 0), jnp.moveaxis(b_vec, 1, 0))
    state0 = jnp.zeros((B, H, N, N), jnp.float32)
    _, y = lax.scan(step, state0, xs)
    return jnp.moveaxis(y, 0, 1)


def _rwkv7_mixer(u, mu, w_up, w0, a_up, a0, k_k, k_a, r_k, lnx_w, lnx_b):
    B, S, _ = u.shape
    u = u + (_token_shift(u) - u) * mu
    r, k, v, wd, ad, gate = jnp.split(
        u, [D_A, 2 * D_A, 3 * D_A, 3 * D_A + DECAY_RANK, 3 * D_A + DECAY_RANK + AAA_RANK], axis=-1)
    w = -jax.nn.softplus(-(w0 + jnp.tanh(wd) @ w_up)) - 0.5
    decay = jnp.exp(-jnp.exp(w.astype(jnp.float32)))
    a = jax.nn.sigmoid(a0 + ad @ a_up)
    heads = lambda t: t.reshape(B, S, H_A, HEAD_A).astype(jnp.float32)
    kk = heads(k * k_k)
    kk = kk / jnp.maximum(jnp.linalg.norm(kk, axis=-1, keepdims=True), 1e-12)
    k = k * (1.0 + (a - 1.0) * k_a)
    r_h, k_h, v_h, a_h = heads(r), heads(k), heads(v), heads(a)
    y = _wkv7_scan(r_h, heads(decay), k_h, v_h, -kk, kk * a_h)
    mean = jnp.mean(y, axis=-1, keepdims=True)
    var = jnp.mean(jnp.square(y - mean), axis=-1, keepdims=True)
    y = (y - mean) * lax.rsqrt(var + LNX_EPS)
    y = y * lnx_w.reshape(H_A, HEAD_A).astype(jnp.float32) + lnx_b.reshape(H_A, HEAD_A).astype(jnp.float32)
    bonus = jnp.sum(r_h * k_h * r_k.astype(jnp.float32), axis=-1, keepdims=True) * v_h
    y = (y + bonus).reshape(B, S, D_A).astype(u.dtype)
    return y * jax.nn.silu(gate)


def _fox_mixer(u, f_bias, q_norm_g, k_norm_g):
    B, S, _ = u.shape
    q, k, v, gate, f_logit = jnp.split(u, [D_B, 2 * D_B, 3 * D_B, 4 * D_B], axis=-1)
    to_bhsd = lambda t: jnp.transpose(t, (0, 2, 1, 3))
    q = to_bhsd(_rmsnorm(q.reshape(B, S, H_B, HEAD_B), q_norm_g))
    k = to_bhsd(_rmsnorm(k.reshape(B, S, H_B, HEAD_B), k_norm_g))
    v = to_bhsd(v.reshape(B, S, H_B, HEAD_B))
    log_f = jax.nn.log_sigmoid((f_logit + f_bias).astype(jnp.float32))
    cum = jnp.transpose(jnp.cumsum(log_f, axis=1), (0, 2, 1))
    scale = HEAD_B ** -0.5
    outs = []
    for i in range(S // Q_BLOCK):
        lo, hi = i * Q_BLOCK, (i + 1) * Q_BLOCK
        qb, kp, vp = q[:, :, lo:hi], k[:, :, :hi], v[:, :, :hi]
        logits = (jnp.einsum('bhqd,bhkd->bhqk', qb, kp).astype(jnp.float32) * scale
                  + cum[:, :, lo:hi, None] - cum[:, :, None, :hi])
        causal = (lo + jnp.arange(Q_BLOCK))[:, None] >= jnp.arange(hi)[None, :]
        logits = jnp.where(causal, logits, -jnp.inf)
        p = jax.nn.softmax(logits, axis=-1)
        outs.append(jnp.einsum('bhqk,bhkd->bhqd', p.astype(vp.dtype), vp))
    o = jnp.concatenate(outs, axis=2)
    o = jnp.transpose(o, (0, 2, 1, 3)).reshape(B, S, D_B)
    return o * jax.nn.silu(gate)


def setup_inputs(seed: int = 0) -> dict:
    key = jax.random.key(seed)
    ks = jax.random.split(key, 20)
    nrm = lambda k, shape, s: jax.random.normal(k, shape, jnp.float32) * s
    unif = lambda k, shape, lo, hi: jax.random.uniform(k, shape, jnp.float32, lo, hi)
    return {
        'x': jax.random.normal(ks[0], (BATCH, SEQ, D_MODEL), jnp.float32),
        'norm_g': 1.0 + nrm(ks[1], (DEPTH, D_MODEL), 0.02),
        'w_in': nrm(ks[2], (DEPTH, D_MODEL, IN_COLS), D_MODEL ** -0.5),
        'shift_mu': unif(ks[3], (DEPTH, RWKV_COLS), 0.0, 1.0),
        'w_lora_up': nrm(ks[4], (DEPTH, DECAY_RANK, D_A), 0.1 * DECAY_RANK ** -0.5),
        'w0': unif(ks[5], (DEPTH, D_A), -6.0, -1.0),
        'a_lora_up': nrm(ks[6], (DEPTH, AAA_RANK, D_A), 0.1 * AAA_RANK ** -0.5),
        'a0': nrm(ks[7], (DEPTH, D_A), 0.5),
        'k_k': 0.85 + nrm(ks[8], (DEPTH, D_A), 0.02),
        'k_a': 1.0 + nrm(ks[9], (DEPTH, D_A), 0.02),
        'r_k': nrm(ks[10], (DEPTH, H_A, HEAD_A), 0.1),
        'lnx_w': 1.0 + nrm(ks[11], (DEPTH, D_A), 0.02),
        'lnx_b': nrm(ks[12], (DEPTH, D_A), 0.02),
        'f_bias': unif(ks[13], (DEPTH, H_B), 1.0, 5.0),
        'q_norm_g': 1.0 + nrm(ks[14], (DEPTH, HEAD_B), 0.02),
        'k_norm_g': 1.0 + nrm(ks[15], (DEPTH, HEAD_B), 0.02),
        'w_out_a': nrm(ks[16], (DEPTH, D_A, D_MODEL), D_A ** -0.5),
        'w_out_b': nrm(ks[17], (DEPTH, D_B, D_MODEL), D_B ** -0.5),
        'w_out': nrm(ks[18], (DEPTH, D_MODEL, D_MODEL), D_MODEL ** -0.5),
        'final_norm_g': 1.0 + nrm(ks[19], (D_MODEL,), 0.02),
    }


def reference(x, norm_g, w_in, shift_mu, w_lora_up, w0, a_lora_up, a0, k_k, k_a, r_k,
              lnx_w, lnx_b, f_bias, q_norm_g, k_norm_g, w_out_a, w_out_b, w_out, final_norm_g):
    for l in range(DEPTH):
        h = _rmsnorm(x, norm_g[l])
        u = h @ w_in[l]
        u_a, u_b, u_g = jnp.split(u, [RWKV_COLS, RWKV_COLS + FOX_COLS], axis=-1)
        y_a = _rwkv7_mixer(u_a, shift_mu[l], w_lora_up[l], w0[l], a_lora_up[l], a0[l],
                           k_k[l], k_a[l], r_k[l], lnx_w[l], lnx_b[l]) @ w_out_a[l]
        y_b = _fox_mixer(u_b, f_bias[l], q_norm_g[l], k_norm_g[l]) @ w_out_b[l]
        g_a, g_b = jnp.split(u_g, 2, axis=-1)
        merged = jax.nn.sigmoid(g_a) * y_a + jax.nn.sigmoid(g_b) * y_b
        x = x + merged @ w_out[l]
    return _rmsnorm(x, final_norm_g)
```

```python
import functools

import jax
import jax.numpy as jnp
from jax import lax
from jax.experimental import pallas as pl
from jax.experimental.pallas import tpu as pltpu

F32 = jnp.float32
BF16 = jnp.bfloat16

HEAD = 64
LANES = 128
RMS_EPS = 1e-6
LNX_EPS = 64e-5
CHUNK = 64
VMEM_LIMIT = 56 * 1024 * 1024


def _bf(x):
    return x.astype(BF16)


def _mm(a, b):
    return jnp.dot(_bf(a), _bf(b), preferred_element_type=F32)


def _mm_nt(a, b):
    return lax.dot_general(_bf(a), _bf(b), (((1,), (1,)), ((), ())), preferred_element_type=F32)


def _mm_tn(a, b):
    return lax.dot_general(_bf(a), _bf(b), (((0,), (0,)), ((), ())), preferred_element_type=F32)


def _split2(x):
    hi = _bf(x)
    lo = _bf(x - hi.astype(F32))
    return hi, lo


def _split3(x):
    hi = _bf(x)
    r1 = x - hi.astype(F32)
    mid = _bf(r1)
    lo = _bf(r1 - mid.astype(F32))
    return hi, mid, lo


def _gsum(x, g):
    hi, lo = _split2(x)
    return (jnp.dot(hi, g, preferred_element_type=F32)
            + jnp.dot(lo, g, preferred_element_type=F32))


def _sigmoid(x):
    return 1.0 / (1.0 + jnp.exp(-x))


def _inproj_kernel(x_ref, g_ref, w_ref, o_ref, of_ref, h_scr, *, n_col_tiles):
    j = pl.program_id(1)

    @pl.when(j == 0)
    def _():
        x = x_ref[...]
        ms = jnp.mean(x * x, axis=-1, keepdims=True)
        h_scr[...] = _bf(x * lax.rsqrt(ms + RMS_EPS) * g_ref[...])

    acc = jnp.dot(h_scr[...], w_ref[...], preferred_element_type=F32)
    o_ref[...] = _bf(acc)

    @pl.when(j == n_col_tiles - 1)
    def _():
        of_ref[...] = acc[:, acc.shape[1] - LANES:]


def _inproj(x2, norm_g, w_perm, *, tm, tn):
    T, D = x2.shape
    N = w_perm.shape[1]
    nj = N // tn
    return pl.pallas_call(
        functools.partial(_inproj_kernel, n_col_tiles=nj),
        grid=(T // tm, nj),
        in_specs=[
            pl.BlockSpec((tm, D), lambda i, j: (i, 0)),
            pl.BlockSpec((1, D), lambda i, j: (0, 0)),
            pl.BlockSpec((D, tn), lambda i, j: (0, j)),
        ],
        out_specs=[
            pl.BlockSpec((tm, tn), lambda i, j: (i, j)),
            pl.BlockSpec((tm, LANES), lambda i, j: (i, 0)),
        ],
        out_shape=[
            jax.ShapeDtypeStruct((T, N), BF16),
            jax.ShapeDtypeStruct((T, LANES), F32),
        ],
        scratch_shapes=[pltpu.VMEM((tm, D), BF16)],
        compiler_params=pltpu.CompilerParams(
            dimension_semantics=("parallel", "arbitrary"), vmem_limit_bytes=VMEM_LIMIT),
        name="inproj",
    )(x2, norm_g, w_perm)


def _fprep_kernel(f_ref, fb_ref, cum_ref, cumt_ref, *, blk):
    S = f_ref.shape[1]
    row = lax.broadcasted_iota(jnp.int32, (blk, blk), 0)
    col = lax.broadcasted_iota(jnp.int32, (blk, blk), 1)
    tril = _bf(jnp.where(row >= col, 1.0, 0.0))
    carry = jnp.zeros((1, LANES), F32)
    for i in range(S // blk):
        z = f_ref[0, i * blk:(i + 1) * blk, :] + fb_ref[...]
        lf = jnp.minimum(z, 0.0) - jnp.log1p(jnp.exp(-jnp.abs(z)))
        hi, mid, lo = _split3(lf)
        c = (jnp.dot(tril, hi, preferred_element_type=F32)
             + jnp.dot(tril, mid, preferred_element_type=F32)
             + jnp.dot(tril, lo, preferred_element_type=F32)) + carry
        cum_ref[0, i * blk:(i + 1) * blk, :] = c
        cumt_ref[0, :, i * blk:(i + 1) * blk] = jnp.transpose(c)[0:8, :]
        carry = c[blk - 1:blk, :]


def _fprep(uf3, fb_pad):
    B, S, _ = uf3.shape
    return pl.pallas_call(
        functools.partial(_fprep_kernel, blk=256),
        grid=(B,),
        in_specs=[
            pl.BlockSpec((1, S, LANES), lambda b: (b, 0, 0)),
            pl.BlockSpec((1, LANES), lambda b: (0, 0)),
        ],
        out_specs=[
            pl.BlockSpec((1, S, LANES), lambda b: (b, 0, 0)),
            pl.BlockSpec((1, 8, S), lambda b: (b, 0, 0)),
        ],
        out_shape=[
            jax.ShapeDtypeStruct((B, S, LANES), F32),
            jax.ShapeDtypeStruct((B, 8, S), F32),
        ],
        compiler_params=pltpu.CompilerParams(dimension_semantics=("parallel",)),
        name="fprep",
    )(uf3, fb_pad)


_P_MU_R, _P_MU_K, _P_MU_V, _P_MU_G, _P_W0, _P_A0, _P_KK, _P_KA, _P_RK, _P_LW, _P_LB = range(11)


def _shift_mix(x, carry_row, mu):
    rolled = pltpu.roll(x, shift=1, axis=0)
    row = lax.broadcasted_iota(jnp.int32, x.shape, 0)
    prev = jnp.where(row == 0, carry_row, rolled)
    return x + (prev - x) * mu


def _rwkv_kernel(r_ref, k_ref, v_ref, g_ref, wa_ref, pv_ref, muwa_ref, lora_ref, gm_ref,
                 o_ref, state_scr, carry_scr, y_scr, *, n_heads):
    c = pl.program_id(1)
    L = r_ref.shape[1]
    DA = r_ref.shape[2]

    @pl.when(c == 0)
    def _():
        state_scr[...] = jnp.zeros_like(state_scr)
        carry_scr[...] = jnp.zeros_like(carry_scr)

    def prm(i):
        return pv_ref[i:i + 1, :]

    r_raw = r_ref[0].astype(F32)
    k_raw = k_ref[0].astype(F32)
    v_raw = v_ref[0].astype(F32)
    g_raw = g_ref[0].astype(F32)
    wa_raw = wa_ref[0].astype(F32)

    r = _shift_mix(r_raw, carry_scr[0:1, :], prm(_P_MU_R))
    k = _shift_mix(k_raw, carry_scr[1:2, :], prm(_P_MU_K))
    v = _shift_mix(v_raw, carry_scr[2:3, :], prm(_P_MU_V))
    gate = _shift_mix(g_raw, carry_scr[3:4, :], prm(_P_MU_G))
    wa = _shift_mix(wa_raw, carry_scr[4:5, 0:LANES], muwa_ref[...])

    carry_scr[0:1, :] = r_raw[L - 1:L, :]
    carry_scr[1:2, :] = k_raw[L - 1:L, :]
    carry_scr[2:3, :] = v_raw[L - 1:L, :]
    carry_scr[3:4, :] = g_raw[L - 1:L, :]
    carry_scr[4:5, 0:LANES] = wa_raw[L - 1:L, :]

    lane = lax.broadcasted_iota(jnp.int32, wa.shape, 1)
    wa_act = jnp.where(lane < HEAD, jnp.tanh(wa), wa)
    lo = _mm(wa_act, lora_ref[...])
    zw = -(prm(_P_W0) + lo[:, 0:DA])
    w = -(jnp.maximum(zw, 0.0) + jnp.log1p(jnp.exp(-jnp.abs(zw)))) - 0.5
    ld = -jnp.exp(w)
    a = _sigmoid(prm(_P_A0) + lo[:, DA:2 * DA])

    gm = gm_ref[...]
    kk = k * prm(_P_KK)
    kk = kk / jnp.maximum(jnp.sqrt(_gsum(kk * kk, gm)), 1e-12)
    k2 = k * (1.0 + (a - 1.0) * prm(_P_KA))
    bvec = kk * a

    row = lax.broadcasted_iota(jnp.int32, (L, L), 0)
    col = lax.broadcasted_iota(jnp.int32, (L, L), 1)
    incl = row >= col
    strict = row > col
    tril = _bf(jnp.where(incl, 1.0, 0.0))
    h3, m3, l3 = _split3(ld)
    cum = (jnp.dot(tril, h3, preferred_element_type=F32)
           + jnp.dot(tril, m3, preferred_element_type=F32)
           + jnp.dot(tril, l3, preferred_element_type=F32))
    cum_l = cum[L - 1:L, :]
    e_pos = jnp.exp(cum)
    e_neg = jnp.exp(-cum)
    w_l = jnp.exp(cum_l)

    a_t = -kk * jnp.exp(cum - ld)
    r_t = r * e_pos
    b_t = bvec * e_neg
    k_t = k2 * e_neg
    b_hat = b_t * w_l
    k_hat = k_t * w_l

    eye = jnp.where(row == col, 1.0, 0.0)
    for h in range(n_heads):
        sl = slice(h * HEAD, (h + 1) * HEAD)
        at, rt, bt, kt = a_t[:, sl], r_t[:, sl], b_t[:, sl], k_t[:, sl]
        vh = v[:, sl]
        s0 = state_scr[h]
        a_ab = jnp.where(strict, _mm_nt(at, bt), 0.0)
        a_ak = jnp.where(strict, _mm_nt(at, kt), 0.0)
        a_rb = jnp.where(incl, _mm_nt(rt, bt), 0.0)
        a_rk = jnp.where(incl, _mm_nt(rt, kt), 0.0)
        tinv = eye + a_ab
        pw = a_ab
        n = 2
        while n < L:
            pw = _mm(pw, pw)
            tinv = tinv + _mm(tinv, pw)
            n *= 2
        u = _mm(tinv, _mm_nt(at, s0) + _mm(a_ak, vh))
        y = _mm_nt(rt, s0) + _mm(a_rb, u) + _mm(a_rk, vh)
        state_scr[h] = s0 * w_l[:, sl] + _mm_tn(u, b_hat[:, sl]) + _mm_tn(vh, k_hat[:, sl])
        y_scr[:, sl] = y

    y = y_scr[...]
    inv_n = 1.0 / HEAD
    mean = _gsum(y, gm) * inv_n
    yc = y - mean
    var = _gsum(yc * yc, gm) * inv_n
    yn = yc * lax.rsqrt(var + LNX_EPS) * prm(_P_LW) + prm(_P_LB)
    bonus = _gsum(r * k2 * prm(_P_RK), gm) * v
    out = (yn + bonus) * (gate * _sigmoid(gate))
    o_ref[0] = _bf(out)


def _rwkv(u3, pvec, mu_wa, lora, gm, *, col_r, col_wa, d_a):
    B, S, _ = u3.shape
    L = CHUNK
    n_heads = d_a // HEAD
    cb = col_r // d_a
    blk = lambda off: pl.BlockSpec((1, L, d_a), lambda b, c, off=off: (b, c, cb + off))
    full = lambda arr: pl.BlockSpec(arr.shape, lambda b, c: (0,) * arr.ndim)
    return pl.pallas_call(
        functools.partial(_rwkv_kernel, n_heads=n_heads),
        grid=(B, S // L),
        in_specs=[
            blk(0), blk(1), blk(2), blk(3),
            pl.BlockSpec((1, L, LANES), lambda b, c: (b, c, col_wa // LANES)),
            full(pvec), full(mu_wa), full(lora), full(gm),
        ],
        out_specs=pl.BlockSpec((1, L, d_a), lambda b, c: (b, c, 0)),
        out_shape=jax.ShapeDtypeStruct((B, S, d_a), BF16),
        scratch_shapes=[
            pltpu.VMEM((n_heads, HEAD, HEAD), F32),
            pltpu.VMEM((8, d_a), F32),
            pltpu.VMEM((L, d_a), F32),
        ],
        compiler_params=pltpu.CompilerParams(
            dimension_semantics=("parallel", "arbitrary"), vmem_limit_bytes=VMEM_LIMIT),
        name="rwkv7",
    )(u3, u3, u3, u3, u3, pvec, mu_wa, lora, gm)


def _fox_kernel(q_ref, k_ref, v_ref, g_ref, cq_ref, ct_ref, qg_ref, kg_ref, gm_ref,
                o_ref, kn_scr, *, tq, tk, pair_axis):
    p = pl.program_id(pair_axis)
    qi = pl.program_id(2)
    S = k_ref.shape[1]
    gm = gm_ref[...]
    scale = HEAD ** -0.5

    @pl.when(qi == 0)
    def _():
        for i in range(S // tk):
            kb = k_ref[0, i * tk:(i + 1) * tk, :].astype(F32)
            ms = _gsum(kb * kb, gm) * (1.0 / HEAD)
            kn_scr[i * tk:(i + 1) * tk, :] = _bf(kb * lax.rsqrt(ms + RMS_EPS) * kg_ref[...])

    q = q_ref[0].astype(F32)
    ms = _gsum(q * q, gm) * (1.0 / HEAD)
    qn = q * lax.rsqrt(ms + RMS_EPS) * (qg_ref[...] * scale)
    lane = lax.broadcasted_iota(jnp.int32, (tq, LANES), 1)
    head_of_lane = lane // HEAD
    cq_all = cq_ref[0]

    outs = []
    for hh in range(2):
        qh = _bf(jnp.where(head_of_lane == hh, qn, 0.0))
        hsel = lax.broadcasted_iota(jnp.int32, (tq, LANES), 1) == (2 * p + hh)
        cq = jnp.sum(jnp.where(hsel, cq_all, 0.0), axis=-1, keepdims=True)

        def logits(j):
            ks = pl.multiple_of(j * tk, tk)
            kb = kn_scr[pl.ds(ks, tk), :]
            s = lax.dot_general(qh, kb, (((1,), (1,)), ((), ())), preferred_element_type=F32)
            ck = ct_ref[0, pl.ds(2 * p + hh, 1), pl.ds(ks, tk)]
            return s + cq - ck, ks

        def update(carry, s, ks):
            m, l, acc = carry
            m_new = jnp.maximum(m, jnp.max(s, axis=-1, keepdims=True))
            alpha = jnp.exp(m - m_new)
            pexp = jnp.exp(s - m_new)
            l = alpha * l + jnp.sum(pexp, axis=-1, keepdims=True)
            vb = v_ref[0, pl.ds(ks, tk), :]
            acc = alpha * acc + jnp.dot(_bf(pexp), vb, preferred_element_type=F32)
            return m_new, l, acc

        def body(j, carry):
            s, ks = logits(j)
            return update(carry, s, ks)

        init = (jnp.full((tq, 1), -1e30, F32), jnp.zeros((tq, 1), F32), jnp.zeros((tq, LANES), F32))
        nfull = (qi * tq) // tk
        carry = lax.fori_loop(0, nfull, body, init)
        for d in range(tq // tk):
            j = nfull + d
            s, ks = logits(j)
            rowi = qi * tq + lax.broadcasted_iota(jnp.int32, (tq, tk), 0)
            coli = j * tk + lax.broadcasted_iota(jnp.int32, (tq, tk), 1)
            s = jnp.where(rowi >= coli, s, -1e30)
            carry = update(carry, s, ks)
        m, l, acc = carry
        outs.append(acc / l)

    o = jnp.where(head_of_lane == 0, outs[0], outs[1])
    g = g_ref[0].astype(F32)
    o_ref[0] = _bf(o * (g * _sigmoid(g)))


def _fox(u3, cum, cumt, qg2, kg2, gm128, *, col_q, d_b, tq, tk):
    B, S, _ = u3.shape
    n_pairs = d_b // LANES
    cq, ck, cv, cg = ((col_q + i * d_b) // LANES for i in range(4))
    return pl.pallas_call(
        functools.partial(_fox_kernel, tq=tq, tk=tk, pair_axis=1),
        grid=(B, n_pairs, S // tq),
        in_specs=[
            pl.BlockSpec((1, tq, LANES), lambda b, p, i: (b, i, cq + p)),
            pl.BlockSpec((1, S, LANES), lambda b, p, i: (b, 0, ck + p)),
            pl.BlockSpec((1, S, LANES), lambda b, p, i: (b, 0, cv + p)),
            pl.BlockSpec((1, tq, LANES), lambda b, p, i: (b, i, cg + p)),
            pl.BlockSpec((1, tq, LANES), lambda b, p, i: (b, i, 0)),
            pl.BlockSpec((1, 8, S), lambda b, p, i: (b, 0, 0)),
            pl.BlockSpec((1, LANES), lambda b, p, i: (0, 0)),
            pl.BlockSpec((1, LANES), lambda b, p, i: (0, 0)),
            pl.BlockSpec((LANES, LANES), lambda b, p, i: (0, 0)),
        ],
        out_specs=pl.BlockSpec((1, tq, LANES), lambda b, p, i: (b, i, p)),
        out_shape=jax.ShapeDtypeStruct((B, S, d_b), BF16),
        scratch_shapes=[pltpu.VMEM((S, LANES), BF16)],
        compiler_params=pltpu.CompilerParams(
            dimension_semantics=("parallel", "parallel", "arbitrary"), vmem_limit_bytes=VMEM_LIMIT),
        name="fox",
    )(u3, u3, u3, u3, cum, cumt, qg2, kg2, gm128)


def _out_kernel(x_ref, ya_ref, yb_ref, ga_ref, gb_ref, woa_ref, wob_ref, wo_ref, fg_ref, o_ref):
    za = jnp.dot(ya_ref[...], woa_ref[...], preferred_element_type=F32)
    zb = jnp.dot(yb_ref[...], wob_ref[...], preferred_element_type=F32)
    merged = (_sigmoid(ga_ref[...].astype(F32)) * za + _sigmoid(gb_ref[...].astype(F32)) * zb)
    o = x_ref[...] + jnp.dot(_bf(merged), wo_ref[...], preferred_element_type=F32)
    ms = jnp.mean(o * o, axis=-1, keepdims=True)
    o_ref[...] = o * lax.rsqrt(ms + RMS_EPS) * fg_ref[...]


def _out(x2, ya2, yb2, u2, woa, wob, wo, fg, *, tm):
    T, D = x2.shape
    full = lambda arr: pl.BlockSpec(arr.shape, lambda i: (0,) * arr.ndim)
    return pl.pallas_call(
        _out_kernel,
        grid=(T // tm,),
        in_specs=[
            pl.BlockSpec((tm, D), lambda i: (i, 0)),
            pl.BlockSpec((tm, ya2.shape[1]), lambda i: (i, 0)),
            pl.BlockSpec((tm, yb2.shape[1]), lambda i: (i, 0)),
            pl.BlockSpec((tm, D), lambda i: (i, 0)),
            pl.BlockSpec((tm, D), lambda i: (i, 1)),
            full(woa), full(wob), full(wo), full(fg),
        ],
        out_specs=pl.BlockSpec((tm, D), lambda i: (i, 0)),
        out_shape=jax.ShapeDtypeStruct((T, D), F32),
        compiler_params=pltpu.CompilerParams(
            dimension_semantics=("parallel",), vmem_limit_bytes=VMEM_LIMIT),
        name="outstage",
    )(x2, ya2, yb2, u2, u2, woa, wob, wo, fg)


def _block_ones(width):
    i = jnp.arange(width) // HEAD
    return (i[:, None] == i[None, :]).astype(BF16)


def _layer(x2, B, S, norm_g, w_in, shift_mu, w_lora_up, w0, a_lora_up, a0, k_k, k_a, r_k,
           lnx_w, lnx_b, f_bias, q_norm_g, k_norm_g, w_out_a, w_out_b, w_out, out_gain):
    T, D = x2.shape
    d_a = w0.shape[0]
    d_b = w_out_b.shape[0]
    rank = w_lora_up.shape[0]
    h_b = f_bias.shape[0]
    rw = 4 * d_a + 2 * rank
    fx = 4 * d_b + h_b

    ca = w_in[:, :rw]
    cb = w_in[:, rw:rw + fx]
    cg = w_in[:, rw + fx:]
    wdad = ca[:, 3 * d_a:3 * d_a + 2 * rank]
    flog = jnp.pad(cb[:, 4 * d_b:], ((0, 0), (0, LANES - h_b)))
    w_perm = _bf(jnp.concatenate(
        [cg, ca[:, :3 * d_a], ca[:, 3 * d_a + 2 * rank:], cb[:, :4 * d_b], wdad, flog], axis=1))
    col_r = 2 * D
    col_q = col_r + 4 * d_a
    col_wa = col_q + 4 * d_b

    u2, uf = _inproj(x2, norm_g.reshape(1, D), w_perm, tm=1024, tn=640)
    u3 = u2.reshape(B, S, u2.shape[1])

    mu = shift_mu
    rows = [mu[:d_a], mu[d_a:2 * d_a], mu[2 * d_a:3 * d_a], mu[3 * d_a + 2 * rank:],
            w0, a0, k_k, k_a, r_k.reshape(-1), lnx_w, lnx_b]
    pvec = jnp.stack(rows + [jnp.zeros_like(w0)] * (16 - len(rows)), axis=0)
    mu_wa = mu[3 * d_a:3 * d_a + 2 * rank].reshape(1, 2 * rank)
    z = jnp.zeros((rank, d_a), F32)
    lora = _bf(jnp.concatenate(
        [jnp.concatenate([w_lora_up, z], axis=1), jnp.concatenate([z, a_lora_up], axis=1)], axis=0))
    ya = _rwkv(u3, pvec, mu_wa, lora, _block_ones(d_a), col_r=col_r, col_wa=col_wa, d_a=d_a)

    fb_pad = jnp.pad(f_bias, (0, LANES - h_b)).reshape(1, LANES)
    cum, cumt = _fprep(uf.reshape(B, S, LANES), fb_pad)
    qg2 = jnp.tile(q_norm_g, 2).reshape(1, LANES)
    kg2 = jnp.tile(k_norm_g, 2).reshape(1, LANES)
    yb = _fox(u3, cum, cumt, qg2, kg2, _block_ones(LANES), col_q=col_q, d_b=d_b, tq=256, tk=256)

    return _out(x2, ya.reshape(T, d_a), yb.reshape(T, d_b), u2,
                _bf(w_out_a), _bf(w_out_b), _bf(w_out), out_gain.reshape(1, D), tm=512)


def kernel(x, norm_g, w_in, shift_mu, w_lora_up, w0, a_lora_up, a0, k_k, k_a, r_k, lnx_w, lnx_b,
           f_bias, q_norm_g, k_norm_g, w_out_a, w_out_b, w_out, final_norm_g):
    B, S, D = x.shape
    depth = w_in.shape[0]
    assert depth == 1, "the fused output stage applies the final norm after the single layer"
    x2 = x.reshape(B * S, D)
    out = _layer(x2, B, S, norm_g[0], w_in[0], shift_mu[0], w_lora_up[0], w0[0], a_lora_up[0],
                 a0[0], k_k[0], k_a[0], r_k[0], lnx_w[0], lnx_b[0], f_bias[0], q_norm_g[0],
                 k_norm_g[0], w_out_a[0], w_out_b[0], w_out[0], final_norm_g)
    return out.reshape(B, S, D)
```

```python
import functools

import jax
import jax.numpy as jnp
from jax import lax
from jax.experimental import pallas as pl
from jax.experimental.pallas import tpu as pltpu

F32 = jnp.float32
BF16 = jnp.bfloat16

HEAD = 64
LANES = 128
RMS_EPS = 1e-6
LNX_EPS = 64e-5
CHUNK = 64
GROUP = 4
GW = GROUP * HEAD
VMEM_LIMIT = 56 * 1024 * 1024


def _bf(x):
    return x.astype(BF16)


def _mm(a, b):
    return jnp.dot(_bf(a), _bf(b), preferred_element_type=F32)


def _mm_nt(a, b):
    return lax.dot_general(_bf(a), _bf(b), (((1,), (1,)), ((), ())), preferred_element_type=F32)


def _mm_tn(a, b):
    return lax.dot_general(_bf(a), _bf(b), (((0,), (0,)), ((), ())), preferred_element_type=F32)


def _split2(x):
    hi = _bf(x)
    lo = _bf(x - hi.astype(F32))
    return hi, lo


def _split3(x):
    hi = _bf(x)
    r1 = x - hi.astype(F32)
    mid = _bf(r1)
    lo = _bf(r1 - mid.astype(F32))
    return hi, mid, lo


def _gsum(x, g):
    hi, lo = _split2(x)
    return (jnp.dot(hi, g, preferred_element_type=F32)
            + jnp.dot(lo, g, preferred_element_type=F32))


def _sigmoid(x):
    return 1.0 / (1.0 + jnp.exp(-x))


def _inproj_kernel(x_ref, g_ref, w_ref, o_ref, of_ref, h_scr, *, n_col_tiles):
    j = pl.program_id(1)

    @pl.when(j == 0)
    def _():
        x = x_ref[...]
        ms = jnp.mean(x * x, axis=-1, keepdims=True)
        h_scr[...] = _bf(x * lax.rsqrt(ms + RMS_EPS) * g_ref[...])

    acc = jnp.dot(h_scr[...], w_ref[...], preferred_element_type=F32)
    o_ref[...] = _bf(acc)

    @pl.when(j == n_col_tiles - 1)
    def _():
        of_ref[...] = acc[:, acc.shape[1] - LANES:]


def _inproj(x2, norm_g, w_perm, *, tm, tn):
    T, D = x2.shape
    N = w_perm.shape[1]
    nj = N // tn
    return pl.pallas_call(
        functools.partial(_inproj_kernel, n_col_tiles=nj),
        grid=(T // tm, nj),
        in_specs=[
            pl.BlockSpec((tm, D), lambda i, j: (i, 0)),
            pl.BlockSpec((1, D), lambda i, j: (0, 0)),
            pl.BlockSpec((D, tn), lambda i, j: (0, j)),
        ],
        out_specs=[
            pl.BlockSpec((tm, tn), lambda i, j: (i, j)),
            pl.BlockSpec((tm, LANES), lambda i, j: (i, 0)),
        ],
        out_shape=[
            jax.ShapeDtypeStruct((T, N), BF16),
            jax.ShapeDtypeStruct((T, LANES), F32),
        ],
        scratch_shapes=[pltpu.VMEM((tm, D), BF16)],
        compiler_params=pltpu.CompilerParams(
            dimension_semantics=("parallel", "arbitrary"), vmem_limit_bytes=VMEM_LIMIT),
        name="inproj",
    )(x2, norm_g, w_perm)


def _fprep_kernel(f_ref, fb_ref, cum_ref, cumt_ref, *, blk):
    S = f_ref.shape[1]
    row = lax.broadcasted_iota(jnp.int32, (blk, blk), 0)
    col = lax.broadcasted_iota(jnp.int32, (blk, blk), 1)
    tril = _bf(jnp.where(row >= col, 1.0, 0.0))
    carry = jnp.zeros((1, LANES), F32)
    for i in range(S // blk):
        z = f_ref[0, i * blk:(i + 1) * blk, :] + fb_ref[...]
        lf = jnp.minimum(z, 0.0) - jnp.log1p(jnp.exp(-jnp.abs(z)))
        hi, mid, lo = _split3(lf)
        c = (jnp.dot(tril, hi, preferred_element_type=F32)
             + jnp.dot(tril, mid, preferred_element_type=F32)
             + jnp.dot(tril, lo, preferred_element_type=F32)) + carry
        cum_ref[0, i * blk:(i + 1) * blk, :] = c
        cumt_ref[0, :, i * blk:(i + 1) * blk] = jnp.transpose(c)[0:8, :]
        carry = c[blk - 1:blk, :]


def _fprep(uf3, fb_pad):
    B, S, _ = uf3.shape
    return pl.pallas_call(
        functools.partial(_fprep_kernel, blk=256),
        grid=(B,),
        in_specs=[
            pl.BlockSpec((1, S, LANES), lambda b: (b, 0, 0)),
            pl.BlockSpec((1, LANES), lambda b: (0, 0)),
        ],
        out_specs=[
            pl.BlockSpec((1, S, LANES), lambda b: (b, 0, 0)),
            pl.BlockSpec((1, 8, S), lambda b: (b, 0, 0)),
        ],
        out_shape=[
            jax.ShapeDtypeStruct((B, S, LANES), F32),
            jax.ShapeDtypeStruct((B, 8, S), F32),
        ],
        compiler_params=pltpu.CompilerParams(dimension_semantics=("parallel",)),
        name="fprep",
    )(uf3, fb_pad)


_P_MU_R, _P_MU_K, _P_MU_V, _P_MU_G, _P_W0, _P_A0, _P_KK, _P_KA, _P_RK, _P_LW, _P_LB = range(11)


def _shift_mix(x, carry_row, mu):
    rolled = pltpu.roll(x, shift=1, axis=0)
    row = lax.broadcasted_iota(jnp.int32, x.shape, 0)
    prev = jnp.where(row == 0, carry_row, rolled)
    return x + (prev - x) * mu


def _rwkv_kernel(r_ref, k_ref, v_ref, g_ref, wa_ref, pv_ref, muwa_ref, lora_ref, gm_ref,
                 o_ref, state_scr, carry_scr):
    c = pl.program_id(1)
    L = r_ref.shape[1]
    DA = r_ref.shape[2]

    @pl.when(c == 0)
    def _():
        state_scr[...] = jnp.zeros_like(state_scr)
        carry_scr[...] = jnp.zeros_like(carry_scr)

    def prm(i):
        return pv_ref[i:i + 1, :]

    r_raw = r_ref[0].astype(F32)
    k_raw = k_ref[0].astype(F32)
    v_raw = v_ref[0].astype(F32)
    g_raw = g_ref[0].astype(F32)
    wa_raw = wa_ref[0].astype(F32)

    r = _shift_mix(r_raw, carry_scr[0:1, :], prm(_P_MU_R))
    k = _shift_mix(k_raw, carry_scr[1:2, :], prm(_P_MU_K))
    v = _shift_mix(v_raw, carry_scr[2:3, :], prm(_P_MU_V))
    gate = _shift_mix(g_raw, carry_scr[3:4, :], prm(_P_MU_G))
    wa = _shift_mix(wa_raw, carry_scr[4:5, 0:LANES], muwa_ref[...])

    carry_scr[0:1, :] = r_raw[L - 1:L, :]
    carry_scr[1:2, :] = k_raw[L - 1:L, :]
    carry_scr[2:3, :] = v_raw[L - 1:L, :]
    carry_scr[3:4, :] = g_raw[L - 1:L, :]
    carry_scr[4:5, 0:LANES] = wa_raw[L - 1:L, :]

    lane = lax.broadcasted_iota(jnp.int32, wa.shape, 1)
    wa_act = jnp.where(lane < HEAD, jnp.tanh(wa), wa)
    lo = _mm(wa_act, lora_ref[...])
    zw = -(prm(_P_W0) + lo[:, 0:DA])
    w = -(jnp.maximum(zw, 0.0) + jnp.log1p(jnp.exp(-jnp.abs(zw)))) - 0.5
    ld = -jnp.exp(w)
    a = _sigmoid(prm(_P_A0) + lo[:, DA:2 * DA])

    gm = gm_ref[...]
    kk = k * prm(_P_KK)
    kk = kk / jnp.maximum(jnp.sqrt(_gsum(kk * kk, gm)), 1e-12)
    k2 = k * (1.0 + (a - 1.0) * prm(_P_KA))
    bvec = kk * a

    row = lax.broadcasted_iota(jnp.int32, (L, L), 0)
    col = lax.broadcasted_iota(jnp.int32, (L, L), 1)
    incl = row >= col
    strict = row > col
    tril = _bf(jnp.where(incl, 1.0, 0.0))
    h3, m3, l3 = _split3(ld)
    cum = (jnp.dot(tril, h3, preferred_element_type=F32)
           + jnp.dot(tril, m3, preferred_element_type=F32)
           + jnp.dot(tril, l3, preferred_element_type=F32))
    cum_l = cum[L - 1:L, :]
    e_pos = jnp.exp(cum)
    e_neg = jnp.exp(-cum)
    w_l = jnp.exp(cum_l)

    a_t = -kk * jnp.exp(cum - ld)
    r_t = r * e_pos
    b_t = bvec * e_neg
    k_t = k2 * e_neg
    b_hat = b_t * w_l
    k_hat = k_t * w_l

    n_groups = DA // GW
    prow = lax.broadcasted_iota(jnp.int32, (L, GW), 0)
    pcol = lax.broadcasted_iota(jnp.int32, (L, GW), 1) & (HEAD - 1)
    incl_p = prow >= pcol
    strict_p = prow > pcol
    eye_p = jnp.where(prow == pcol, 1.0, 0.0)
    bd_mask = (lax.broadcasted_iota(jnp.int32, (GROUP * L, GW), 0) // L
               == lax.broadcasted_iota(jnp.int32, (GROUP * L, GW), 1) // HEAD)
    st_mask = (lax.broadcasted_iota(jnp.int32, (GW, GW), 0) // HEAD
               == lax.broadcasted_iota(jnp.int32, (GW, GW), 1) // HEAD)

    def bd(x):
        return _bf(jnp.where(bd_mask, jnp.concatenate([x] * GROUP, axis=0), 0.0))

    gs = [slice(g * GW, (g + 1) * GW) for g in range(n_groups)]
    s0 = [state_scr[g] for g in range(n_groups)]
    s0b = [_bf(s) for s in s0]
    lhs = [jnp.concatenate([a_t[:, s], r_t[:, s]], axis=0) for s in gs]
    rhs = [jnp.concatenate([bd(b_t[:, s]), bd(k_t[:, s])], axis=0) for s in gs]
    pm = [_mm_nt(lhs[g], rhs[g]) for g in range(n_groups)]
    a_ab = [jnp.where(strict_p, p[0:L, 0:GW], 0.0) for p in pm]
    a_ak = [jnp.where(strict_p, p[0:L, GW:2 * GW], 0.0) for p in pm]
    a_rb = [jnp.where(incl_p, p[L:2 * L, 0:GW], 0.0) for p in pm]
    a_rk = [jnp.where(incl_p, p[L:2 * L, GW:2 * GW], 0.0) for p in pm]
    vbd = [bd(v[:, s]) for s in gs]
    tinv = [eye_p + x for x in a_ab]
    pw = a_ab
    n = 2
    while n < L:
        pwb = [bd(x) for x in pw]
        pw = [jnp.dot(_bf(pw[g]), pwb[g], preferred_element_type=F32) for g in range(n_groups)]
        pwb = [bd(x) for x in pw]
        tinv = [tinv[g] + jnp.dot(_bf(tinv[g]), pwb[g], preferred_element_type=F32)
                for g in range(n_groups)]
        n *= 2
    rhs_u = [_mm_nt(a_t[:, gs[g]], s0b[g]) + jnp.dot(_bf(a_ak[g]), vbd[g], preferred_element_type=F32)
             for g in range(n_groups)]
    u = [jnp.dot(_bf(tinv[g]), bd(rhs_u[g]), preferred_element_type=F32)
         for g in range(n_groups)]
    ys = [_mm_nt(r_t[:, gs[g]], s0b[g])
          + jnp.dot(_bf(a_rb[g]), bd(u[g]), preferred_element_type=F32)
          + jnp.dot(_bf(a_rk[g]), vbd[g], preferred_element_type=F32) for g in range(n_groups)]
    for g in range(n_groups):
        upd = _mm_tn(jnp.concatenate([u[g], v[:, gs[g]]], axis=0),
                     jnp.concatenate([b_hat[:, gs[g]], k_hat[:, gs[g]]], axis=0))
        state_scr[g] = s0[g] * w_l[:, gs[g]] + jnp.where(st_mask, upd, 0.0)

    y = jnp.concatenate(ys, axis=1)
    inv_n = 1.0 / HEAD
    mean = _gsum(y, gm) * inv_n
    yc = y - mean
    var = _gsum(yc * yc, gm) * inv_n
    yn = yc * lax.rsqrt(var + LNX_EPS) * prm(_P_LW) + prm(_P_LB)
    bonus = _gsum(r * k2 * prm(_P_RK), gm) * v
    out = (yn + bonus) * (gate * _sigmoid(gate))
    o_ref[0] = _bf(out)


def _rwkv(u3, pvec, mu_wa, lora, gm, *, col_r, col_wa, d_a):
    B, S, _ = u3.shape
    L = CHUNK
    assert L == HEAD and d_a % GW == 0
    cb = col_r // d_a
    blk = lambda off: pl.BlockSpec((1, L, d_a), lambda b, c, off=off: (b, c, cb + off))
    full = lambda arr: pl.BlockSpec(arr.shape, lambda b, c: (0,) * arr.ndim)
    return pl.pallas_call(
        _rwkv_kernel,
        grid=(B, S // L),
        in_specs=[
            blk(0), blk(1), blk(2), blk(3),
            pl.BlockSpec((1, L, LANES), lambda b, c: (b, c, col_wa // LANES)),
            full(pvec), full(mu_wa), full(lora), full(gm),
        ],
        out_specs=pl.BlockSpec((1, L, d_a), lambda b, c: (b, c, 0)),
        out_shape=jax.ShapeDtypeStruct((B, S, d_a), BF16),
        scratch_shapes=[
            pltpu.VMEM((d_a // GW, GW, GW), F32),
            pltpu.VMEM((8, d_a), F32),
        ],
        compiler_params=pltpu.CompilerParams(
            dimension_semantics=("parallel", "arbitrary"), vmem_limit_bytes=VMEM_LIMIT),
        name="rwkv7",
    )(u3, u3, u3, u3, u3, pvec, mu_wa, lora, gm)


def _fox_kernel(q_ref, k_ref, v_ref, g_ref, cq_ref, ct_ref, qg_ref, kg_ref, gm_ref,
                o_ref, kn_scr, *, tq, tk, pair_axis):
    p = pl.program_id(pair_axis)
    qi = pl.program_id(2)
    S = k_ref.shape[1]
    gm = gm_ref[...]
    scale = HEAD ** -0.5

    @pl.when(qi == 0)
    def _():
        for i in range(S // tk):
            kb = k_ref[0, i * tk:(i + 1) * tk, :].astype(F32)
            ms = _gsum(kb * kb, gm) * (1.0 / HEAD)
            kn_scr[i * tk:(i + 1) * tk, :] = _bf(kb * lax.rsqrt(ms + RMS_EPS) * kg_ref[...])

    q = q_ref[0].astype(F32)
    ms = _gsum(q * q, gm) * (1.0 / HEAD)
    qn = q * lax.rsqrt(ms + RMS_EPS) * (qg_ref[...] * scale)
    lane = lax.broadcasted_iota(jnp.int32, (tq, LANES), 1)
    head_of_lane = lane // HEAD
    cq_all = cq_ref[0]

    outs = []
    for hh in range(2):
        qh = _bf(jnp.where(head_of_lane == hh, qn, 0.0))
        hsel = lax.broadcasted_iota(jnp.int32, (tq, LANES), 1) == (2 * p + hh)
        cq = jnp.sum(jnp.where(hsel, cq_all, 0.0), axis=-1, keepdims=True)

        def logits(j):
            ks = pl.multiple_of(j * tk, tk)
            kb = kn_scr[pl.ds(ks, tk), :]
            s = lax.dot_general(qh, kb, (((1,), (1,)), ((), ())), preferred_element_type=F32)
            ck = ct_ref[0, pl.ds(2 * p + hh, 1), pl.ds(ks, tk)]
            return s + cq - ck, ks

        def update(carry, s, ks):
            m, l, acc = carry
            m_new = jnp.maximum(m, jnp.max(s, axis=-1, keepdims=True))
            alpha = jnp.exp(m - m_new)
            pexp = jnp.exp(s - m_new)
            l = alpha * l + jnp.sum(pexp, axis=-1, keepdims=True)
            vb = v_ref[0, pl.ds(ks, tk), :]
            acc = alpha * acc + jnp.dot(_bf(pexp), vb, preferred_element_type=F32)
            return m_new, l, acc

        def body(j, carry):
            s, ks = logits(j)
            return update(carry, s, ks)

        init = (jnp.full((tq, 1), -1e30, F32), jnp.zeros((tq, 1), F32), jnp.zeros((tq, LANES), F32))
        nfull = (qi * tq) // tk
        carry = lax.fori_loop(0, nfull, body, init)
        for d in range(tq // tk):
            j = nfull + d
            s, ks = logits(j)
            rowi = qi * tq + lax.broadcasted_iota(jnp.int32, (tq, tk), 0)
            coli = j * tk + lax.broadcasted_iota(jnp.int32, (tq, tk), 1)
            s = jnp.where(rowi >= coli, s, -1e30)
            carry = update(carry, s, ks)
        m, l, acc = carry
        outs.append(acc / l)

    o = jnp.where(head_of_lane == 0, outs[0], outs[1])
    g = g_ref[0].astype(F32)
    o_ref[0] = _bf(o * (g * _sigmoid(g)))


def _fox(u3, cum, cumt, qg2, kg2, gm128, *, col_q, d_b, tq, tk):
    B, S, _ = u3.shape
    n_pairs = d_b // LANES
    cq, ck, cv, cg = ((col_q + i * d_b) // LANES for i in range(4))
    return pl.pallas_call(
        functools.partial(_fox_kernel, tq=tq, tk=tk, pair_axis=1),
        grid=(B, n_pairs, S // tq),
        in_specs=[
            pl.BlockSpec((1, tq, LANES), lambda b, p, i: (b, i, cq + p)),
            pl.BlockSpec((1, S, LANES), lambda b, p, i: (b, 0, ck + p)),
            pl.BlockSpec((1, S, LANES), lambda b, p, i: (b, 0, cv + p)),
            pl.BlockSpec((1, tq, LANES), lambda b, p, i: (b, i, cg + p)),
            pl.BlockSpec((1, tq, LANES), lambda b, p, i: (b, i, 0)),
            pl.BlockSpec((1, 8, S), lambda b, p, i: (b, 0, 0)),
            pl.BlockSpec((1, LANES), lambda b, p, i: (0, 0)),
            pl.BlockSpec((1, LANES), lambda b, p, i: (0, 0)),
            pl.BlockSpec((LANES, LANES), lambda b, p, i: (0, 0)),
        ],
        out_specs=pl.BlockSpec((1, tq, LANES), lambda b, p, i: (b, i, p)),
        out_shape=jax.ShapeDtypeStruct((B, S, d_b), BF16),
        scratch_shapes=[pltpu.VMEM((S, LANES), BF16)],
        compiler_params=pltpu.CompilerParams(
            dimension_semantics=("parallel", "parallel", "arbitrary"), vmem_limit_bytes=VMEM_LIMIT),
        name="fox",
    )(u3, u3, u3, u3, cum, cumt, qg2, kg2, gm128)


def _out_kernel(x_ref, ya_ref, yb_ref, ga_ref, gb_ref, woa_ref, wob_ref, wo_ref, fg_ref, o_ref):
    za = jnp.dot(ya_ref[...], woa_ref[...], preferred_element_type=F32)
    zb = jnp.dot(yb_ref[...], wob_ref[...], preferred_element_type=F32)
    merged = (_sigmoid(ga_ref[...].astype(F32)) * za + _sigmoid(gb_ref[...].astype(F32)) * zb)
    o = x_ref[...] + jnp.dot(_bf(merged), wo_ref[...], preferred_element_type=F32)
    ms = jnp.mean(o * o, axis=-1, keepdims=True)
    o_ref[...] = o * lax.rsqrt(ms + RMS_EPS) * fg_ref[...]


def _out(x2, ya2, yb2, u2, woa, wob, wo, fg, *, tm):
    T, D = x2.shape
    full = lambda arr: pl.BlockSpec(arr.shape, lambda i: (0,) * arr.ndim)
    return pl.pallas_call(
        _out_kernel,
        grid=(T // tm,),
        in_specs=[
            pl.BlockSpec((tm, D), lambda i: (i, 0)),
            pl.BlockSpec((tm, ya2.shape[1]), lambda i: (i, 0)),
            pl.BlockSpec((tm, yb2.shape[1]), lambda i: (i, 0)),
            pl.BlockSpec((tm, D), lambda i: (i, 0)),
            pl.BlockSpec((tm, D), lambda i: (i, 1)),
            full(woa), full(wob), full(wo), full(fg),
        ],
        out_specs=pl.BlockSpec((tm, D), lambda i: (i, 0)),
        out_shape=jax.ShapeDtypeStruct((T, D), F32),
        compiler_params=pltpu.CompilerParams(
            dimension_semantics=("parallel",), vmem_limit_bytes=VMEM_LIMIT),
        name="outstage",
    )(x2, ya2, yb2, u2, u2, woa, wob, wo, fg)


def _block_ones(width):
    i = jnp.arange(width) // HEAD
    return (i[:, None] == i[None, :]).astype(BF16)


def _layer(x2, B, S, norm_g, w_in, shift_mu, w_lora_up, w0, a_lora_up, a0, k_k, k_a, r_k,
           lnx_w, lnx_b, f_bias, q_norm_g, k_norm_g, w_out_a, w_out_b, w_out, out_gain):
    T, D = x2.shape
    d_a = w0.shape[0]
    d_b = w_out_b.shape[0]
    rank = w_lora_up.shape[0]
    h_b = f_bias.shape[0]
    rw = 4 * d_a + 2 * rank
    fx = 4 * d_b + h_b

    ca = w_in[:, :rw]
    cb = w_in[:, rw:rw + fx]
    cg = w_in[:, rw + fx:]
    wdad = ca[:, 3 * d_a:3 * d_a + 2 * rank]
    flog = jnp.pad(cb[:, 4 * d_b:], ((0, 0), (0, LANES - h_b)))
    w_perm = _bf(jnp.concatenate(
        [cg, ca[:, :3 * d_a], ca[:, 3 * d_a + 2 * rank:], cb[:, :4 * d_b], wdad, flog], axis=1))
    col_r = 2 * D
    col_q = col_r + 4 * d_a
    col_wa = col_q + 4 * d_b

    u2, uf = _inproj(x2, norm_g.reshape(1, D), w_perm, tm=1024, tn=640)
    u3 = u2.reshape(B, S, u2.shape[1])

    mu = shift_mu
    rows = [mu[:d_a], mu[d_a:2 * d_a], mu[2 * d_a:3 * d_a], mu[3 * d_a + 2 * rank:],
            w0, a0, k_k, k_a, r_k.reshape(-1), lnx_w, lnx_b]
    pvec = jnp.stack(rows + [jnp.zeros_like(w0)] * (16 - len(rows)), axis=0)
    mu_wa = mu[3 * d_a:3 * d_a + 2 * rank].reshape(1, 2 * rank)
    z = jnp.zeros((rank, d_a), F32)
    lora = _bf(jnp.concatenate(
        [jnp.concatenate([w_lora_up, z], axis=1), jnp.concatenate([z, a_lora_up], axis=1)], axis=0))
    ya = _rwkv(u3, pvec, mu_wa, lora, _block_ones(d_a), col_r=col_r, col_wa=col_wa, d_a=d_a)

    fb_pad = jnp.pad(f_bias, (0, LANES - h_b)).reshape(1, LANES)
    cum, cumt = _fprep(uf.reshape(B, S, LANES), fb_pad)
    qg2 = jnp.tile(q_norm_g, 2).reshape(1, LANES)
    kg2 = jnp.tile(k_norm_g, 2).reshape(1, LANES)
    yb = _fox(u3, cum, cumt, qg2, kg2, _block_ones(LANES), col_q=col_q, d_b=d_b, tq=256, tk=256)

    return _out(x2, ya.reshape(T, d_a), yb.reshape(T, d_b), u2,
                _bf(w_out_a), _bf(w_out_b), _bf(w_out), out_gain.reshape(1, D), tm=512)


def kernel(x, norm_g, w_in, shift_mu, w_lora_up, w0, a_lora_up, a0, k_k, k_a, r_k, lnx_w, lnx_b,
           f_bias, q_norm_g, k_norm_g, w_out_a, w_out_b, w_out, final_norm_g):
    B, S, D = x.shape
    depth = w_in.shape[0]
    assert depth == 1, "the fused output stage applies the final norm after the single layer"
    x2 = x.reshape(B * S, D)
    out = _layer(x2, B, S, norm_g[0], w_in[0], shift_mu[0], w_lora_up[0], w0[0], a_lora_up[0],
                 a0[0], k_k[0], k_a[0], r_k[0], lnx_w[0], lnx_b[0], f_bias[0], q_norm_g[0],
                 k_norm_g[0], w_out_a[0], w_out_b[0], w_out[0], final_norm_g)
    return out.reshape(B, S, D)
```

```python
import functools

import jax
import jax.numpy as jnp
from jax import lax
from jax.experimental import pallas as pl
from jax.experimental.pallas import tpu as pltpu

F32 = jnp.float32
BF16 = jnp.bfloat16

HEAD = 64
LANES = 128
RMS_EPS = 1e-6
LNX_EPS = 64e-5
CHUNK = 64
GROUP = 4
GW = GROUP * HEAD
FOX_HEADS = 4
VMEM_LIMIT = 56 * 1024 * 1024


def _bf(x):
    return x.astype(BF16)


def _mm(a, b):
    return jnp.dot(_bf(a), _bf(b), preferred_element_type=F32)


def _mm_nt(a, b):
    return lax.dot_general(_bf(a), _bf(b), (((1,), (1,)), ((), ())), preferred_element_type=F32)


def _mm_tn(a, b):
    return lax.dot_general(_bf(a), _bf(b), (((0,), (0,)), ((), ())), preferred_element_type=F32)


def _split2(x):
    hi = _bf(x)
    lo = _bf(x - hi.astype(F32))
    return hi, lo


def _split3(x):
    hi = _bf(x)
    r1 = x - hi.astype(F32)
    mid = _bf(r1)
    lo = _bf(r1 - mid.astype(F32))
    return hi, mid, lo


def _gsum(x, g):
    hi, lo = _split2(x)
    return (jnp.dot(hi, g, preferred_element_type=F32)
            + jnp.dot(lo, g, preferred_element_type=F32))


def _sigmoid(x):
    return 1.0 / (1.0 + jnp.exp(-x))


def _inproj_kernel(x_ref, g_ref, w_ref, o_ref, of_ref, h_scr, *, n_col_tiles):
    j = pl.program_id(1)

    @pl.when(j == 0)
    def _():
        x = x_ref[...]
        ms = jnp.mean(x * x, axis=-1, keepdims=True)
        h_scr[...] = _bf(x * lax.rsqrt(ms + RMS_EPS) * g_ref[...])

    acc = jnp.dot(h_scr[...], w_ref[...], preferred_element_type=F32)
    o_ref[...] = _bf(acc)

    @pl.when(j == n_col_tiles - 1)
    def _():
        of_ref[...] = acc[:, acc.shape[1] - LANES:]


def _inproj(x2, norm_g, w_perm, *, tm, tn):
    T, D = x2.shape
    N = w_perm.shape[1]
    nj = N // tn
    return pl.pallas_call(
        functools.partial(_inproj_kernel, n_col_tiles=nj),
        grid=(T // tm, nj),
        in_specs=[
            pl.BlockSpec((tm, D), lambda i, j: (i, 0)),
            pl.BlockSpec((1, D), lambda i, j: (0, 0)),
            pl.BlockSpec((D, tn), lambda i, j: (0, j)),
        ],
        out_specs=[
            pl.BlockSpec((tm, tn), lambda i, j: (i, j)),
            pl.BlockSpec((tm, LANES), lambda i, j: (i, 0)),
        ],
        out_shape=[
            jax.ShapeDtypeStruct((T, N), BF16),
            jax.ShapeDtypeStruct((T, LANES), F32),
        ],
        scratch_shapes=[pltpu.VMEM((tm, D), BF16)],
        compiler_params=pltpu.CompilerParams(
            dimension_semantics=("parallel", "arbitrary"), vmem_limit_bytes=VMEM_LIMIT),
        name="inproj",
    )(x2, norm_g, w_perm)


def _fprep_kernel(f_ref, fb_ref, cum_ref, cumt_ref, *, blk):
    S = f_ref.shape[1]
    row = lax.broadcasted_iota(jnp.int32, (blk, blk), 0)
    col = lax.broadcasted_iota(jnp.int32, (blk, blk), 1)
    tril = _bf(jnp.where(row >= col, 1.0, 0.0))
    carry = jnp.zeros((1, LANES), F32)
    for i in range(S // blk):
        z = f_ref[0, i * blk:(i + 1) * blk, :] + fb_ref[...]
        lf = jnp.minimum(z, 0.0) - jnp.log1p(jnp.exp(-jnp.abs(z)))
        hi, mid, lo = _split3(lf)
        c = (jnp.dot(tril, hi, preferred_element_type=F32)
             + jnp.dot(tril, mid, preferred_element_type=F32)
             + jnp.dot(tril, lo, preferred_element_type=F32)) + carry
        cum_ref[0, i * blk:(i + 1) * blk, :] = c
        cumt_ref[0, :, i * blk:(i + 1) * blk] = jnp.transpose(c)[0:8, :]
        carry = c[blk - 1:blk, :]


def _fprep(uf3, fb_pad):
    B, S, _ = uf3.shape
    return pl.pallas_call(
        functools.partial(_fprep_kernel, blk=256),
        grid=(B,),
        in_specs=[
            pl.BlockSpec((1, S, LANES), lambda b: (b, 0, 0)),
            pl.BlockSpec((1, LANES), lambda b: (0, 0)),
        ],
        out_specs=[
            pl.BlockSpec((1, S, LANES), lambda b: (b, 0, 0)),
            pl.BlockSpec((1, 8, S), lambda b: (b, 0, 0)),
        ],
        out_shape=[
            jax.ShapeDtypeStruct((B, S, LANES), F32),
            jax.ShapeDtypeStruct((B, 8, S), F32),
        ],
        compiler_params=pltpu.CompilerParams(dimension_semantics=("parallel",)),
        name="fprep",
    )(uf3, fb_pad)


_P_MU_R, _P_MU_K, _P_MU_V, _P_MU_G, _P_W0, _P_A0, _P_KK, _P_KA, _P_RK, _P_LW, _P_LB = range(11)


def _shift_mix(x, carry_row, mu):
    rolled = pltpu.roll(x, shift=1, axis=0)
    row = lax.broadcasted_iota(jnp.int32, x.shape, 0)
    prev = jnp.where(row == 0, carry_row, rolled)
    return x + (prev - x) * mu


def _rwkv_kernel(r_ref, k_ref, v_ref, g_ref, wa_ref, pv_ref, muwa_ref, lora_ref, gm_ref,
                 o_ref, state_scr, carry_scr):
    c = pl.program_id(1)
    L = r_ref.shape[1]
    DA = r_ref.shape[2]

    @pl.when(c == 0)
    def _():
        state_scr[...] = jnp.zeros_like(state_scr)
        carry_scr[...] = jnp.zeros_like(carry_scr)

    def prm(i):
        return pv_ref[i:i + 1, :]

    r_raw = r_ref[0].astype(F32)
    k_raw = k_ref[0].astype(F32)
    v_raw = v_ref[0].astype(F32)
    g_raw = g_ref[0].astype(F32)
    wa_raw = wa_ref[0].astype(F32)

    r = _shift_mix(r_raw, carry_scr[0:1, :], prm(_P_MU_R))
    k = _shift_mix(k_raw, carry_scr[1:2, :], prm(_P_MU_K))
    v = _shift_mix(v_raw, carry_scr[2:3, :], prm(_P_MU_V))
    gate = _shift_mix(g_raw, carry_scr[3:4, :], prm(_P_MU_G))
    wa = _shift_mix(wa_raw, carry_scr[4:5, 0:LANES], muwa_ref[...])

    carry_scr[0:1, :] = r_raw[L - 1:L, :]
    carry_scr[1:2, :] = k_raw[L - 1:L, :]
    carry_scr[2:3, :] = v_raw[L - 1:L, :]
    carry_scr[3:4, :] = g_raw[L - 1:L, :]
    carry_scr[4:5, 0:LANES] = wa_raw[L - 1:L, :]

    lane = lax.broadcasted_iota(jnp.int32, wa.shape, 1)
    wa_act = jnp.where(lane < HEAD, jnp.tanh(wa), wa)
    lo = _mm(wa_act, lora_ref[...])
    zw = -(prm(_P_W0) + lo[:, 0:DA])
    w = -(jnp.maximum(zw, 0.0) + jnp.log1p(jnp.exp(-jnp.abs(zw)))) - 0.5
    ld = -jnp.exp(w)
    a = _sigmoid(prm(_P_A0) + lo[:, DA:2 * DA])

    gm = gm_ref[...]
    kk = k * prm(_P_KK)
    kk = kk / jnp.maximum(jnp.sqrt(_gsum(kk * kk, gm)), 1e-12)
    k2 = k * (1.0 + (a - 1.0) * prm(_P_KA))
    bvec = kk * a

    row = lax.broadcasted_iota(jnp.int32, (L, L), 0)
    col = lax.broadcasted_iota(jnp.int32, (L, L), 1)
    incl = row >= col
    strict = row > col
    tril = _bf(jnp.where(incl, 1.0, 0.0))
    h3, m3, l3 = _split3(ld)
    cum = (jnp.dot(tril, h3, preferred_element_type=F32)
           + jnp.dot(tril, m3, preferred_element_type=F32)
           + jnp.dot(tril, l3, preferred_element_type=F32))
    cum_l = cum[L - 1:L, :]
    e_pos = jnp.exp(cum)
    e_neg = jnp.exp(-cum)
    w_l = jnp.exp(cum_l)

    a_t = -kk * jnp.exp(cum - ld)
    r_t = r * e_pos
    b_t = bvec * e_neg
    k_t = k2 * e_neg
    b_hat = b_t * w_l
    k_hat = k_t * w_l

    n_groups = DA // GW
    prow = lax.broadcasted_iota(jnp.int32, (L, GW), 0)
    pcol = lax.broadcasted_iota(jnp.int32, (L, GW), 1) & (HEAD - 1)
    incl_p = prow >= pcol
    strict_p = prow > pcol
    eye_p = jnp.where(prow == pcol, 1.0, 0.0)
    bd_mask = (lax.broadcasted_iota(jnp.int32, (GROUP * L, GW), 0) // L
               == lax.broadcasted_iota(jnp.int32, (GROUP * L, GW), 1) // HEAD)
    st_mask = (lax.broadcasted_iota(jnp.int32, (GW, GW), 0) // HEAD
               == lax.broadcasted_iota(jnp.int32, (GW, GW), 1) // HEAD)

    def bd(x):
        return _bf(jnp.where(bd_mask, jnp.concatenate([x] * GROUP, axis=0), 0.0))

    gs = [slice(g * GW, (g + 1) * GW) for g in range(n_groups)]
    s0 = [state_scr[g] for g in range(n_groups)]
    s0b = [_bf(s) for s in s0]
    lhs = [jnp.concatenate([a_t[:, s], r_t[:, s]], axis=0) for s in gs]
    rhs = [jnp.concatenate([bd(b_t[:, s]), bd(k_t[:, s])], axis=0) for s in gs]
    pm = [_mm_nt(lhs[g], rhs[g]) for g in range(n_groups)]
    a_ab = [jnp.where(strict_p, p[0:L, 0:GW], 0.0) for p in pm]
    a_ak = [jnp.where(strict_p, p[0:L, GW:2 * GW], 0.0) for p in pm]
    a_rb = [jnp.where(incl_p, p[L:2 * L, 0:GW], 0.0) for p in pm]
    a_rk = [jnp.where(incl_p, p[L:2 * L, GW:2 * GW], 0.0) for p in pm]
    vbd = [bd(v[:, s]) for s in gs]
    tinv = [eye_p + x for x in a_ab]
    pw = a_ab
    n = 2
    while n < L:
        pwb = [bd(x) for x in pw]
        pw = [jnp.dot(_bf(pw[g]), pwb[g], preferred_element_type=F32) for g in range(n_groups)]
        pwb = [bd(x) for x in pw]
        tinv = [tinv[g] + jnp.dot(_bf(tinv[g]), pwb[g], preferred_element_type=F32)
                for g in range(n_groups)]
        n *= 2
    rhs_u = [_mm_nt(a_t[:, gs[g]], s0b[g]) + jnp.dot(_bf(a_ak[g]), vbd[g], preferred_element_type=F32)
             for g in range(n_groups)]
    u = [jnp.dot(_bf(tinv[g]), bd(rhs_u[g]), preferred_element_type=F32)
         for g in range(n_groups)]
    ys = [_mm_nt(r_t[:, gs[g]], s0b[g])
          + jnp.dot(_bf(a_rb[g]), bd(u[g]), preferred_element_type=F32)
          + jnp.dot(_bf(a_rk[g]), vbd[g], preferred_element_type=F32) for g in range(n_groups)]
    for g in range(n_groups):
        upd = _mm_tn(jnp.concatenate([u[g], v[:, gs[g]]], axis=0),
                     jnp.concatenate([b_hat[:, gs[g]], k_hat[:, gs[g]]], axis=0))
        state_scr[g] = s0[g] * w_l[:, gs[g]] + jnp.where(st_mask, upd, 0.0)

    y = jnp.concatenate(ys, axis=1)
    inv_n = 1.0 / HEAD
    mean = _gsum(y, gm) * inv_n
    yc = y - mean
    var = _gsum(yc * yc, gm) * inv_n
    yn = yc * lax.rsqrt(var + LNX_EPS) * prm(_P_LW) + prm(_P_LB)
    bonus = _gsum(r * k2 * prm(_P_RK), gm) * v
    out = (yn + bonus) * (gate * _sigmoid(gate))
    o_ref[0] = _bf(out)


def _rwkv(u3, pvec, mu_wa, lora, gm, *, col_r, col_wa, d_a):
    B, S, _ = u3.shape
    L = CHUNK
    assert L == HEAD and d_a % GW == 0
    cb = col_r // d_a
    blk = lambda off: pl.BlockSpec((1, L, d_a), lambda b, c, off=off: (b, c, cb + off))
    full = lambda arr: pl.BlockSpec(arr.shape, lambda b, c: (0,) * arr.ndim)
    return pl.pallas_call(
        _rwkv_kernel,
        grid=(B, S // L),
        in_specs=[
            blk(0), blk(1), blk(2), blk(3),
            pl.BlockSpec((1, L, LANES), lambda b, c: (b, c, col_wa // LANES)),
            full(pvec), full(mu_wa), full(lora), full(gm),
        ],
        out_specs=pl.BlockSpec((1, L, d_a), lambda b, c: (b, c, 0)),
        out_shape=jax.ShapeDtypeStruct((B, S, d_a), BF16),
        scratch_shapes=[
            pltpu.VMEM((d_a // GW, GW, GW), F32),
            pltpu.VMEM((8, d_a), F32),
        ],
        compiler_params=pltpu.CompilerParams(
            dimension_semantics=("parallel", "arbitrary"), vmem_limit_bytes=VMEM_LIMIT),
        name="rwkv7",
    )(u3, u3, u3, u3, u3, pvec, mu_wa, lora, gm)


def _fox_kernel(q_ref, k_ref, v_ref, g_ref, cq_ref, ct_ref, qg_ref, kg_ref, gm_ref,
                o_ref, kn_scr, *, tq, tk, pair_axis):
    p = pl.program_id(pair_axis)
    qi = pl.program_id(2)
    S = k_ref.shape[1]
    gm = gm_ref[...]
    scale = HEAD ** -0.5

    @pl.when(qi == 0)
    def _():
        for i in range(S // tk):
            kb = k_ref[0, i * tk:(i + 1) * tk, :].astype(F32)
            ms = _gsum(kb * kb, gm) * (1.0 / HEAD)
            kn_scr[i * tk:(i + 1) * tk, :] = _bf(kb * lax.rsqrt(ms + RMS_EPS) * kg_ref[...])

    q = q_ref[0].astype(F32)
    ms = _gsum(q * q, gm) * (1.0 / HEAD)
    qn = q * lax.rsqrt(ms + RMS_EPS) * (qg_ref[...] * scale)
    lane = lax.broadcasted_iota(jnp.int32, (tq, LANES), 1)
    head_of_lane = lane // HEAD
    cq_all = cq_ref[0]
    nh = q.shape[1] // HEAD
    tiles = [slice((hh // 2) * LANES, (hh // 2 + 1) * LANES) for hh in range(nh)]

    qh = [_bf(jnp.where(head_of_lane == hh % 2, qn[:, tiles[hh]], 0.0)) for hh in range(nh)]
    cq = [jnp.sum(jnp.where(lane == nh * p + hh, cq_all, 0.0), axis=-1, keepdims=True)
          for hh in range(nh)]

    def step(j, carry, masked):
        ks = pl.multiple_of(j * tk, tk)
        s = [lax.dot_general(qh[hh], kn_scr[pl.ds(ks, tk), tiles[hh]], (((1,), (1,)), ((), ())),
                             preferred_element_type=F32) for hh in range(nh)]
        s = [s[hh] + cq[hh] - ct_ref[0, pl.ds(nh * p + hh, 1), pl.ds(ks, tk)] for hh in range(nh)]
        if masked:
            rowi = qi * tq + lax.broadcasted_iota(jnp.int32, (tq, tk), 0)
            coli = j * tk + lax.broadcasted_iota(jnp.int32, (tq, tk), 1)
            s = [jnp.where(rowi >= coli, x, -1e30) for x in s]
        m_new = [jnp.maximum(carry[hh][0], jnp.max(s[hh], axis=-1, keepdims=True)) for hh in range(nh)]
        alpha = [jnp.exp(carry[hh][0] - m_new[hh]) for hh in range(nh)]
        pexp = [jnp.exp(s[hh] - m_new[hh]) for hh in range(nh)]
        l_new = [alpha[hh] * carry[hh][1] + jnp.sum(pexp[hh], axis=-1, keepdims=True)
                 for hh in range(nh)]
        pv = [jnp.dot(_bf(pexp[hh]), v_ref[0, pl.ds(ks, tk), tiles[hh]], preferred_element_type=F32)
              for hh in range(nh)]
        return tuple((m_new[hh], l_new[hh], alpha[hh] * carry[hh][2] + pv[hh]) for hh in range(nh))

    init = tuple((jnp.full((tq, 1), -1e30, F32), jnp.zeros((tq, 1), F32),
                  jnp.zeros((tq, LANES), F32)) for _ in range(nh))
    nfull = (qi * tq) // tk
    carry = lax.fori_loop(0, nfull, lambda j, c: step(j, c, False), init)
    for d in range(tq // tk):
        carry = step(nfull + d, carry, True)

    outs = [acc / l for (_, l, acc) in carry]
    o = jnp.concatenate([jnp.where(head_of_lane == 0, outs[2 * t], outs[2 * t + 1])
                         for t in range(nh // 2)], axis=1)
    g = g_ref[0].astype(F32)
    o_ref[0] = _bf(o * (g * _sigmoid(g)))


def _fox(u3, cum, cumt, qg2, kg2, gm, *, col_q, d_b, tq, tk):
    B, S, _ = u3.shape
    fw = gm.shape[0]
    cq, ck, cv, cg = ((col_q + i * d_b) // fw for i in range(4))
    return pl.pallas_call(
        functools.partial(_fox_kernel, tq=tq, tk=tk, pair_axis=1),
        grid=(B, d_b // fw, S // tq),
        in_specs=[
            pl.BlockSpec((1, tq, fw), lambda b, p, i: (b, i, cq + p)),
            pl.BlockSpec((1, S, fw), lambda b, p, i: (b, 0, ck + p)),
            pl.BlockSpec((1, S, fw), lambda b, p, i: (b, 0, cv + p)),
            pl.BlockSpec((1, tq, fw), lambda b, p, i: (b, i, cg + p)),
            pl.BlockSpec((1, tq, LANES), lambda b, p, i: (b, i, 0)),
            pl.BlockSpec((1, 8, S), lambda b, p, i: (b, 0, 0)),
            pl.BlockSpec((1, fw), lambda b, p, i: (0, 0)),
            pl.BlockSpec((1, fw), lambda b, p, i: (0, 0)),
            pl.BlockSpec((fw, fw), lambda b, p, i: (0, 0)),
        ],
        out_specs=pl.BlockSpec((1, tq, fw), lambda b, p, i: (b, i, p)),
        out_shape=jax.ShapeDtypeStruct((B, S, d_b), BF16),
        scratch_shapes=[pltpu.VMEM((S, fw), BF16)],
        compiler_params=pltpu.CompilerParams(
            dimension_semantics=("parallel", "parallel", "arbitrary"), vmem_limit_bytes=VMEM_LIMIT),
        name="fox",
    )(u3, u3, u3, u3, cum, cumt, qg2, kg2, gm)


def _out_kernel(x_ref, ya_ref, yb_ref, ga_ref, gb_ref, woa_ref, wob_ref, wo_ref, fg_ref, o_ref):
    za = jnp.dot(ya_ref[...], woa_ref[...], preferred_element_type=F32)
    zb = jnp.dot(yb_ref[...], wob_ref[...], preferred_element_type=F32)
    merged = (_sigmoid(ga_ref[...].astype(F32)) * za + _sigmoid(gb_ref[...].astype(F32)) * zb)
    o = x_ref[...] + jnp.dot(_bf(merged), wo_ref[...], preferred_element_type=F32)
    ms = jnp.mean(o * o, axis=-1, keepdims=True)
    o_ref[...] = o * lax.rsqrt(ms + RMS_EPS) * fg_ref[...]


def _out(x2, ya2, yb2, u2, woa, wob, wo, fg, *, tm):
    T, D = x2.shape
    full = lambda arr: pl.BlockSpec(arr.shape, lambda i: (0,) * arr.ndim)
    return pl.pallas_call(
        _out_kernel,
        grid=(T // tm,),
        in_specs=[
            pl.BlockSpec((tm, D), lambda i: (i, 0)),
            pl.BlockSpec((tm, ya2.shape[1]), lambda i: (i, 0)),
            pl.BlockSpec((tm, yb2.shape[1]), lambda i: (i, 0)),
            pl.BlockSpec((tm, D), lambda i: (i, 0)),
            pl.BlockSpec((tm, D), lambda i: (i, 1)),
            full(woa), full(wob), full(wo), full(fg),
        ],
        out_specs=pl.BlockSpec((tm, D), lambda i: (i, 0)),
        out_shape=jax.ShapeDtypeStruct((T, D), F32),
        compiler_params=pltpu.CompilerParams(
            dimension_semantics=("parallel",), vmem_limit_bytes=VMEM_LIMIT),
        name="outstage",
    )(x2, ya2, yb2, u2, u2, woa, wob, wo, fg)


def _block_ones(width):
    i = jnp.arange(width) // HEAD
    return (i[:, None] == i[None, :]).astype(BF16)


def _layer(x2, B, S, norm_g, w_in, shift_mu, w_lora_up, w0, a_lora_up, a0, k_k, k_a, r_k,
           lnx_w, lnx_b, f_bias, q_norm_g, k_norm_g, w_out_a, w_out_b, w_out, out_gain):
    T, D = x2.shape
    d_a = w0.shape[0]
    d_b = w_out_b.shape[0]
    rank = w_lora_up.shape[0]
    h_b = f_bias.shape[0]
    rw = 4 * d_a + 2 * rank
    fx = 4 * d_b + h_b

    ca = w_in[:, :rw]
    cb = w_in[:, rw:rw + fx]
    cg = w_in[:, rw + fx:]
    wdad = ca[:, 3 * d_a:3 * d_a + 2 * rank]
    flog = jnp.pad(cb[:, 4 * d_b:], ((0, 0), (0, LANES - h_b)))
    w_perm = _bf(jnp.concatenate(
        [cg, ca[:, :3 * d_a], ca[:, 3 * d_a + 2 * rank:], cb[:, :4 * d_b], wdad, flog], axis=1))
    col_r = 2 * D
    col_q = col_r + 4 * d_a
    col_wa = col_q + 4 * d_b

    u2, uf = _inproj(x2, norm_g.reshape(1, D), w_perm, tm=1024, tn=640)
    u3 = u2.reshape(B, S, u2.shape[1])

    mu = shift_mu
    rows = [mu[:d_a], mu[d_a:2 * d_a], mu[2 * d_a:3 * d_a], mu[3 * d_a + 2 * rank:],
            w0, a0, k_k, k_a, r_k.reshape(-1), lnx_w, lnx_b]
    pvec = jnp.stack(rows + [jnp.zeros_like(w0)] * (16 - len(rows)), axis=0)
    mu_wa = mu[3 * d_a:3 * d_a + 2 * rank].reshape(1, 2 * rank)
    z = jnp.zeros((rank, d_a), F32)
    lora = _bf(jnp.concatenate(
        [jnp.concatenate([w_lora_up, z], axis=1), jnp.concatenate([z, a_lora_up], axis=1)], axis=0))
    ya = _rwkv(u3, pvec, mu_wa, lora, _block_ones(d_a), col_r=col_r, col_wa=col_wa, d_a=d_a)

    fb_pad = jnp.pad(f_bias, (0, LANES - h_b)).reshape(1, LANES)
    cum, cumt = _fprep(uf.reshape(B, S, LANES), fb_pad)
    qg2 = jnp.tile(q_norm_g, FOX_HEADS).reshape(1, FOX_HEADS * HEAD)
    kg2 = jnp.tile(k_norm_g, FOX_HEADS).reshape(1, FOX_HEADS * HEAD)
    yb = _fox(u3, cum, cumt, qg2, kg2, _block_ones(FOX_HEADS * HEAD), col_q=col_q, d_b=d_b,
              tq=256, tk=256)

    return _out(x2, ya.reshape(T, d_a), yb.reshape(T, d_b), u2,
                _bf(w_out_a), _bf(w_out_b), _bf(w_out), out_gain.reshape(1, D), tm=512)


def kernel(x, norm_g, w_in, shift_mu, w_lora_up, w0, a_lora_up, a0, k_k, k_a, r_k, lnx_w, lnx_b,
           f_bias, q_norm_g, k_norm_g, w_out_a, w_out_b, w_out, final_norm_g):
    B, S, D = x.shape
    depth = w_in.shape[0]
    assert depth == 1, "the fused output stage applies the final norm after the single layer"
    x2 = x.reshape(B * S, D)
    out = _layer(x2, B, S, norm_g[0], w_in[0], shift_mu[0], w_lora_up[0], w0[0], a_lora_up[0],
                 a0[0], k_k[0], k_a[0], r_k[0], lnx_w[0], lnx_b[0], f_bias[0], q_norm_g[0],
                 k_norm_g[0], w_out_a[0], w_out_b[0], w_out[0], final_norm_g)
    return out.reshape(B, S, D)
```

```python
import functools

import jax
import jax.numpy as jnp
from jax import lax
from jax.experimental import pallas as pl
from jax.experimental.pallas import tpu as pltpu

F32 = jnp.float32
BF16 = jnp.bfloat16

HEAD = 64
LANES = 128
RMS_EPS = 1e-6
LNX_EPS = 64e-5
CHUNK = 64
GROUP = 4
GW = GROUP * HEAD
RWKV_NB = 4
FOX_HEADS = 4
VMEM_LIMIT = 56 * 1024 * 1024


def _bf(x):
    return x.astype(BF16)


def _dot(a, b):
    return jnp.dot(a, b, preferred_element_type=F32)


def _dot_nt(a, b):
    return lax.dot_general(a, b, (((1,), (1,)), ((), ())), preferred_element_type=F32)


def _dot_tn(a, b):
    return lax.dot_general(a, b, (((0,), (0,)), ((), ())), preferred_element_type=F32)


def _split2(x):
    hi = _bf(x)
    lo = _bf(x - hi.astype(F32))
    return hi, lo


def _split3(x):
    hi = _bf(x)
    r1 = x - hi.astype(F32)
    mid = _bf(r1)
    lo = _bf(r1 - mid.astype(F32))
    return hi, mid, lo


def _gsum(x, g):
    hi, lo = _split2(x)
    return _dot(hi, g) + _dot(lo, g)


def _head_sums(xs, g128):
    m, w = xs[0].shape
    nt = w // LANES
    parts = []
    for x in xs:
        hi, lo = _split2(x)
        parts += [hi[:, t * LANES:(t + 1) * LANES] for t in range(nt)]
        parts += [lo[:, t * LANES:(t + 1) * LANES] for t in range(nt)]
    r = _dot(jnp.concatenate(parts, axis=0), g128)
    outs = []
    for i in range(len(xs)):
        base = 2 * nt * i
        outs.append(jnp.concatenate(
            [r[(base + t) * m:(base + t + 1) * m] + r[(base + nt + t) * m:(base + nt + t + 1) * m]
             for t in range(nt)], axis=1))
    return outs


def _sigmoid(x):
    return 1.0 / (1.0 + jnp.exp(-x))


def _inproj_kernel(x_ref, g_ref, w_ref, o_ref, of_ref, h_scr, *, n_col_tiles):
    j = pl.program_id(1)

    @pl.when(j == 0)
    def _():
        x = x_ref[...]
        ms = jnp.mean(x * x, axis=-1, keepdims=True)
        h_scr[...] = _bf(x * lax.rsqrt(ms + RMS_EPS) * g_ref[...])

    acc = _dot(h_scr[...], w_ref[...])
    o_ref[...] = _bf(acc)

    @pl.when(j == n_col_tiles - 1)
    def _():
        of_ref[...] = acc[:, acc.shape[1] - LANES:]


def _inproj(x2, norm_g, w_perm, *, tm, tn):
    T, D = x2.shape
    N = w_perm.shape[1]
    nj = N // tn
    return pl.pallas_call(
        functools.partial(_inproj_kernel, n_col_tiles=nj),
        grid=(T // tm, nj),
        in_specs=[
            pl.BlockSpec((tm, D), lambda i, j: (i, 0)),
            pl.BlockSpec((1, D), lambda i, j: (0, 0)),
            pl.BlockSpec((D, tn), lambda i, j: (0, j)),
        ],
        out_specs=[
            pl.BlockSpec((tm, tn), lambda i, j: (i, j)),
            pl.BlockSpec((tm, LANES), lambda i, j: (i, 0)),
        ],
        out_shape=[
            jax.ShapeDtypeStruct((T, N), BF16),
            jax.ShapeDtypeStruct((T, LANES), F32),
        ],
        scratch_shapes=[pltpu.VMEM((tm, D), BF16)],
        compiler_params=pltpu.CompilerParams(
            dimension_semantics=("parallel", "arbitrary"), vmem_limit_bytes=VMEM_LIMIT),
        name="inproj",
    )(x2, norm_g, w_perm)


def _fprep_kernel(f_ref, fb_ref, cum_ref, cumt_ref, *, blk):
    S = f_ref.shape[1]
    row = lax.broadcasted_iota(jnp.int32, (blk, blk), 0)
    col = lax.broadcasted_iota(jnp.int32, (blk, blk), 1)
    tril = _bf(jnp.where(row >= col, 1.0, 0.0))
    carry = jnp.zeros((1, LANES), F32)
    for i in range(S // blk):
        z = f_ref[0, i * blk:(i + 1) * blk, :] + fb_ref[...]
        lf = jnp.minimum(z, 0.0) - jnp.log1p(jnp.exp(-jnp.abs(z)))
        hi, mid, lo = _split3(lf)
        c = (_dot(tril, hi) + _dot(tril, mid) + _dot(tril, lo)) + carry
        cum_ref[0, i * blk:(i + 1) * blk, :] = c
        cumt_ref[0, :, i * blk:(i + 1) * blk] = jnp.transpose(c)[0:8, :]
        carry = c[blk - 1:blk, :]


def _fprep(uf3, fb_pad):
    B, S, _ = uf3.shape
    return pl.pallas_call(
        functools.partial(_fprep_kernel, blk=256),
        grid=(B,),
        in_specs=[
            pl.BlockSpec((1, S, LANES), lambda b: (b, 0, 0)),
            pl.BlockSpec((1, LANES), lambda b: (0, 0)),
        ],
        out_specs=[
            pl.BlockSpec((1, S, LANES), lambda b: (b, 0, 0)),
            pl.BlockSpec((1, 8, S), lambda b: (b, 0, 0)),
        ],
        out_shape=[
            jax.ShapeDtypeStruct((B, S, LANES), F32),
            jax.ShapeDtypeStruct((B, 8, S), F32),
        ],
        compiler_params=pltpu.CompilerParams(dimension_semantics=("parallel",)),
        name="fprep",
    )(uf3, fb_pad)


_P_MU_R, _P_MU_K, _P_MU_V, _P_MU_G, _P_W0, _P_A0, _P_KK, _P_KA, _P_RK, _P_LW, _P_LB = range(11)


def _shift_mix(x, carry_row, mu):
    rolled = pltpu.roll(x, shift=1, axis=0)
    row = lax.broadcasted_iota(jnp.int32, x.shape, 0)
    prev = jnp.where(row == 0, carry_row, rolled)
    return x + (prev - x) * mu


def _block_diag(x, half_masks):
    xb = _bf(x)
    n_tiles = xb.shape[1] // LANES
    zero = jnp.zeros((xb.shape[0], LANES), BF16)
    rows = []
    for h in range(xb.shape[1] // HEAD):
        t = h // 2
        piece = xb[:, t * LANES:(t + 1) * LANES] * half_masks[h % 2]
        rows.append(jnp.concatenate([piece if i == t else zero for i in range(n_tiles)], axis=1))
    return jnp.concatenate(rows, axis=0)


def _rwkv_kernel(r_ref, k_ref, v_ref, g_ref, wa_ref, pv_ref, muwa_ref, lora_ref, gm_ref,
                 o_ref, state_scr, carry_scr):
    c = pl.program_id(1)
    NB = r_ref.shape[0]
    L = r_ref.shape[1]
    DA = r_ref.shape[2]
    n_groups = DA // GW

    @pl.when(c == 0)
    def _():
        state_scr[...] = jnp.zeros_like(state_scr)
        carry_scr[...] = jnp.zeros_like(carry_scr)

    def prm(i):
        return pv_ref[i:i + 1, :]

    g128 = gm_ref[...]
    row = lax.broadcasted_iota(jnp.int32, (L, L), 0)
    col = lax.broadcasted_iota(jnp.int32, (L, L), 1)
    tril = _bf(jnp.where(row >= col, 1.0, 0.0))
    lane128 = lax.broadcasted_iota(jnp.int32, (1, LANES), 1)
    half_masks = [_bf(jnp.where(lane128 // HEAD == i, 1.0, 0.0)) for i in range(2)]
    prow = lax.broadcasted_iota(jnp.int32, (L, GW), 0)
    pcol = lax.broadcasted_iota(jnp.int32, (L, GW), 1) & (HEAD - 1)
    incl_p = prow >= pcol
    strict_p = prow > pcol
    eye_p = jnp.where(prow == pcol, 1.0, 0.0)
    st_mask = (lax.broadcasted_iota(jnp.int32, (GW, GW), 0) // HEAD
               == lax.broadcasted_iota(jnp.int32, (GW, GW), 1) // HEAD)
    bd = lambda x: _block_diag(x, half_masks)

    pre = []
    for bb in range(NB):
        raw = [ref[bb].astype(F32) for ref in (r_ref, k_ref, v_ref, g_ref, wa_ref)]
        mus = [prm(_P_MU_R), prm(_P_MU_K), prm(_P_MU_V), prm(_P_MU_G), muwa_ref[...]]
        mixed = []
        for i, x in enumerate(raw):
            wdt = x.shape[1]
            mixed.append(_shift_mix(x, carry_scr[bb, i:i + 1, 0:wdt], mus[i]))
            carry_scr[bb, i:i + 1, 0:wdt] = x[L - 1:L, :]
        r, k, v, gate, wa = mixed

        wa_act = jnp.where(lane128 < HEAD, jnp.tanh(wa), wa)
        lo = _dot(_bf(wa_act), lora_ref[...])
        zw = -(prm(_P_W0) + lo[:, 0:DA])
        w = -(jnp.maximum(zw, 0.0) + jnp.log(1.0 + jnp.exp(-jnp.abs(zw)))) - 0.5
        ld = -jnp.exp(w)
        a = _sigmoid(prm(_P_A0) + lo[:, DA:2 * DA])

        kk = k * prm(_P_KK)
        k2 = k * (1.0 + (a - 1.0) * prm(_P_KA))
        ssq, bon = _head_sums([kk * kk, r * k2 * prm(_P_RK)], g128)
        kk = kk / jnp.maximum(jnp.sqrt(ssq), 1e-12)

        h3, m3, l3 = _split3(ld)
        cum = _dot(tril, h3) + _dot(tril, m3) + _dot(tril, l3)
        e_neg = jnp.exp(-cum)
        w_l = jnp.exp(cum[L - 1:L, :])
        a_t = -kk * jnp.exp(cum - ld)
        r_t = r * jnp.exp(cum)
        b_t = (kk * a) * e_neg
        k_t = k2 * e_neg
        pre.append(dict(v=v, gate=gate, bonus=bon * v, w_l=w_l, a_t=a_t, r_t=r_t, b_t=b_t, k_t=k_t))

    streams = [(bb, g) for bb in range(NB) for g in range(n_groups)]
    gsl = lambda g: slice(g * GW, (g + 1) * GW)
    P = lambda name, st: pre[st[0]][name][:, gsl(st[1])]
    sidx = lambda st: st[0] * n_groups + st[1]

    s0 = [state_scr[sidx(st)] for st in streams]
    s0b = [_bf(x) for x in s0]
    ar = [_bf(jnp.concatenate([P("a_t", st), P("r_t", st)], axis=0)) for st in streams]
    bk = [jnp.concatenate([bd(P("b_t", st)), bd(P("k_t", st))], axis=0) for st in streams]
    pm = [_dot_nt(ar[i], bk[i]) for i in range(len(streams))]
    a_ab = [jnp.where(strict_p, p[0:L, 0:GW], 0.0) for p in pm]
    a_ak = [jnp.where(strict_p, p[0:L, GW:2 * GW], 0.0) for p in pm]
    a_rb = [jnp.where(incl_p, p[L:2 * L, 0:GW], 0.0) for p in pm]
    a_rk = [jnp.where(incl_p, p[L:2 * L, GW:2 * GW], 0.0) for p in pm]
    vbd = [bd(P("v", st)) for st in streams]
    ars = [_dot_nt(ar[i], s0b[i]) for i in range(len(streams))]
    akv = [_dot(_bf(jnp.concatenate([a_ak[i], a_rk[i]], axis=0)), vbd[i])
           for i in range(len(streams))]

    pw = [_dot(_bf(x), bd(x)) for x in a_ab]
    tinv = [eye_p + x for x in a_ab]
    n = 2
    while n < L:
        last = 2 * n >= L
        pwb = [bd(x) for x in pw]
        if last:
            tinv = [tinv[i] + _dot(_bf(tinv[i]), pwb[i]) for i in range(len(streams))]
        else:
            both = [_dot(_bf(jnp.concatenate([pw[i], tinv[i]], axis=0)), pwb[i])
                    for i in range(len(streams))]
            pw = [x[0:L] for x in both]
            tinv = [tinv[i] + both[i][L:2 * L] for i in range(len(streams))]
        n *= 2

    u = [_dot(_bf(tinv[i]), bd(ars[i][0:L] + akv[i][0:L])) for i in range(len(streams))]
    ys = [ars[i][L:2 * L] + akv[i][L:2 * L] + _dot(_bf(a_rb[i]), bd(u[i]))
          for i in range(len(streams))]
    for i, st in enumerate(streams):
        w_l = P("w_l", st)
        uv = _bf(jnp.concatenate([u[i], P("v", st)], axis=0))
        bkh = _bf(jnp.concatenate([P("b_t", st) * w_l, P("k_t", st) * w_l], axis=0))
        state_scr[sidx(st)] = s0[i] * w_l + jnp.where(st_mask, _dot_tn(uv, bkh), 0.0)

    inv_n = 1.0 / HEAD
    y = [jnp.concatenate([ys[bb * n_groups + g] for g in range(n_groups)], axis=1) for bb in range(NB)]
    mean = [m * inv_n for m in _head_sums(y, g128)]
    yc = [y[bb] - mean[bb] for bb in range(NB)]
    var = [s * inv_n for s in _head_sums([x * x for x in yc], g128)]
    for bb in range(NB):
        yn = yc[bb] * lax.rsqrt(var[bb] + LNX_EPS) * prm(_P_LW) + prm(_P_LB)
        gate = pre[bb]["gate"]
        o_ref[bb] = _bf((yn + pre[bb]["bonus"]) * (gate * _sigmoid(gate)))


def _rwkv(u3, pvec, mu_wa, lora, g128, *, col_r, col_wa, d_a):
    B, S, _ = u3.shape
    L = CHUNK
    nb = RWKV_NB
    assert L == HEAD and d_a % GW == 0 and B % nb == 0
    cb = col_r // d_a
    blk = lambda off: pl.BlockSpec((nb, L, d_a), lambda b, c, off=off: (b, c, cb + off))
    full = lambda arr: pl.BlockSpec(arr.shape, lambda b, c: (0,) * arr.ndim)
    return pl.pallas_call(
        _rwkv_kernel,
        grid=(B // nb, S // L),
        in_specs=[
            blk(0), blk(1), blk(2), blk(3),
            pl.BlockSpec((nb, L, LANES), lambda b, c: (b, c, col_wa // LANES)),
            full(pvec), full(mu_wa), full(lora), full(g128),
        ],
        out_specs=pl.BlockSpec((nb, L, d_a), lambda b, c: (b, c, 0)),
        out_shape=jax.ShapeDtypeStruct((B, S, d_a), BF16),
        scratch_shapes=[
            pltpu.VMEM((nb * (d_a // GW), GW, GW), F32),
            pltpu.VMEM((nb, 8, d_a), F32),
        ],
        compiler_params=pltpu.CompilerParams(
            dimension_semantics=("parallel", "arbitrary"), vmem_limit_bytes=VMEM_LIMIT),
        name="rwkv7",
    )(u3, u3, u3, u3, u3, pvec, mu_wa, lora, g128)


def _fox_kernel(q_ref, k_ref, v_ref, g_ref, cq_ref, ct_ref, qg_ref, kg_ref, gm_ref,
                o_ref, kn_scr, *, tq, tk, pair_axis):
    p = pl.program_id(pair_axis)
    qi = pl.program_id(2)
    S = k_ref.shape[1]
    gm = gm_ref[...]
    scale = HEAD ** -0.5

    @pl.when(qi == 0)
    def _():
        for i in range(S // tk):
            kb = k_ref[0, i * tk:(i + 1) * tk, :].astype(F32)
            ms = _gsum(kb * kb, gm) * (1.0 / HEAD)
            kn_scr[i * tk:(i + 1) * tk, :] = _bf(kb * lax.rsqrt(ms + RMS_EPS) * kg_ref[...])

    q = q_ref[0].astype(F32)
    ms = _gsum(q * q, gm) * (1.0 / HEAD)
    qn = q * lax.rsqrt(ms + RMS_EPS) * (qg_ref[...] * scale)
    lane = lax.broadcasted_iota(jnp.int32, (tq, LANES), 1)
    head_of_lane = lane // HEAD
    cq_all = cq_ref[0]
    nh = q.shape[1] // HEAD
    tiles = [slice((hh // 2) * LANES, (hh // 2 + 1) * LANES) for hh in range(nh)]

    qh = [_bf(jnp.where(head_of_lane == hh % 2, qn[:, tiles[hh]], 0.0)) for hh in range(nh)]
    cq = [jnp.sum(jnp.where(lane == nh * p + hh, cq_all, 0.0), axis=-1, keepdims=True)
          for hh in range(nh)]

    def step(j, carry, masked):
        ks = pl.multiple_of(j * tk, tk)
        s = [_dot_nt(qh[hh], kn_scr[pl.ds(ks, tk), tiles[hh]]) for hh in range(nh)]
        s = [s[hh] + cq[hh] - ct_ref[0, pl.ds(nh * p + hh, 1), pl.ds(ks, tk)] for hh in range(nh)]
        if masked:
            rowi = qi * tq + lax.broadcasted_iota(jnp.int32, (tq, tk), 0)
            coli = j * tk + lax.broadcasted_iota(jnp.int32, (tq, tk), 1)
            s = [jnp.where(rowi >= coli, x, -1e30) for x in s]
        m_new = [jnp.maximum(carry[hh][0], jnp.max(s[hh], axis=-1, keepdims=True)) for hh in range(nh)]
        alpha = [jnp.exp(carry[hh][0] - m_new[hh]) for hh in range(nh)]
        pexp = [jnp.exp(s[hh] - m_new[hh]) for hh in range(nh)]
        l_new = [alpha[hh] * carry[hh][1] + jnp.sum(pexp[hh], axis=-1, keepdims=True)
                 for hh in range(nh)]
        pv = [_dot(_bf(pexp[hh]), v_ref[0, pl.ds(ks, tk), tiles[hh]]) for hh in range(nh)]
        return tuple((m_new[hh], l_new[hh], alpha[hh] * carry[hh][2] + pv[hh]) for hh in range(nh))

    init = tuple((jnp.full((tq, 1), -1e30, F32), jnp.zeros((tq, 1), F32),
                  jnp.zeros((tq, LANES), F32)) for _ in range(nh))
    nfull = (qi * tq) // tk
    carry = lax.fori_loop(0, nfull, lambda j, c: step(j, c, False), init)
    for d in range(tq // tk):
        carry = step(nfull + d, carry, True)

    outs = [acc / l for (_, l, acc) in carry]
    o = jnp.concatenate([jnp.where(head_of_lane == 0, outs[2 * t], outs[2 * t + 1])
                         for t in range(nh // 2)], axis=1)
    g = g_ref[0].astype(F32)
    o_ref[0] = _bf(o * (g * _sigmoid(g)))


def _fox(u3, cum, cumt, qg2, kg2, gm, *, col_q, d_b, tq, tk):
    B, S, _ = u3.shape
    fw = gm.shape[0]
    cq, ck, cv, cg = ((col_q + i * d_b) // fw for i in range(4))
    return pl.pallas_call(
        functools.partial(_fox_kernel, tq=tq, tk=tk, pair_axis=1),
        grid=(B, d_b // fw, S // tq),
        in_specs=[
            pl.BlockSpec((1, tq, fw), lambda b, p, i: (b, i, cq + p)),
            pl.BlockSpec((1, S, fw), lambda b, p, i: (b, 0, ck + p)),
            pl.BlockSpec((1, S, fw), lambda b, p, i: (b, 0, cv + p)),
            pl.BlockSpec((1, tq, fw), lambda b, p, i: (b, i, cg + p)),
            pl.BlockSpec((1, tq, LANES), lambda b, p, i: (b, i, 0)),
            pl.BlockSpec((1, 8, S), lambda b, p, i: (b, 0, 0)),
            pl.BlockSpec((1, fw), lambda b, p, i: (0, 0)),
            pl.BlockSpec((1, fw), lambda b, p, i: (0, 0)),
            pl.BlockSpec((fw, fw), lambda b, p, i: (0, 0)),
        ],
        out_specs=pl.BlockSpec((1, tq, fw), lambda b, p, i: (b, i, p)),
        out_shape=jax.ShapeDtypeStruct((B, S, d_b), BF16),
        scratch_shapes=[pltpu.VMEM((S, fw), BF16)],
        compiler_params=pltpu.CompilerParams(
            dimension_semantics=("parallel", "parallel", "arbitrary"), vmem_limit_bytes=VMEM_LIMIT),
        name="fox",
    )(u3, u3, u3, u3, cum, cumt, qg2, kg2, gm)


def _out_kernel(x_ref, ya_ref, yb_ref, ga_ref, gb_ref, woa_ref, wob_ref, wo_ref, fg_ref, o_ref):
    za = _dot(ya_ref[...], woa_ref[...])
    zb = _dot(yb_ref[...], wob_ref[...])
    merged = (_sigmoid(ga_ref[...].astype(F32)) * za + _sigmoid(gb_ref[...].astype(F32)) * zb)
    o = x_ref[...] + _dot(_bf(merged), wo_ref[...])
    ms = jnp.mean(o * o, axis=-1, keepdims=True)
    o_ref[...] = o * lax.rsqrt(ms + RMS_EPS) * fg_ref[...]


def _out(x2, ya2, yb2, u2, woa, wob, wo, fg, *, tm):
    T, D = x2.shape
    full = lambda arr: pl.BlockSpec(arr.shape, lambda i: (0,) * arr.ndim)
    return pl.pallas_call(
        _out_kernel,
        grid=(T // tm,),
        in_specs=[
            pl.BlockSpec((tm, D), lambda i: (i, 0)),
            pl.BlockSpec((tm, ya2.shape[1]), lambda i: (i, 0)),
            pl.BlockSpec((tm, yb2.shape[1]), lambda i: (i, 0)),
            pl.BlockSpec((tm, D), lambda i: (i, 0)),
            pl.BlockSpec((tm, D), lambda i: (i, 1)),
            full(woa), full(wob), full(wo), full(fg),
        ],
        out_specs=pl.BlockSpec((tm, D), lambda i: (i, 0)),
        out_shape=jax.ShapeDtypeStruct((T, D), F32),
        compiler_params=pltpu.CompilerParams(
            dimension_semantics=("parallel",), vmem_limit_bytes=VMEM_LIMIT),
        name="outstage",
    )(x2, ya2, yb2, u2, u2, woa, wob, wo, fg)


def _block_ones(width):
    i = jnp.arange(width) // HEAD
    return (i[:, None] == i[None, :]).astype(BF16)


def _layer(x2, B, S, norm_g, w_in, shift_mu, w_lora_up, w0, a_lora_up, a0, k_k, k_a, r_k,
           lnx_w, lnx_b, f_bias, q_norm_g, k_norm_g, w_out_a, w_out_b, w_out, out_gain):
    T, D = x2.shape
    d_a = w0.shape[0]
    d_b = w_out_b.shape[0]
    rank = w_lora_up.shape[0]
    h_b = f_bias.shape[0]
    rw = 4 * d_a + 2 * rank
    fx = 4 * d_b + h_b

    ca = w_in[:, :rw]
    cb = w_in[:, rw:rw + fx]
    cg = w_in[:, rw + fx:]
    wdad = ca[:, 3 * d_a:3 * d_a + 2 * rank]
    flog = jnp.pad(cb[:, 4 * d_b:], ((0, 0), (0, LANES - h_b)))
    w_perm = _bf(jnp.concatenate(
        [cg, ca[:, :3 * d_a], ca[:, 3 * d_a + 2 * rank:], cb[:, :4 * d_b], wdad, flog], axis=1))
    col_r = 2 * D
    col_q = col_r + 4 * d_a
    col_wa = col_q + 4 * d_b

    u2, uf = _inproj(x2, norm_g.reshape(1, D), w_perm, tm=1024, tn=1280)
    u3 = u2.reshape(B, S, u2.shape[1])

    mu = shift_mu
    rows = [mu[:d_a], mu[d_a:2 * d_a], mu[2 * d_a:3 * d_a], mu[3 * d_a + 2 * rank:],
            w0, a0, k_k, k_a, r_k.reshape(-1), lnx_w, lnx_b]
    pvec = jnp.stack(rows + [jnp.zeros_like(w0)] * (16 - len(rows)), axis=0)
    mu_wa = mu[3 * d_a:3 * d_a + 2 * rank].reshape(1, 2 * rank)
    z = jnp.zeros((rank, d_a), F32)
    lora = _bf(jnp.concatenate(
        [jnp.concatenate([w_lora_up, z], axis=1), jnp.concatenate([z, a_lora_up], axis=1)], axis=0))
    ya = _rwkv(u3, pvec, mu_wa, lora, _block_ones(LANES), col_r=col_r, col_wa=col_wa, d_a=d_a)

    fb_pad = jnp.pad(f_bias, (0, LANES - h_b)).reshape(1, LANES)
    cum, cumt = _fprep(uf.reshape(B, S, LANES), fb_pad)
    qg2 = jnp.tile(q_norm_g, FOX_HEADS).reshape(1, FOX_HEADS * HEAD)
    kg2 = jnp.tile(k_norm_g, FOX_HEADS).reshape(1, FOX_HEADS * HEAD)
    yb = _fox(u3, cum, cumt, qg2, kg2, _block_ones(FOX_HEADS * HEAD), col_q=col_q, d_b=d_b,
              tq=256, tk=256)

    return _out(x2, ya.reshape(T, d_a), yb.reshape(T, d_b), u2,
                _bf(w_out_a), _bf(w_out_b), _bf(w_out), out_gain.reshape(1, D), tm=512)


def kernel(x, norm_g, w_in, shift_mu, w_lora_up, w0, a_lora_up, a0, k_k, k_a, r_k, lnx_w, lnx_b,
           f_bias, q_norm_g, k_norm_g, w_out_a, w_out_b, w_out, final_norm_g):
    B, S, D = x.shape
    depth = w_in.shape[0]
    assert depth == 1, "the fused output stage applies the final norm after the single layer"
    x2 = x.reshape(B * S, D)
    out = _layer(x2, B, S, norm_g[0], w_in[0], shift_mu[0], w_lora_up[0], w0[0], a_lora_up[0],
                 a0[0], k_k[0], k_a[0], r_k[0], lnx_w[0], lnx_b[0], f_bias[0], q_norm_g[0],
                 k_norm_g[0], w_out_a[0], w_out_b[0], w_out[0], final_norm_g)
    return out.reshape(B, S, D)
```

```python
import functools

import jax
import jax.numpy as jnp
from jax import lax
from jax.experimental import pallas as pl
from jax.experimental.pallas import tpu as pltpu

F32 = jnp.float32
BF16 = jnp.bfloat16

HEAD = 64
LANES = 128
RMS_EPS = 1e-6
LNX_EPS = 64e-5
CHUNK = 64
GROUP = 4
GW = GROUP * HEAD
RWKV_NB = 4
FOX_HEADS = 4
AUG0 = HEAD
LOG2E = 1.4426950408889634
VMEM_LIMIT = 56 * 1024 * 1024


def _bf(x):
    return x.astype(BF16)


def _dot(a, b):
    return jnp.dot(a, b, preferred_element_type=F32)


def _dot_nt(a, b):
    return lax.dot_general(a, b, (((1,), (1,)), ((), ())), preferred_element_type=F32)


def _dot_tn(a, b):
    return lax.dot_general(a, b, (((0,), (0,)), ((), ())), preferred_element_type=F32)


def _split2(x):
    hi = _bf(x)
    lo = _bf(x - hi.astype(F32))
    return hi, lo


def _split3(x):
    hi = _bf(x)
    r1 = x - hi.astype(F32)
    mid = _bf(r1)
    lo = _bf(r1 - mid.astype(F32))
    return hi, mid, lo


def _gsum(x, g):
    hi, lo = _split2(x)
    return _dot(hi, g) + _dot(lo, g)


def _head_sums(xs, g128):
    m, w = xs[0].shape
    nt = w // LANES
    parts = []
    for x in xs:
        hi, lo = _split2(x)
        parts += [hi[:, t * LANES:(t + 1) * LANES] for t in range(nt)]
        parts += [lo[:, t * LANES:(t + 1) * LANES] for t in range(nt)]
    r = _dot(jnp.concatenate(parts, axis=0), g128)
    outs = []
    for i in range(len(xs)):
        base = 2 * nt * i
        outs.append(jnp.concatenate(
            [r[(base + t) * m:(base + t + 1) * m] + r[(base + nt + t) * m:(base + nt + t + 1) * m]
             for t in range(nt)], axis=1))
    return outs


def _sigmoid(x):
    return 1.0 / (1.0 + jnp.exp(-x))


def _inproj_kernel(x_ref, g_ref, w_ref, o_ref, of_ref, h_scr, *, n_col_tiles):
    j = pl.program_id(1)

    @pl.when(j == 0)
    def _():
        x = x_ref[...]
        ms = jnp.mean(x * x, axis=-1, keepdims=True)
        h_scr[...] = _bf(x * lax.rsqrt(ms + RMS_EPS) * g_ref[...])

    acc = _dot(h_scr[...], w_ref[...])
    o_ref[...] = _bf(acc)

    @pl.when(j == n_col_tiles - 1)
    def _():
        of_ref[...] = acc[:, acc.shape[1] - LANES:]


def _inproj(x2, norm_g, w_perm, *, tm, tn):
    T, D = x2.shape
    N = w_perm.shape[1]
    nj = N // tn
    return pl.pallas_call(
        functools.partial(_inproj_kernel, n_col_tiles=nj),
        grid=(T // tm, nj),
        in_specs=[
            pl.BlockSpec((tm, D), lambda i, j: (i, 0)),
            pl.BlockSpec((1, D), lambda i, j: (0, 0)),
            pl.BlockSpec((D, tn), lambda i, j: (0, j)),
        ],
        out_specs=[
            pl.BlockSpec((tm, tn), lambda i, j: (i, j)),
            pl.BlockSpec((tm, LANES), lambda i, j: (i, 0)),
        ],
        out_shape=[
            jax.ShapeDtypeStruct((T, N), BF16),
            jax.ShapeDtypeStruct((T, LANES), F32),
        ],
        scratch_shapes=[pltpu.VMEM((tm, D), BF16)],
        compiler_params=pltpu.CompilerParams(
            dimension_semantics=("parallel", "arbitrary"), vmem_limit_bytes=VMEM_LIMIT),
        name="inproj",
    )(x2, norm_g, w_perm)


def _fprep_kernel(f_ref, fb_ref, sel_ref, aug_ref, *, blk):
    S = f_ref.shape[1]
    row = lax.broadcasted_iota(jnp.int32, (blk, blk), 0)
    col = lax.broadcasted_iota(jnp.int32, (blk, blk), 1)
    tril = _bf(jnp.where(row >= col, 1.0, 0.0))
    carry = jnp.zeros((1, LANES), F32)
    for i in range(S // blk):
        z = f_ref[0, i * blk:(i + 1) * blk, :] + fb_ref[...]
        lf = jnp.minimum(z, 0.0) - jnp.log1p(jnp.exp(-jnp.abs(z)))
        hi, mid, lo = _split3(lf)
        c = (_dot(tril, hi) + _dot(tril, mid) + _dot(tril, lo)) + carry
        pieces = jnp.concatenate(_split3(c * LOG2E), axis=1)
        aug_ref[0, i * blk:(i + 1) * blk, :] = _bf(_dot(pieces, sel_ref[...]))
        carry = c[blk - 1:blk, :]


def _fprep(uf3, fb_pad, sel):
    B, S, _ = uf3.shape
    return pl.pallas_call(
        functools.partial(_fprep_kernel, blk=256),
        grid=(B,),
        in_specs=[
            pl.BlockSpec((1, S, LANES), lambda b: (b, 0, 0)),
            pl.BlockSpec((1, LANES), lambda b: (0, 0)),
            pl.BlockSpec(sel.shape, lambda b: (0, 0)),
        ],
        out_specs=pl.BlockSpec((1, S, LANES), lambda b: (b, 0, 0)),
        out_shape=jax.ShapeDtypeStruct((B, S, LANES), BF16),
        compiler_params=pltpu.CompilerParams(dimension_semantics=("parallel",)),
        name="fprep",
    )(uf3, fb_pad, sel)


def _aug_selector(n_heads):
    assert AUG0 + 6 * n_heads <= LANES
    r = jnp.arange(LANES)[:, None]
    c = jnp.arange(LANES)[None, :]
    blocks = []
    for i in range(3):
        m = jnp.where(c == AUG0 + 6 * r + i, -1.0, jnp.where(c == AUG0 + 6 * r + 3 + i, 1.0, 0.0))
        blocks.append(jnp.where(r < n_heads, m, 0.0))
    return _bf(jnp.concatenate(blocks, axis=0))


_P_MU_R, _P_MU_K, _P_MU_V, _P_MU_G, _P_W0, _P_A0, _P_KK, _P_KA, _P_RK, _P_LW, _P_LB = range(11)


def _shift_mix(x, carry_row, mu):
    rolled = pltpu.roll(x, shift=1, axis=0)
    row = lax.broadcasted_iota(jnp.int32, x.shape, 0)
    prev = jnp.where(row == 0, carry_row, rolled)
    return x + (prev - x) * mu


def _block_diag(x, half_masks):
    xb = _bf(x)
    n_tiles = xb.shape[1] // LANES
    zero = jnp.zeros((xb.shape[0], LANES), BF16)
    rows = []
    for h in range(xb.shape[1] // HEAD):
        t = h // 2
        piece = xb[:, t * LANES:(t + 1) * LANES] * half_masks[h % 2]
        rows.append(jnp.concatenate([piece if i == t else zero for i in range(n_tiles)], axis=1))
    return jnp.concatenate(rows, axis=0)


def _rwkv_kernel(r_ref, k_ref, v_ref, g_ref, wa_ref, pv_ref, muwa_ref, lora_ref, gm_ref,
                 o_ref, state_scr, carry_scr):
    c = pl.program_id(1)
    NB = r_ref.shape[0]
    L = r_ref.shape[1]
    DA = r_ref.shape[2]
    n_groups = DA // GW

    @pl.when(c == 0)
    def _():
        state_scr[...] = jnp.zeros_like(state_scr)
        carry_scr[...] = jnp.zeros_like(carry_scr)

    def prm(i):
        return pv_ref[i:i + 1, :]

    g128 = gm_ref[...]
    row = lax.broadcasted_iota(jnp.int32, (L, L), 0)
    col = lax.broadcasted_iota(jnp.int32, (L, L), 1)
    tril = _bf(jnp.where(row >= col, 1.0, 0.0))
    lane128 = lax.broadcasted_iota(jnp.int32, (1, LANES), 1)
    half_masks = [_bf(jnp.where(lane128 // HEAD == i, 1.0, 0.0)) for i in range(2)]
    prow = lax.broadcasted_iota(jnp.int32, (L, GW), 0)
    pcol = lax.broadcasted_iota(jnp.int32, (L, GW), 1) & (HEAD - 1)
    incl_p = prow >= pcol
    strict_p = prow > pcol
    eye_p = jnp.where(prow == pcol, 1.0, 0.0)
    st_mask = (lax.broadcasted_iota(jnp.int32, (GW, GW), 0) // HEAD
               == lax.broadcasted_iota(jnp.int32, (GW, GW), 1) // HEAD)
    bd = lambda x: _block_diag(x, half_masks)

    pre = []
    for bb in range(NB):
        raw = [ref[bb].astype(F32) for ref in (r_ref, k_ref, v_ref, g_ref, wa_ref)]
        mus = [prm(_P_MU_R), prm(_P_MU_K), prm(_P_MU_V), prm(_P_MU_G), muwa_ref[...]]
        mixed = []
        for i, x in enumerate(raw):
            wdt = x.shape[1]
            mixed.append(_shift_mix(x, carry_scr[bb, i:i + 1, 0:wdt], mus[i]))
            carry_scr[bb, i:i + 1, 0:wdt] = x[L - 1:L, :]
        r, k, v, gate, wa = mixed

        wa_act = jnp.where(lane128 < HEAD, jnp.tanh(wa), wa)
        lo = _dot(_bf(wa_act), lora_ref[...])
        zw = -(prm(_P_W0) + lo[:, 0:DA])
        w = -(jnp.maximum(zw, 0.0) + jnp.log(1.0 + jnp.exp(-jnp.abs(zw)))) - 0.5
        ld = -jnp.exp(w)
        a = _sigmoid(prm(_P_A0) + lo[:, DA:2 * DA])

        kk = k * prm(_P_KK)
        k2 = k * (1.0 + (a - 1.0) * prm(_P_KA))
        ssq, bon = _head_sums([kk * kk, r * k2 * prm(_P_RK)], g128)
        kk = kk / jnp.maximum(jnp.sqrt(ssq), 1e-12)

        h3, m3, l3 = _split3(ld)
        cum = _dot(tril, h3) + _dot(tril, m3) + _dot(tril, l3)
        e_neg = jnp.exp(-cum)
        w_l = jnp.exp(cum[L - 1:L, :])
        a_t = -kk * jnp.exp(cum - ld)
        r_t = r * jnp.exp(cum)
        b_t = (kk * a) * e_neg
        k_t = k2 * e_neg
        pre.append(dict(v=v, gate=gate, bonus=bon * v, w_l=w_l, a_t=a_t, r_t=r_t, b_t=b_t, k_t=k_t))

    streams = [(bb, g) for bb in range(NB) for g in range(n_groups)]
    gsl = lambda g: slice(g * GW, (g + 1) * GW)
    P = lambda name, st: pre[st[0]][name][:, gsl(st[1])]
    sidx = lambda st: st[0] * n_groups + st[1]

    s0 = [state_scr[sidx(st)] for st in streams]
    s0b = [_bf(x) for x in s0]
    ar = [_bf(jnp.concatenate([P("a_t", st), P("r_t", st)], axis=0)) for st in streams]
    bk = [jnp.concatenate([bd(P("b_t", st)), bd(P("k_t", st))], axis=0) for st in streams]
    pm = [_dot_nt(ar[i], bk[i]) for i in range(len(streams))]
    a_ab = [jnp.where(strict_p, p[0:L, 0:GW], 0.0) for p in pm]
    a_ak = [jnp.where(strict_p, p[0:L, GW:2 * GW], 0.0) for p in pm]
    a_rb = [jnp.where(incl_p, p[L:2 * L, 0:GW], 0.0) for p in pm]
    a_rk = [jnp.where(incl_p, p[L:2 * L, GW:2 * GW], 0.0) for p in pm]
    vbd = [bd(P("v", st)) for st in streams]
    ars = [_dot_nt(ar[i], s0b[i]) for i in range(len(streams))]
    akv = [_dot(_bf(jnp.concatenate([a_ak[i], a_rk[i]], axis=0)), vbd[i])
           for i in range(len(streams))]

    pw = [_dot(_bf(x), bd(x)) for x in a_ab]
    tinv = [eye_p + x for x in a_ab]
    n = 2
    while n < L:
        last = 2 * n >= L
        pwb = [bd(x) for x in pw]
        if last:
            tinv = [tinv[i] + _dot(_bf(tinv[i]), pwb[i]) for i in range(len(streams))]
        else:
            both = [_dot(_bf(jnp.concatenate([pw[i], tinv[i]], axis=0)), pwb[i])
                    for i in range(len(streams))]
            pw = [x[0:L] for x in both]
            tinv = [tinv[i] + both[i][L:2 * L] for i in range(len(streams))]
        n *= 2

    u = [_dot(_bf(tinv[i]), bd(ars[i][0:L] + akv[i][0:L])) for i in range(len(streams))]
    ys = [ars[i][L:2 * L] + akv[i][L:2 * L] + _dot(_bf(a_rb[i]), bd(u[i]))
          for i in range(len(streams))]
    for i, st in enumerate(streams):
        w_l = P("w_l", st)
        uv = _bf(jnp.concatenate([u[i], P("v", st)], axis=0))
        bkh = _bf(jnp.concatenate([P("b_t", st) * w_l, P("k_t", st) * w_l], axis=0))
        state_scr[sidx(st)] = s0[i] * w_l + jnp.where(st_mask, _dot_tn(uv, bkh), 0.0)

    inv_n = 1.0 / HEAD
    y = [jnp.concatenate([ys[bb * n_groups + g] for g in range(n_groups)], axis=1) for bb in range(NB)]
    mean = [m * inv_n for m in _head_sums(y, g128)]
    yc = [y[bb] - mean[bb] for bb in range(NB)]
    var = [s * inv_n for s in _head_sums([x * x for x in yc], g128)]
    for bb in range(NB):
        yn = yc[bb] * lax.rsqrt(var[bb] + LNX_EPS) * prm(_P_LW) + prm(_P_LB)
        gate = pre[bb]["gate"]
        o_ref[bb] = _bf((yn + pre[bb]["bonus"]) * (gate * _sigmoid(gate)))


def _rwkv(u3, pvec, mu_wa, lora, g128, *, col_r, col_wa, d_a):
    B, S, _ = u3.shape
    L = CHUNK
    nb = RWKV_NB
    assert L == HEAD and d_a % GW == 0 and B % nb == 0
    cb = col_r // d_a
    blk = lambda off: pl.BlockSpec((nb, L, d_a), lambda b, c, off=off: (b, c, cb + off))
    full = lambda arr: pl.BlockSpec(arr.shape, lambda b, c: (0,) * arr.ndim)
    return pl.pallas_call(
        _rwkv_kernel,
        grid=(B // nb, S // L),
        in_specs=[
            blk(0), blk(1), blk(2), blk(3),
            pl.BlockSpec((nb, L, LANES), lambda b, c: (b, c, col_wa // LANES)),
            full(pvec), full(mu_wa), full(lora), full(g128),
        ],
        out_specs=pl.BlockSpec((nb, L, d_a), lambda b, c: (b, c, 0)),
        out_shape=jax.ShapeDtypeStruct((B, S, d_a), BF16),
        scratch_shapes=[
            pltpu.VMEM((nb * (d_a // GW), GW, GW), F32),
            pltpu.VMEM((nb, 8, d_a), F32),
        ],
        compiler_params=pltpu.CompilerParams(
            dimension_semantics=("parallel", "arbitrary"), vmem_limit_bytes=VMEM_LIMIT),
        name="rwkv7",
    )(u3, u3, u3, u3, u3, pvec, mu_wa, lora, g128)


def _fox_kernel(q_ref, k_ref, v_ref, g_ref, aq_ref, ak_ref, qg_ref, kg_ref, gm_ref,
                o_ref, kp_scr, vt_scr, *, tq, tk):
    p = pl.program_id(1)
    qi = pl.program_id(2)
    S = k_ref.shape[1]
    nh = q_ref.shape[2] // HEAD
    g128 = gm_ref[...]
    lane = lax.broadcasted_iota(jnp.int32, (1, LANES), 1)
    feat = lane < HEAD

    def minus_lanes(hh):
        lo = AUG0 + 6 * (nh * p + hh)
        return (lane >= lo) & (lane < lo + 3)

    def plus_lanes(hh):
        lo = AUG0 + 6 * (nh * p + hh) + 3
        return (lane >= lo) & (lane < lo + 3)

    def head_tile(x, hh):
        t = x[:, (hh // 2) * LANES:(hh // 2 + 1) * LANES]
        return pltpu.roll(t, HEAD, axis=1) if hh % 2 else t

    @pl.when(qi == 0)
    def _():
        for i in range(S // tk):
            rows = slice(i * tk, (i + 1) * tk)
            kb = k_ref[0, rows, :].astype(F32)
            (ssq,) = _head_sums([kb * kb], g128)
            kn = kb * lax.rsqrt(ssq * (1.0 / HEAD) + RMS_EPS) * kg_ref[...]
            aug = ak_ref[0, rows, :].astype(F32)
            for hh in range(nh):
                kp = jnp.where(feat, head_tile(kn, hh), jnp.where(plus_lanes(hh), 1.0, aug))
                kp_scr[hh, rows, :] = _bf(kp)
            vt_scr[:, rows] = _bf(jnp.transpose(v_ref[0, rows, :].astype(F32)))

    q = q_ref[0].astype(F32)
    (ssq,) = _head_sums([q * q], g128)
    qn = q * lax.rsqrt(ssq * (1.0 / HEAD) + RMS_EPS) * (qg_ref[...] * (HEAD ** -0.5 * LOG2E))
    augq = aq_ref[0].astype(F32)
    qp = [_bf(jnp.where(feat, head_tile(qn, hh),
                        jnp.where(minus_lanes(hh), 1.0, jnp.where(plus_lanes(hh), augq, 0.0))))
          for hh in range(nh)]

    def step(j, carry, masked):
        ks = pl.multiple_of(j * tk, tk)
        st = [_dot_nt(kp_scr[hh, pl.ds(ks, tk), :], qp[hh]) for hh in range(nh)]
        if masked:
            keyi = j * tk + lax.broadcasted_iota(jnp.int32, (tk, tq), 0)
            qryi = qi * tq + lax.broadcasted_iota(jnp.int32, (tk, tq), 1)
            st = [jnp.where(qryi >= keyi, x, -1e30) for x in st]
        m_new = [jnp.maximum(carry[hh][0], jnp.max(st[hh], axis=0, keepdims=True)) for hh in range(nh)]
        alpha = [jnp.exp2(carry[hh][0] - m_new[hh]) for hh in range(nh)]
        pt = [jnp.exp2(st[hh] - m_new[hh]) for hh in range(nh)]
        l_new = [alpha[hh] * carry[hh][1] + jnp.sum(pt[hh], axis=0, keepdims=True) for hh in range(nh)]
        pv = [_dot(vt_scr[hh * HEAD:(hh + 1) * HEAD, pl.ds(ks, tk)], _bf(pt[hh]))
              for hh in range(nh)]
        return tuple((m_new[hh], l_new[hh], alpha[hh] * carry[hh][2] + pv[hh]) for hh in range(nh))

    init = tuple((jnp.full((1, tq), -1e30, F32), jnp.zeros((1, tq), F32),
                  jnp.zeros((HEAD, tq), F32)) for _ in range(nh))
    nfull = (qi * tq) // tk
    carry = lax.fori_loop(0, nfull, lambda j, c: step(j, c, False), init)
    for d in range(tq // tk):
        carry = step(nfull + d, carry, True)

    ot = jnp.concatenate([acc * (1.0 / l) for (_, l, acc) in carry], axis=0)
    g = g_ref[0].astype(F32)
    o_ref[0] = _bf(jnp.transpose(ot) * (g * _sigmoid(g)))


def _fox(u3, aug, qg2, kg2, g128, *, col_q, d_b, tq, tk):
    B, S, _ = u3.shape
    fw = qg2.shape[1]
    cq, ck, cv, cg = ((col_q + i * d_b) // fw for i in range(4))
    return pl.pallas_call(
        functools.partial(_fox_kernel, tq=tq, tk=tk),
        grid=(B, d_b // fw, S // tq),
        in_specs=[
            pl.BlockSpec((1, tq, fw), lambda b, p, i: (b, i, cq + p)),
            pl.BlockSpec((1, S, fw), lambda b, p, i: (b, 0, ck + p)),
            pl.BlockSpec((1, S, fw), lambda b, p, i: (b, 0, cv + p)),
            pl.BlockSpec((1, tq, fw), lambda b, p, i: (b, i, cg + p)),
            pl.BlockSpec((1, tq, LANES), lambda b, p, i: (b, i, 0)),
            pl.BlockSpec((1, S, LANES), lambda b, p, i: (b, 0, 0)),
            pl.BlockSpec((1, fw), lambda b, p, i: (0, 0)),
            pl.BlockSpec((1, fw), lambda b, p, i: (0, 0)),
            pl.BlockSpec((LANES, LANES), lambda b, p, i: (0, 0)),
        ],
        out_specs=pl.BlockSpec((1, tq, fw), lambda b, p, i: (b, i, p)),
        out_shape=jax.ShapeDtypeStruct((B, S, d_b), BF16),
        scratch_shapes=[
            pltpu.VMEM((fw // HEAD, S, LANES), BF16),
            pltpu.VMEM((fw, S), BF16),
        ],
        compiler_params=pltpu.CompilerParams(
            dimension_semantics=("parallel", "parallel", "arbitrary"), vmem_limit_bytes=VMEM_LIMIT),
        name="fox",
    )(u3, u3, u3, u3, aug, aug, qg2, kg2, g128)


def _out_kernel(x_ref, ya_ref, yb_ref, ga_ref, gb_ref, woa_ref, wob_ref, wo_ref, fg_ref, o_ref):
    za = _dot(ya_ref[...], woa_ref[...])
    zb = _dot(yb_ref[...], wob_ref[...])
    merged = (_sigmoid(ga_ref[...].astype(F32)) * za + _sigmoid(gb_ref[...].astype(F32)) * zb)
    o = x_ref[...] + _dot(_bf(merged), wo_ref[...])
    ms = jnp.mean(o * o, axis=-1, keepdims=True)
    o_ref[...] = o * lax.rsqrt(ms + RMS_EPS) * fg_ref[...]


def _out(x2, ya2, yb2, u2, woa, wob, wo, fg, *, tm):
    T, D = x2.shape
    full = lambda arr: pl.BlockSpec(arr.shape, lambda i: (0,) * arr.ndim)
    return pl.pallas_call(
        _out_kernel,
        grid=(T // tm,),
        in_specs=[
            pl.BlockSpec((tm, D), lambda i: (i, 0)),
            pl.BlockSpec((tm, ya2.shape[1]), lambda i: (i, 0)),
            pl.BlockSpec((tm, yb2.shape[1]), lambda i: (i, 0)),
            pl.BlockSpec((tm, D), lambda i: (i, 0)),
            pl.BlockSpec((tm, D), lambda i: (i, 1)),
            full(woa), full(wob), full(wo), full(fg),
        ],
        out_specs=pl.BlockSpec((tm, D), lambda i: (i, 0)),
        out_shape=jax.ShapeDtypeStruct((T, D), F32),
        compiler_params=pltpu.CompilerParams(
            dimension_semantics=("parallel",), vmem_limit_bytes=VMEM_LIMIT),
        name="outstage",
    )(x2, ya2, yb2, u2, u2, woa, wob, wo, fg)


def _block_ones(width):
    i = jnp.arange(width) // HEAD
    return (i[:, None] == i[None, :]).astype(BF16)


def _layer(x2, B, S, norm_g, w_in, shift_mu, w_lora_up, w0, a_lora_up, a0, k_k, k_a, r_k,
           lnx_w, lnx_b, f_bias, q_norm_g, k_norm_g, w_out_a, w_out_b, w_out, out_gain):
    T, D = x2.shape
    d_a = w0.shape[0]
    d_b = w_out_b.shape[0]
    rank = w_lora_up.shape[0]
    h_b = f_bias.shape[0]
    rw = 4 * d_a + 2 * rank
    fx = 4 * d_b + h_b

    ca = w_in[:, :rw]
    cb = w_in[:, rw:rw + fx]
    cg = w_in[:, rw + fx:]
    wdad = ca[:, 3 * d_a:3 * d_a + 2 * rank]
    flog = jnp.pad(cb[:, 4 * d_b:], ((0, 0), (0, LANES - h_b)))
    w_perm = _bf(jnp.concatenate(
        [cg, ca[:, :3 * d_a], ca[:, 3 * d_a + 2 * rank:], cb[:, :4 * d_b], wdad, flog], axis=1))
    col_r = 2 * D
    col_q = col_r + 4 * d_a
    col_wa = col_q + 4 * d_b

    u2, uf = _inproj(x2, norm_g.reshape(1, D), w_perm, tm=1024, tn=1280)
    u3 = u2.reshape(B, S, u2.shape[1])

    mu = shift_mu
    rows = [mu[:d_a], mu[d_a:2 * d_a], mu[2 * d_a:3 * d_a], mu[3 * d_a + 2 * rank:],
            w0, a0, k_k, k_a, r_k.reshape(-1), lnx_w, lnx_b]
    pvec = jnp.stack(rows + [jnp.zeros_like(w0)] * (16 - len(rows)), axis=0)
    mu_wa = mu[3 * d_a:3 * d_a + 2 * rank].reshape(1, 2 * rank)
    z = jnp.zeros((rank, d_a), F32)
    lora = _bf(jnp.concatenate(
        [jnp.concatenate([w_lora_up, z], axis=1), jnp.concatenate([z, a_lora_up], axis=1)], axis=0))
    ya = _rwkv(u3, pvec, mu_wa, lora, _block_ones(LANES), col_r=col_r, col_wa=col_wa, d_a=d_a)

    fb_pad = jnp.pad(f_bias, (0, LANES - h_b)).reshape(1, LANES)
    aug = _fprep(uf.reshape(B, S, LANES), fb_pad, _aug_selector(h_b))
    qg2 = jnp.tile(q_norm_g, FOX_HEADS).reshape(1, FOX_HEADS * HEAD)
    kg2 = jnp.tile(k_norm_g, FOX_HEADS).reshape(1, FOX_HEADS * HEAD)
    yb = _fox(u3, aug, qg2, kg2, _block_ones(LANES), col_q=col_q, d_b=d_b, tq=512, tk=512)

    return _out(x2, ya.reshape(T, d_a), yb.reshape(T, d_b), u2,
                _bf(w_out_a), _bf(w_out_b), _bf(w_out), out_gain.reshape(1, D), tm=512)


def kernel(x, norm_g, w_in, shift_mu, w_lora_up, w0, a_lora_up, a0, k_k, k_a, r_k, lnx_w, lnx_b,
           f_bias, q_norm_g, k_norm_g, w_out_a, w_out_b, w_out, final_norm_g):
    B, S, D = x.shape
    depth = w_in.shape[0]
    assert depth == 1, "the fused output stage applies the final norm after the single layer"
    x2 = x.reshape(B * S, D)
    out = _layer(x2, B, S, norm_g[0], w_in[0], shift_mu[0], w_lora_up[0], w0[0], a_lora_up[0],
                 a0[0], k_k[0], k_a[0], r_k[0], lnx_w[0], lnx_b[0], f_bias[0], q_norm_g[0],
                 k_norm_g[0], w_out_a[0], w_out_b[0], w_out[0], final_norm_g)
    return out.reshape(B, S, D)
```

```python
import functools

import jax
import jax.numpy as jnp
from jax import lax
from jax.experimental import pallas as pl
from jax.experimental.pallas import tpu as pltpu

F32 = jnp.float32
BF16 = jnp.bfloat16

HEAD = 64
LANES = 128
RMS_EPS = 1e-6
LNX_EPS = 64e-5
CHUNK = 64
GROUP = 4
GW = GROUP * HEAD
RWKV_NB = 8
FOX_HEADS = 4
AUG0 = HEAD
LOG2E = 1.4426950408889634
DECAY_SCALE = -0.6065306597126334
VMEM_LIMIT = 56 * 1024 * 1024


def _bf(x):
    return x.astype(BF16)


def _dot(a, b):
    return jnp.dot(a, b, preferred_element_type=F32)


def _dot_nt(a, b):
    return lax.dot_general(a, b, (((1,), (1,)), ((), ())), preferred_element_type=F32)


def _dot_tn(a, b):
    return lax.dot_general(a, b, (((0,), (0,)), ((), ())), preferred_element_type=F32)


def _split2(x):
    hi = _bf(x)
    lo = _bf(x - hi.astype(F32))
    return hi, lo


def _split3(x):
    hi = _bf(x)
    r1 = x - hi.astype(F32)
    mid = _bf(r1)
    lo = _bf(r1 - mid.astype(F32))
    return hi, mid, lo


def _head_sums(xs, g128, two_pass):
    m, w = xs[0].shape
    nt = w // LANES
    parts = []
    for x, tp in zip(xs, two_pass):
        for piece in (_split2(x) if tp else (_bf(x),)):
            parts += [piece[:, t * LANES:(t + 1) * LANES] for t in range(nt)]
    r = _dot(jnp.concatenate(parts, axis=0), g128)
    tile = lambda i: r[i * m:(i + 1) * m]
    outs, base = [], 0
    for tp in two_pass:
        if tp:
            cols = [tile(base + t) + tile(base + nt + t) for t in range(nt)]
        else:
            cols = [tile(base + t) for t in range(nt)]
        outs.append(jnp.concatenate(cols, axis=1))
        base += (2 if tp else 1) * nt
    return outs


def _sigmoid(x):
    return 1.0 / (1.0 + jnp.exp(-x))


def _inproj_kernel(x_ref, g_ref, w_ref, o_ref, of_ref, h_scr, *, n_col_tiles):
    j = pl.program_id(1)

    @pl.when(j == 0)
    def _():
        x = x_ref[...]
        ms = jnp.mean(x * x, axis=-1, keepdims=True)
        h_scr[...] = _bf(x * lax.rsqrt(ms + RMS_EPS) * g_ref[...])

    acc = _dot(h_scr[...], w_ref[...])
    o_ref[...] = _bf(acc)

    @pl.when(j == n_col_tiles - 1)
    def _():
        of_ref[...] = acc[:, acc.shape[1] - LANES:]


def _inproj(x2, norm_g, w_perm, *, tm, tn):
    T, D = x2.shape
    N = w_perm.shape[1]
    assert T % tm == 0 and N % tn == 0
    nj = N // tn
    return pl.pallas_call(
        functools.partial(_inproj_kernel, n_col_tiles=nj),
        grid=(T // tm, nj),
        in_specs=[
            pl.BlockSpec((tm, D), lambda i, j: (i, 0)),
            pl.BlockSpec((1, D), lambda i, j: (0, 0)),
            pl.BlockSpec((D, tn), lambda i, j: (0, j)),
        ],
        out_specs=[
            pl.BlockSpec((tm, tn), lambda i, j: (i, j)),
            pl.BlockSpec((tm, LANES), lambda i, j: (i, 0)),
        ],
        out_shape=[
            jax.ShapeDtypeStruct((T, N), BF16),
            jax.ShapeDtypeStruct((T, LANES), F32),
        ],
        scratch_shapes=[pltpu.VMEM((tm, D), BF16)],
        compiler_params=pltpu.CompilerParams(
            dimension_semantics=("parallel", "arbitrary"), vmem_limit_bytes=VMEM_LIMIT),
        name="inproj",
    )(x2, norm_g, w_perm)


def _fprep_kernel(f_ref, fb_ref, sel_ref, aug_ref, *, blk):
    S = f_ref.shape[1]
    row = lax.broadcasted_iota(jnp.int32, (blk, blk), 0)
    col = lax.broadcasted_iota(jnp.int32, (blk, blk), 1)
    tril = _bf(jnp.where(row >= col, 1.0, 0.0))
    carry = jnp.zeros((1, LANES), F32)
    for i in range(S // blk):
        z = f_ref[0, i * blk:(i + 1) * blk, :] + fb_ref[...]
        lf = jnp.minimum(z, 0.0) - jnp.log1p(jnp.exp(-jnp.abs(z)))
        hi, mid, lo = _split3(lf)
        c = (_dot(tril, hi) + _dot(tril, mid) + _dot(tril, lo)) + carry
        pieces = jnp.concatenate(_split3(c * LOG2E), axis=1)
        aug_ref[0, i * blk:(i + 1) * blk, :] = _bf(_dot(pieces, sel_ref[...]))
        carry = c[blk - 1:blk, :]


def _fprep(uf3, fb_pad, sel):
    B, S, _ = uf3.shape
    return pl.pallas_call(
        functools.partial(_fprep_kernel, blk=256),
        grid=(B,),
        in_specs=[
            pl.BlockSpec((1, S, LANES), lambda b: (b, 0, 0)),
            pl.BlockSpec((1, LANES), lambda b: (0, 0)),
            pl.BlockSpec(sel.shape, lambda b: (0, 0)),
        ],
        out_specs=pl.BlockSpec((1, S, LANES), lambda b: (b, 0, 0)),
        out_shape=jax.ShapeDtypeStruct((B, S, LANES), BF16),
        compiler_params=pltpu.CompilerParams(dimension_semantics=("parallel",)),
        name="fprep",
    )(uf3, fb_pad, sel)


def _aug_selector(n_heads):
    assert AUG0 + 6 * n_heads <= LANES
    r = jnp.arange(LANES)[:, None]
    c = jnp.arange(LANES)[None, :]
    blocks = []
    for i in range(3):
        m = jnp.where(c == AUG0 + 6 * r + i, -1.0, jnp.where(c == AUG0 + 6 * r + 3 + i, 1.0, 0.0))
        blocks.append(jnp.where(r < n_heads, m, 0.0))
    return _bf(jnp.concatenate(blocks, axis=0))


_P_MU_R, _P_MU_K, _P_MU_V, _P_MU_G, _P_W0, _P_A0, _P_KK, _P_KA, _P_RK, _P_LW, _P_LB = range(11)


def _shift_mix(x, carry_row, mu):
    rolled = pltpu.roll(x, shift=1, axis=0)
    row = lax.broadcasted_iota(jnp.int32, x.shape, 0)
    prev = jnp.where(row == 0, carry_row, rolled)
    return x + (prev - x) * mu


def _block_diag(x, half_masks):
    xb = _bf(x)
    n_tiles = xb.shape[1] // LANES
    zero = jnp.zeros((xb.shape[0], LANES), BF16)
    rows = []
    for h in range(xb.shape[1] // HEAD):
        t = h // 2
        piece = xb[:, t * LANES:(t + 1) * LANES] * half_masks[h % 2]
        rows.append(jnp.concatenate([piece if i == t else zero for i in range(n_tiles)], axis=1))
    return jnp.concatenate(rows, axis=0)


def _rwkv_kernel(r_ref, k_ref, v_ref, g_ref, wa_ref, pv_ref, muwa_ref, lora_ref, gm_ref,
                 o_ref, state_scr, carry_scr):
    c = pl.program_id(1)
    NB = r_ref.shape[0]
    L = r_ref.shape[1]
    DA = r_ref.shape[2]
    n_groups = DA // GW

    @pl.when(c == 0)
    def _():
        state_scr[...] = jnp.zeros_like(state_scr)
        carry_scr[...] = jnp.zeros_like(carry_scr)

    def prm(i):
        return pv_ref[i:i + 1, :]

    g128 = gm_ref[...]
    row = lax.broadcasted_iota(jnp.int32, (L, L), 0)
    col = lax.broadcasted_iota(jnp.int32, (L, L), 1)
    tril = _bf(jnp.where(row >= col, 1.0, 0.0))
    lane128 = lax.broadcasted_iota(jnp.int32, (1, LANES), 1)
    half_masks = [_bf(jnp.where(lane128 // HEAD == i, 1.0, 0.0)) for i in range(2)]
    prow = lax.broadcasted_iota(jnp.int32, (L, GW), 0)
    pcol = lax.broadcasted_iota(jnp.int32, (L, GW), 1) & (HEAD - 1)
    incl_p = prow >= pcol
    strict_p = prow > pcol
    eye_p = jnp.where(prow == pcol, 1.0, 0.0)
    st_mask = (lax.broadcasted_iota(jnp.int32, (GW, GW), 0) // HEAD
               == lax.broadcasted_iota(jnp.int32, (GW, GW), 1) // HEAD)
    bd = lambda x: _block_diag(x, half_masks)

    pre = []
    for bb in range(NB):
        raw = [ref[bb].astype(F32) for ref in (r_ref, k_ref, v_ref, g_ref, wa_ref)]
        mus = [prm(_P_MU_R), prm(_P_MU_K), prm(_P_MU_V), prm(_P_MU_G), muwa_ref[...]]
        mixed = []
        for i, x in enumerate(raw):
            wdt = x.shape[1]
            mixed.append(_shift_mix(x, carry_scr[bb, i:i + 1, 0:wdt], mus[i]))
            carry_scr[bb, i:i + 1, 0:wdt] = x[L - 1:L, :]
        r, k, v, gate, wa = mixed

        wa_act = jnp.where(lane128 < HEAD, jnp.tanh(wa), wa)
        lo = _dot(_bf(wa_act), lora_ref[...])
        ld = DECAY_SCALE * _sigmoid(prm(_P_W0) + lo[:, 0:DA])
        a = _sigmoid(prm(_P_A0) + lo[:, DA:2 * DA])

        kk = k * prm(_P_KK)
        k2 = k * (1.0 + (a - 1.0) * prm(_P_KA))
        ssq, bon = _head_sums([kk * kk, r * k2 * prm(_P_RK)], g128, (False, False))
        kk = kk * jnp.minimum(lax.rsqrt(ssq), 1e12)

        h3, m3, l3 = _split3(ld)
        cum = _dot(tril, h3) + _dot(tril, m3) + _dot(tril, l3)
        e_neg = jnp.exp(-cum)
        w_l = jnp.exp(cum[L - 1:L, :])
        a_t = -kk * jnp.exp(cum - ld)
        r_t = r * jnp.exp(cum)
        b_t = (kk * a) * e_neg
        k_t = k2 * e_neg
        pre.append(dict(v=v, gate=gate, bonus=bon * v, w_l=w_l, a_t=a_t, r_t=r_t, b_t=b_t, k_t=k_t))

    streams = [(bb, g) for bb in range(NB) for g in range(n_groups)]
    gsl = lambda g: slice(g * GW, (g + 1) * GW)
    P = lambda name, st: pre[st[0]][name][:, gsl(st[1])]
    sidx = lambda st: st[0] * n_groups + st[1]

    s0 = [state_scr[sidx(st)] for st in streams]
    s0b = [_bf(x) for x in s0]
    ar = [_bf(jnp.concatenate([P("a_t", st), P("r_t", st)], axis=0)) for st in streams]
    bk = [jnp.concatenate([bd(P("b_t", st)), bd(P("k_t", st))], axis=0) for st in streams]
    pm = [_dot_nt(ar[i], bk[i]) for i in range(len(streams))]
    a_ab = [jnp.where(strict_p, p[0:L, 0:GW], 0.0) for p in pm]
    a_ak = [jnp.where(strict_p, p[0:L, GW:2 * GW], 0.0) for p in pm]
    a_rb = [jnp.where(incl_p, p[L:2 * L, 0:GW], 0.0) for p in pm]
    a_rk = [jnp.where(incl_p, p[L:2 * L, GW:2 * GW], 0.0) for p in pm]
    vbd = [bd(P("v", st)) for st in streams]
    ars = [_dot_nt(ar[i], s0b[i]) for i in range(len(streams))]
    akv = [_dot(_bf(jnp.concatenate([a_ak[i], a_rk[i]], axis=0)), vbd[i])
           for i in range(len(streams))]

    pw = [_dot(_bf(x), bd(x)) for x in a_ab]
    tinv = [eye_p + x for x in a_ab]
    n = 2
    while n < L:
        last = 2 * n >= L
        pwb = [bd(x) for x in pw]
        if last:
            tinv = [tinv[i] + _dot(_bf(tinv[i]), pwb[i]) for i in range(len(streams))]
        else:
            both = [_dot(_bf(jnp.concatenate([pw[i], tinv[i]], axis=0)), pwb[i])
                    for i in range(len(streams))]
            pw = [x[0:L] for x in both]
            tinv = [tinv[i] + both[i][L:2 * L] for i in range(len(streams))]
        n *= 2

    u = [_dot(_bf(tinv[i]), bd(ars[i][0:L] + akv[i][0:L])) for i in range(len(streams))]
    ys = [ars[i][L:2 * L] + akv[i][L:2 * L] + _dot(_bf(a_rb[i]), bd(u[i]))
          for i in range(len(streams))]
    for i, st in enumerate(streams):
        w_l = P("w_l", st)
        uv = _bf(jnp.concatenate([u[i], P("v", st)], axis=0))
        bkh = _bf(jnp.concatenate([P("b_t", st) * w_l, P("k_t", st) * w_l], axis=0))
        state_scr[sidx(st)] = s0[i] * w_l + jnp.where(st_mask, _dot_tn(uv, bkh), 0.0)

    inv_n = 1.0 / HEAD
    y = [jnp.concatenate([ys[bb * n_groups + g] for g in range(n_groups)], axis=1) for bb in range(NB)]
    mean = [m * inv_n for m in _head_sums(y, g128, (True,) * NB)]
    yc = [y[bb] - mean[bb] for bb in range(NB)]
    var = [s * inv_n for s in _head_sums([x * x for x in yc], g128, (False,) * NB)]
    for bb in range(NB):
        yn = yc[bb] * lax.rsqrt(var[bb] + LNX_EPS) * prm(_P_LW) + prm(_P_LB)
        gate = pre[bb]["gate"]
        o_ref[bb] = _bf((yn + pre[bb]["bonus"]) * (gate * _sigmoid(gate)))


def _rwkv(u3, pvec, mu_wa, lora, g128, *, col_r, col_wa, d_a):
    B, S, _ = u3.shape
    L = CHUNK
    nb = RWKV_NB
    assert L == HEAD and d_a % GW == 0 and B % nb == 0 and S % L == 0
    cb = col_r // d_a
    blk = lambda off: pl.BlockSpec((nb, L, d_a), lambda b, c, off=off: (b, c, cb + off))
    full = lambda arr: pl.BlockSpec(arr.shape, lambda b, c: (0,) * arr.ndim)
    return pl.pallas_call(
        _rwkv_kernel,
        grid=(B // nb, S // L),
        in_specs=[
            blk(0), blk(1), blk(2), blk(3),
            pl.BlockSpec((nb, L, LANES), lambda b, c: (b, c, col_wa // LANES)),
            full(pvec), full(mu_wa), full(lora), full(g128),
        ],
        out_specs=pl.BlockSpec((nb, L, d_a), lambda b, c: (b, c, 0)),
        out_shape=jax.ShapeDtypeStruct((B, S, d_a), BF16),
        scratch_shapes=[
            pltpu.VMEM((nb * (d_a // GW), GW, GW), F32),
            pltpu.VMEM((nb, 8, d_a), F32),
        ],
        compiler_params=pltpu.CompilerParams(
            dimension_semantics=("parallel", "arbitrary"), vmem_limit_bytes=VMEM_LIMIT),
        name="rwkv7",
    )(u3, u3, u3, u3, u3, pvec, mu_wa, lora, g128)


def _fox_kernel(q_ref, k_ref, v_ref, g_ref, aq_ref, ak_ref, qg_ref, kg_ref, gm_ref,
                o_ref, kp_scr, vt_scr, *, tq, tk):
    p = pl.program_id(1)
    qi = pl.program_id(2)
    S = k_ref.shape[1]
    nh = q_ref.shape[2] // HEAD
    g128 = gm_ref[...]
    lane = lax.broadcasted_iota(jnp.int32, (1, LANES), 1)
    feat = lane < HEAD

    def minus_lanes(hh):
        lo = AUG0 + 6 * (nh * p + hh)
        return (lane >= lo) & (lane < lo + 3)

    def plus_lanes(hh):
        lo = AUG0 + 6 * (nh * p + hh) + 3
        return (lane >= lo) & (lane < lo + 3)

    def head_tile(x, hh):
        t = x[:, (hh // 2) * LANES:(hh // 2 + 1) * LANES]
        return pltpu.roll(t, HEAD, axis=1) if hh % 2 else t

    @pl.when(qi == 0)
    def _():
        for i in range(S // tk):
            rows = slice(i * tk, (i + 1) * tk)
            kb = k_ref[0, rows, :].astype(F32)
            (ssq,) = _head_sums([kb * kb], g128, (False,))
            kn = kb * lax.rsqrt(ssq * (1.0 / HEAD) + RMS_EPS) * kg_ref[...]
            aug = ak_ref[0, rows, :].astype(F32)
            for hh in range(nh):
                kp = jnp.where(feat, head_tile(kn, hh), jnp.where(plus_lanes(hh), 1.0, aug))
                kp_scr[hh, rows, :] = _bf(kp)
            vt_scr[:, rows] = _bf(jnp.transpose(v_ref[0, rows, :].astype(F32)))

    q = q_ref[0].astype(F32)
    (ssq,) = _head_sums([q * q], g128, (False,))
    qn = q * lax.rsqrt(ssq * (1.0 / HEAD) + RMS_EPS) * (qg_ref[...] * (HEAD ** -0.5 * LOG2E))
    augq = aq_ref[0].astype(F32)
    qp = [_bf(jnp.where(feat, head_tile(qn, hh),
                        jnp.where(minus_lanes(hh), 1.0, jnp.where(plus_lanes(hh), augq, 0.0))))
          for hh in range(nh)]

    def step(j, carry, masked):
        ks = pl.multiple_of(j * tk, tk)
        st = [_dot_nt(kp_scr[hh, pl.ds(ks, tk), :], qp[hh]) for hh in range(nh)]
        if masked:
            keyi = j * tk + lax.broadcasted_iota(jnp.int32, (tk, tq), 0)
            qryi = qi * tq + lax.broadcasted_iota(jnp.int32, (tk, tq), 1)
            st = [jnp.where(qryi >= keyi, x, -1e30) for x in st]
        m_new = [jnp.maximum(carry[hh][0], jnp.max(st[hh], axis=0, keepdims=True)) for hh in range(nh)]
        alpha = [jnp.exp2(carry[hh][0] - m_new[hh]) for hh in range(nh)]
        pt = [jnp.exp2(st[hh] - m_new[hh]) for hh in range(nh)]
        l_new = [alpha[hh] * carry[hh][1] + jnp.sum(pt[hh], axis=0, keepdims=True) for hh in range(nh)]
        pv = [_dot(vt_scr[hh * HEAD:(hh + 1) * HEAD, pl.ds(ks, tk)], _bf(pt[hh]))
              for hh in range(nh)]
        return tuple((m_new[hh], l_new[hh], alpha[hh] * carry[hh][2] + pv[hh]) for hh in range(nh))

    init = tuple((jnp.full((1, tq), -1e30, F32), jnp.zeros((1, tq), F32),
                  jnp.zeros((HEAD, tq), F32)) for _ in range(nh))
    nfull = (qi * tq) // tk
    carry = lax.fori_loop(0, nfull, lambda j, c: step(j, c, False), init)
    for d in range(tq // tk):
        carry = step(nfull + d, carry, True)

    ot = jnp.concatenate([acc * (1.0 / l) for (_, l, acc) in carry], axis=0)
    g = g_ref[0].astype(F32)
    o_ref[0] = _bf(jnp.transpose(ot) * (g * _sigmoid(g)))


def _fox(u3, aug, qg2, kg2, g128, *, col_q, d_b, tq, tk):
    B, S, _ = u3.shape
    fw = qg2.shape[1]
    assert S % tq == 0 and tq % tk == 0 and d_b % fw == 0
    cq, ck, cv, cg = ((col_q + i * d_b) // fw for i in range(4))
    return pl.pallas_call(
        functools.partial(_fox_kernel, tq=tq, tk=tk),
        grid=(B, d_b // fw, S // tq),
        in_specs=[
            pl.BlockSpec((1, tq, fw), lambda b, p, i: (b, i, cq + p)),
            pl.BlockSpec((1, S, fw), lambda b, p, i: (b, 0, ck + p)),
            pl.BlockSpec((1, S, fw), lambda b, p, i: (b, 0, cv + p)),
            pl.BlockSpec((1, tq, fw), lambda b, p, i: (b, i, cg + p)),
            pl.BlockSpec((1, tq, LANES), lambda b, p, i: (b, i, 0)),
            pl.BlockSpec((1, S, LANES), lambda b, p, i: (b, 0, 0)),
            pl.BlockSpec((1, fw), lambda b, p, i: (0, 0)),
            pl.BlockSpec((1, fw), lambda b, p, i: (0, 0)),
            pl.BlockSpec((LANES, LANES), lambda b, p, i: (0, 0)),
        ],
        out_specs=pl.BlockSpec((1, tq, fw), lambda b, p, i: (b, i, p)),
        out_shape=jax.ShapeDtypeStruct((B, S, d_b), BF16),
        scratch_shapes=[
            pltpu.VMEM((fw // HEAD, S, LANES), BF16),
            pltpu.VMEM((fw, S), BF16),
        ],
        compiler_params=pltpu.CompilerParams(
            dimension_semantics=("parallel", "parallel", "arbitrary"), vmem_limit_bytes=VMEM_LIMIT),
        name="fox",
    )(u3, u3, u3, u3, aug, aug, qg2, kg2, g128)


def _out_kernel(x_ref, ya_ref, yb_ref, ga_ref, gb_ref, woa_ref, wob_ref, wo_ref, fg_ref, o_ref):
    za = _dot(ya_ref[...], woa_ref[...])
    zb = _dot(yb_ref[...], wob_ref[...])
    merged = (_sigmoid(ga_ref[...].astype(F32)) * za + _sigmoid(gb_ref[...].astype(F32)) * zb)
    o = x_ref[...] + _dot(_bf(merged), wo_ref[...])
    ms = jnp.mean(o * o, axis=-1, keepdims=True)
    o_ref[...] = o * lax.rsqrt(ms + RMS_EPS) * fg_ref[...]


def _out(x2, ya2, yb2, u2, woa, wob, wo, fg, *, tm):
    T, D = x2.shape
    assert T % tm == 0
    full = lambda arr: pl.BlockSpec(arr.shape, lambda i: (0,) * arr.ndim)
    return pl.pallas_call(
        _out_kernel,
        grid=(T // tm,),
        in_specs=[
            pl.BlockSpec((tm, D), lambda i: (i, 0)),
            pl.BlockSpec((tm, ya2.shape[1]), lambda i: (i, 0)),
            pl.BlockSpec((tm, yb2.shape[1]), lambda i: (i, 0)),
            pl.BlockSpec((tm, D), lambda i: (i, 0)),
            pl.BlockSpec((tm, D), lambda i: (i, 1)),
            full(woa), full(wob), full(wo), full(fg),
        ],
        out_specs=pl.BlockSpec((tm, D), lambda i: (i, 0)),
        out_shape=jax.ShapeDtypeStruct((T, D), F32),
        compiler_params=pltpu.CompilerParams(
            dimension_semantics=("parallel",), vmem_limit_bytes=VMEM_LIMIT),
        name="outstage",
    )(x2, ya2, yb2, u2, u2, woa, wob, wo, fg)


def _block_ones(width):
    i = jnp.arange(width) // HEAD
    return (i[:, None] == i[None, :]).astype(BF16)


def _layer(x2, B, S, norm_g, w_in, shift_mu, w_lora_up, w0, a_lora_up, a0, k_k, k_a, r_k,
           lnx_w, lnx_b, f_bias, q_norm_g, k_norm_g, w_out_a, w_out_b, w_out, out_gain):
    T, D = x2.shape
    d_a = w0.shape[0]
    d_b = w_out_b.shape[0]
    rank = w_lora_up.shape[0]
    h_b = f_bias.shape[0]
    rw = 4 * d_a + 2 * rank
    fx = 4 * d_b + h_b

    ca = w_in[:, :rw]
    cb = w_in[:, rw:rw + fx]
    cg = w_in[:, rw + fx:]
    wdad = ca[:, 3 * d_a:3 * d_a + 2 * rank]
    flog = jnp.pad(cb[:, 4 * d_b:], ((0, 0), (0, LANES - h_b)))
    w_perm = _bf(jnp.concatenate(
        [cg, ca[:, :3 * d_a], ca[:, 3 * d_a + 2 * rank:], cb[:, :4 * d_b], wdad, flog], axis=1))
    col_r = 2 * D
    col_q = col_r + 4 * d_a
    col_wa = col_q + 4 * d_b

    u2, uf = _inproj(x2, norm_g.reshape(1, D), w_perm, tm=2048, tn=1280)
    u3 = u2.reshape(B, S, u2.shape[1])

    mu = shift_mu
    rows = [mu[:d_a], mu[d_a:2 * d_a], mu[2 * d_a:3 * d_a], mu[3 * d_a + 2 * rank:],
            w0, a0, k_k, k_a, r_k.reshape(-1), lnx_w, lnx_b]
    pvec = jnp.stack(rows + [jnp.zeros_like(w0)] * (16 - len(rows)), axis=0)
    mu_wa = mu[3 * d_a:3 * d_a + 2 * rank].reshape(1, 2 * rank)
    z = jnp.zeros((rank, d_a), F32)
    lora = _bf(jnp.concatenate(
        [jnp.concatenate([w_lora_up, z], axis=1), jnp.concatenate([z, a_lora_up], axis=1)], axis=0))
    ya = _rwkv(u3, pvec, mu_wa, lora, _block_ones(LANES), col_r=col_r, col_wa=col_wa, d_a=d_a)

    fb_pad = jnp.pad(f_bias, (0, LANES - h_b)).reshape(1, LANES)
    aug = _fprep(uf.reshape(B, S, LANES), fb_pad, _aug_selector(h_b))
    qg2 = jnp.tile(q_norm_g, FOX_HEADS).reshape(1, FOX_HEADS * HEAD)
    kg2 = jnp.tile(k_norm_g, FOX_HEADS).reshape(1, FOX_HEADS * HEAD)
    yb = _fox(u3, aug, qg2, kg2, _block_ones(LANES), col_q=col_q, d_b=d_b, tq=512, tk=512)

    return _out(x2, ya.reshape(T, d_a), yb.reshape(T, d_b), u2,
                _bf(w_out_a), _bf(w_out_b), _bf(w_out), out_gain.reshape(1, D), tm=512)


def kernel(x, norm_g, w_in, shift_mu, w_lora_up, w0, a_lora_up, a0, k_k, k_a, r_k, lnx_w, lnx_b,
           f_bias, q_norm_g, k_norm_g, w_out_a, w_out_b, w_out, final_norm_g):
    B, S, D = x.shape
    depth = w_in.shape[0]
    assert depth == 1, "the fused output stage applies the final norm after the single layer"
    x2 = x.reshape(B * S, D)
    out = _layer(x2, B, S, norm_g[0], w_in[0], shift_mu[0], w_lora_up[0], w0[0], a_lora_up[0],
                 a0[0], k_k[0], k_a[0], r_k[0], lnx_w[0], lnx_b[0], f_bias[0], q_norm_g[0],
                 k_norm_g[0], w_out_a[0], w_out_b[0], w_out[0], final_norm_g)
    return out.reshape(B, S, D)
```

```python
import functools

import jax
import jax.numpy as jnp
from jax import lax
from jax.experimental import pallas as pl
from jax.experimental.pallas import tpu as pltpu

F32 = jnp.float32
BF16 = jnp.bfloat16

HEAD = 64
LANES = 128
RMS_EPS = 1e-6
LNX_EPS = 64e-5
CHUNK = 64
GROUP = 4
GW = GROUP * HEAD
RWKV_NB = 8
FOX_HEADS = 4
AUG0 = HEAD
VROWS = HEAD + 16
LOG2E = 1.4426950408889634
DECAY_SCALE = -0.6065306597126334
VMEM_LIMIT = 56 * 1024 * 1024


def _bf(x):
    return x.astype(BF16)


def _dot(a, b):
    return jnp.dot(a, b, preferred_element_type=F32)


def _dot_nt(a, b):
    return lax.dot_general(a, b, (((1,), (1,)), ((), ())), preferred_element_type=F32)


def _dot_tn(a, b):
    return lax.dot_general(a, b, (((0,), (0,)), ((), ())), preferred_element_type=F32)


def _split2(x):
    hi = _bf(x)
    lo = _bf(x - hi.astype(F32))
    return hi, lo


def _split3(x):
    hi = _bf(x)
    r1 = x - hi.astype(F32)
    mid = _bf(r1)
    lo = _bf(r1 - mid.astype(F32))
    return hi, mid, lo


def _head_sums(xs, g128, two_pass):
    m, w = xs[0].shape
    nt = w // LANES
    parts = []
    for x, tp in zip(xs, two_pass):
        for piece in (_split2(x) if tp else (_bf(x),)):
            parts += [piece[:, t * LANES:(t + 1) * LANES] for t in range(nt)]
    r = _dot(jnp.concatenate(parts, axis=0), g128)
    tile = lambda i: r[i * m:(i + 1) * m]
    outs, base = [], 0
    for tp in two_pass:
        if tp:
            cols = [tile(base + t) + tile(base + nt + t) for t in range(nt)]
        else:
            cols = [tile(base + t) for t in range(nt)]
        outs.append(jnp.concatenate(cols, axis=1))
        base += (2 if tp else 1) * nt
    return outs


def _sigmoid(x):
    return 1.0 / (1.0 + jnp.exp(-x))


def _inproj_kernel(x_ref, g_ref, w_ref, o_ref, of_ref, h_scr, *, n_col_tiles):
    j = pl.program_id(1)

    @pl.when(j == 0)
    def _():
        x = x_ref[...]
        ms = jnp.mean(x * x, axis=-1, keepdims=True)
        h_scr[...] = _bf(x * lax.rsqrt(ms + RMS_EPS) * g_ref[...])

    acc = _dot(h_scr[...], w_ref[...])
    o_ref[...] = _bf(acc)

    @pl.when(j == n_col_tiles - 1)
    def _():
        of_ref[...] = acc[:, acc.shape[1] - LANES:]


def _inproj(x2, norm_g, w_perm, *, tm, tn):
    T, D = x2.shape
    N = w_perm.shape[1]
    assert T % tm == 0 and N % tn == 0
    nj = N // tn
    return pl.pallas_call(
        functools.partial(_inproj_kernel, n_col_tiles=nj),
        grid=(T // tm, nj),
        in_specs=[
            pl.BlockSpec((tm, D), lambda i, j: (i, 0)),
            pl.BlockSpec((1, D), lambda i, j: (0, 0)),
            pl.BlockSpec((D, tn), lambda i, j: (0, j)),
        ],
        out_specs=[
            pl.BlockSpec((tm, tn), lambda i, j: (i, j)),
            pl.BlockSpec((tm, LANES), lambda i, j: (i, 0)),
        ],
        out_shape=[
            jax.ShapeDtypeStruct((T, N), BF16),
            jax.ShapeDtypeStruct((T, LANES), F32),
        ],
        scratch_shapes=[pltpu.VMEM((tm, D), BF16)],
        compiler_params=pltpu.CompilerParams(
            dimension_semantics=("parallel", "arbitrary"), vmem_limit_bytes=VMEM_LIMIT),
        name="inproj",
    )(x2, norm_g, w_perm)


def _fprep_kernel(f_ref, fb_ref, sel_ref, aug_ref, *, blk):
    S = f_ref.shape[1]
    row = lax.broadcasted_iota(jnp.int32, (blk, blk), 0)
    col = lax.broadcasted_iota(jnp.int32, (blk, blk), 1)
    tril = _bf(jnp.where(row >= col, 1.0, 0.0))
    carry = jnp.zeros((1, LANES), F32)
    for i in range(S // blk):
        z = f_ref[0, i * blk:(i + 1) * blk, :] + fb_ref[...]
        lf = jnp.minimum(z, 0.0) - jnp.log1p(jnp.exp(-jnp.abs(z)))
        hi, mid, lo = _split3(lf)
        c = (_dot(tril, hi) + _dot(tril, mid) + _dot(tril, lo)) + carry
        pieces = jnp.concatenate(_split3(c * LOG2E), axis=1)
        aug_ref[0, i * blk:(i + 1) * blk, :] = _bf(_dot(pieces, sel_ref[...]))
        carry = c[blk - 1:blk, :]


def _fprep(uf3, fb_pad, sel):
    B, S, _ = uf3.shape
    return pl.pallas_call(
        functools.partial(_fprep_kernel, blk=256),
        grid=(B,),
        in_specs=[
            pl.BlockSpec((1, S, LANES), lambda b: (b, 0, 0)),
            pl.BlockSpec((1, LANES), lambda b: (0, 0)),
            pl.BlockSpec(sel.shape, lambda b: (0, 0)),
        ],
        out_specs=pl.BlockSpec((1, S, LANES), lambda b: (b, 0, 0)),
        out_shape=jax.ShapeDtypeStruct((B, S, LANES), BF16),
        compiler_params=pltpu.CompilerParams(dimension_semantics=("parallel",)),
        name="fprep",
    )(uf3, fb_pad, sel)


def _aug_selector(n_heads):
    assert AUG0 + 6 * n_heads <= LANES
    r = jnp.arange(LANES)[:, None]
    c = jnp.arange(LANES)[None, :]
    blocks = []
    for i in range(3):
        m = jnp.where(c == AUG0 + 6 * r + i, -1.0, jnp.where(c == AUG0 + 6 * r + 3 + i, 1.0, 0.0))
        blocks.append(jnp.where(r < n_heads, m, 0.0))
    return _bf(jnp.concatenate(blocks, axis=0))


_P_MU_R, _P_MU_K, _P_MU_V, _P_MU_G, _P_W0, _P_A0, _P_KK, _P_KA, _P_RK, _P_LW, _P_LB = range(11)


def _shift_mix(x, carry_row, mu):
    rolled = pltpu.roll(x, shift=1, axis=0)
    row = lax.broadcasted_iota(jnp.int32, x.shape, 0)
    prev = jnp.where(row == 0, carry_row, rolled)
    return x + (prev - x) * mu


def _block_diag(x, half_masks):
    xb = _bf(x)
    n_tiles = xb.shape[1] // LANES
    zero = jnp.zeros((xb.shape[0], LANES), BF16)
    rows = []
    for h in range(xb.shape[1] // HEAD):
        t = h // 2
        piece = xb[:, t * LANES:(t + 1) * LANES] * half_masks[h % 2]
        rows.append(jnp.concatenate([piece if i == t else zero for i in range(n_tiles)], axis=1))
    return jnp.concatenate(rows, axis=0)


def _rwkv_kernel(r_ref, k_ref, v_ref, g_ref, wa_ref, pv_ref, muwa_ref, lora_ref, gm_ref,
                 o_ref, state_scr, carry_scr):
    c = pl.program_id(1)
    NB = r_ref.shape[0]
    L = r_ref.shape[1]
    DA = r_ref.shape[2]
    n_groups = DA // GW

    @pl.when(c == 0)
    def _():
        state_scr[...] = jnp.zeros_like(state_scr)
        carry_scr[...] = jnp.zeros_like(carry_scr)

    def prm(i):
        return pv_ref[i:i + 1, :]

    g128 = gm_ref[...]
    row = lax.broadcasted_iota(jnp.int32, (L, L), 0)
    col = lax.broadcasted_iota(jnp.int32, (L, L), 1)
    tril = _bf(jnp.where(row >= col, 1.0, 0.0))
    lane128 = lax.broadcasted_iota(jnp.int32, (1, LANES), 1)
    half_masks = [_bf(jnp.where(lane128 // HEAD == i, 1.0, 0.0)) for i in range(2)]
    prow = lax.broadcasted_iota(jnp.int32, (L, GW), 0)
    pcol = lax.broadcasted_iota(jnp.int32, (L, GW), 1) & (HEAD - 1)
    incl_p = prow >= pcol
    strict_p = prow > pcol
    eye_p = jnp.where(prow == pcol, 1.0, 0.0)
    st_mask = (lax.broadcasted_iota(jnp.int32, (GW, GW), 0) // HEAD
               == lax.broadcasted_iota(jnp.int32, (GW, GW), 1) // HEAD)
    bd = lambda x: _block_diag(x, half_masks)

    pre = []
    for bb in range(NB):
        raw = [ref[bb].astype(F32) for ref in (r_ref, k_ref, v_ref, g_ref, wa_ref)]
        mus = [prm(_P_MU_R), prm(_P_MU_K), prm(_P_MU_V), prm(_P_MU_G), muwa_ref[...]]
        mixed = []
        for i, x in enumerate(raw):
            wdt = x.shape[1]
            mixed.append(_shift_mix(x, carry_scr[bb, i:i + 1, 0:wdt], mus[i]))
            carry_scr[bb, i:i + 1, 0:wdt] = x[L - 1:L, :]
        r, k, v, gate, wa = mixed

        wa_act = jnp.where(lane128 < HEAD, jnp.tanh(wa), wa)
        lo = _dot(_bf(wa_act), lora_ref[...])
        ld = DECAY_SCALE * _sigmoid(prm(_P_W0) + lo[:, 0:DA])
        a = _sigmoid(prm(_P_A0) + lo[:, DA:2 * DA])

        kk = k * prm(_P_KK)
        k2 = k * (1.0 + (a - 1.0) * prm(_P_KA))
        ssq, bon = _head_sums([kk * kk, r * k2 * prm(_P_RK)], g128, (False, False))
        kk = kk * jnp.minimum(lax.rsqrt(ssq), 1e12)

        h3, m3, l3 = _split3(ld)
        cum = _dot(tril, h3) + _dot(tril, m3) + _dot(tril, l3)
        e_neg = jnp.exp(-cum)
        w_l = jnp.exp(cum[L - 1:L, :])
        a_t = -kk * jnp.exp(cum - ld)
        r_t = r * jnp.exp(cum)
        b_t = (kk * a) * e_neg
        k_t = k2 * e_neg
        pre.append(dict(v=v, gate=gate, bonus=bon * v, w_l=w_l, a_t=a_t, r_t=r_t, b_t=b_t, k_t=k_t))

    streams = [(bb, g) for bb in range(NB) for g in range(n_groups)]
    gsl = lambda g: slice(g * GW, (g + 1) * GW)
    P = lambda name, st: pre[st[0]][name][:, gsl(st[1])]
    sidx = lambda st: st[0] * n_groups + st[1]

    s0 = [state_scr[sidx(st)] for st in streams]
    s0b = [_bf(x) for x in s0]
    ar = [_bf(jnp.concatenate([P("a_t", st), P("r_t", st)], axis=0)) for st in streams]
    bk = [jnp.concatenate([bd(P("b_t", st)), bd(P("k_t", st))], axis=0) for st in streams]
    pm = [_dot_nt(ar[i], bk[i]) for i in range(len(streams))]
    a_ab = [jnp.where(strict_p, p[0:L, 0:GW], 0.0) for p in pm]
    a_ak = [jnp.where(strict_p, p[0:L, GW:2 * GW], 0.0) for p in pm]
    a_rb = [jnp.where(incl_p, p[L:2 * L, 0:GW], 0.0) for p in pm]
    a_rk = [jnp.where(incl_p, p[L:2 * L, GW:2 * GW], 0.0) for p in pm]
    vbd = [bd(P("v", st)) for st in streams]
    ars = [_dot_nt(ar[i], s0b[i]) for i in range(len(streams))]
    akv = [_dot(_bf(jnp.concatenate([a_ak[i], a_rk[i]], axis=0)), vbd[i])
           for i in range(len(streams))]

    pw = [_dot(_bf(x), bd(x)) for x in a_ab]
    tinv = [eye_p + x for x in a_ab]
    n = 2
    while n < L:
        last = 2 * n >= L
        pwb = [bd(x) for x in pw]
        if last:
            tinv = [tinv[i] + _dot(_bf(tinv[i]), pwb[i]) for i in range(len(streams))]
        else:
            both = [_dot(_bf(jnp.concatenate([pw[i], tinv[i]], axis=0)), pwb[i])
                    for i in range(len(streams))]
            pw = [x[0:L] for x in both]
            tinv = [tinv[i] + both[i][L:2 * L] for i in range(len(streams))]
        n *= 2

    u = [_dot(_bf(tinv[i]), bd(ars[i][0:L] + akv[i][0:L])) for i in range(len(streams))]
    ys = [ars[i][L:2 * L] + akv[i][L:2 * L] + _dot(_bf(a_rb[i]), bd(u[i]))
          for i in range(len(streams))]
    for i, st in enumerate(streams):
        w_l = P("w_l", st)
        uv = _bf(jnp.concatenate([u[i], P("v", st)], axis=0))
        bkh = _bf(jnp.concatenate([P("b_t", st) * w_l, P("k_t", st) * w_l], axis=0))
        state_scr[sidx(st)] = s0[i] * w_l + jnp.where(st_mask, _dot_tn(uv, bkh), 0.0)

    inv_n = 1.0 / HEAD
    y = [jnp.concatenate([ys[bb * n_groups + g] for g in range(n_groups)], axis=1) for bb in range(NB)]
    mean = [m * inv_n for m in _head_sums(y, g128, (True,) * NB)]
    yc = [y[bb] - mean[bb] for bb in range(NB)]
    var = [s * inv_n for s in _head_sums([x * x for x in yc], g128, (False,) * NB)]
    for bb in range(NB):
        yn = yc[bb] * lax.rsqrt(var[bb] + LNX_EPS) * prm(_P_LW) + prm(_P_LB)
        gate = pre[bb]["gate"]
        o_ref[bb] = _bf((yn + pre[bb]["bonus"]) * (gate * _sigmoid(gate)))


def _rwkv(u3, pvec, mu_wa, lora, g128, *, col_r, col_wa, d_a):
    B, S, _ = u3.shape
    L = CHUNK
    nb = RWKV_NB
    assert L == HEAD and d_a % GW == 0 and B % nb == 0 and S % L == 0
    cb = col_r // d_a
    blk = lambda off: pl.BlockSpec((nb, L, d_a), lambda b, c, off=off: (b, c, cb + off))
    full = lambda arr: pl.BlockSpec(arr.shape, lambda b, c: (0,) * arr.ndim)
    return pl.pallas_call(
        _rwkv_kernel,
        grid=(B // nb, S // L),
        in_specs=[
            blk(0), blk(1), blk(2), blk(3),
            pl.BlockSpec((nb, L, LANES), lambda b, c: (b, c, col_wa // LANES)),
            full(pvec), full(mu_wa), full(lora), full(g128),
        ],
        out_specs=pl.BlockSpec((nb, L, d_a), lambda b, c: (b, c, 0)),
        out_shape=jax.ShapeDtypeStruct((B, S, d_a), BF16),
        scratch_shapes=[
            pltpu.VMEM((nb * (d_a // GW), GW, GW), F32),
            pltpu.VMEM((nb, 8, d_a), F32),
        ],
        compiler_params=pltpu.CompilerParams(
            dimension_semantics=("parallel", "arbitrary"), vmem_limit_bytes=VMEM_LIMIT),
        name="rwkv7",
    )(u3, u3, u3, u3, u3, pvec, mu_wa, lora, g128)


def _fox_kernel(q_ref, k_ref, v_ref, g_ref, aq_ref, ak_ref, qg_ref, kg_ref, gm_ref,
                o_ref, kp_scr, vt_scr, sta_scr, stb_scr, m_scr, acc_scr, *, tq, tk):
    p = pl.program_id(1)
    qi = pl.program_id(2)
    S = k_ref.shape[1]
    nh = q_ref.shape[2] // HEAD
    g128 = gm_ref[...]
    lane = lax.broadcasted_iota(jnp.int32, (1, LANES), 1)
    feat = lane < HEAD

    def minus_lanes(hh):
        lo = AUG0 + 6 * (nh * p + hh)
        return (lane >= lo) & (lane < lo + 3)

    def plus_lanes(hh):
        lo = AUG0 + 6 * (nh * p + hh) + 3
        return (lane >= lo) & (lane < lo + 3)

    def head_tile(x, hh):
        t = x[:, (hh // 2) * LANES:(hh // 2 + 1) * LANES]
        return pltpu.roll(t, HEAD, axis=1) if hh % 2 else t

    @pl.when(qi == 0)
    def _():
        for i in range(S // tk):
            rows = slice(i * tk, (i + 1) * tk)
            kb = k_ref[0, rows, :].astype(F32)
            (ssq,) = _head_sums([kb * kb], g128, (False,))
            kn = kb * lax.rsqrt(ssq * (1.0 / HEAD) + RMS_EPS) * kg_ref[...]
            aug = ak_ref[0, rows, :].astype(F32)
            for hh in range(nh):
                kp = jnp.where(feat, head_tile(kn, hh), jnp.where(plus_lanes(hh), 1.0, aug))
                kp_scr[hh, rows, :] = _bf(kp)
            vt = _bf(jnp.transpose(v_ref[0, rows, :].astype(F32)))
            for hh in range(nh):
                vt_scr[hh * VROWS:hh * VROWS + HEAD, rows] = vt[hh * HEAD:(hh + 1) * HEAD, :]
                vt_scr[hh * VROWS + HEAD:(hh + 1) * VROWS, rows] = jnp.ones((VROWS - HEAD, tk), BF16)

    q = q_ref[0].astype(F32)
    (ssq,) = _head_sums([q * q], g128, (False,))
    qn = q * lax.rsqrt(ssq * (1.0 / HEAD) + RMS_EPS) * (qg_ref[...] * (HEAD ** -0.5 * LOG2E))
    augq = aq_ref[0].astype(F32)
    qp = [_bf(jnp.where(feat, head_tile(qn, hh),
                        jnp.where(minus_lanes(hh), 1.0, jnp.where(plus_lanes(hh), augq, 0.0))))
          for hh in range(nh)]

    def scores(j, st_ref):
        ks = pl.multiple_of(j * tk, tk)
        for hh in range(nh):
            st_ref[hh] = _dot_nt(kp_scr[hh, pl.ds(ks, tk), :], qp[hh])

    def softmax_pv(j, st_ref, masked):
        ks = pl.multiple_of(j * tk, tk)
        st = [st_ref[hh] for hh in range(nh)]
        if masked:
            keyi = j * tk + lax.broadcasted_iota(jnp.int32, (tk, tq), 0)
            qryi = qi * tq + lax.broadcasted_iota(jnp.int32, (tk, tq), 1)
            st = [jnp.where(qryi >= keyi, x, -1e30) for x in st]
        m_old = [m_scr[hh, 0:1, :] for hh in range(nh)]
        m_new = [jnp.maximum(m_old[hh], jnp.max(st[hh], axis=0, keepdims=True)) for hh in range(nh)]
        alpha = [jnp.exp2(m_old[hh] - m_new[hh]) for hh in range(nh)]
        pt = [jnp.exp2(_bf(st[hh] - m_new[hh])) for hh in range(nh)]
        pv = [_dot(vt_scr[hh * VROWS:(hh + 1) * VROWS, pl.ds(ks, tk)], pt[hh])
              for hh in range(nh)]
        for hh in range(nh):
            m_scr[hh, 0:1, :] = m_new[hh]
            acc_scr[hh] = alpha[hh] * acc_scr[hh] + pv[hh]

    m_scr[...] = jnp.full(m_scr.shape, -1e30, F32)
    acc_scr[...] = jnp.zeros(acc_scr.shape, F32)
    scores(0, sta_scr)
    n_pairs = qi // 2

    def body(t, _):
        scores(2 * t + 1, stb_scr)
        softmax_pv(2 * t, sta_scr, False)
        scores(2 * t + 2, sta_scr)
        softmax_pv(2 * t + 1, stb_scr, False)
        return 0

    lax.fori_loop(0, n_pairs, body, 0)

    @pl.when(qi % 2 == 1)
    def _():
        scores(qi, stb_scr)
        softmax_pv(qi - 1, sta_scr, False)
        softmax_pv(qi, stb_scr, True)

    @pl.when(qi % 2 == 0)
    def _():
        softmax_pv(qi, sta_scr, True)

    ot = jnp.concatenate([acc_scr[hh, 0:HEAD, :] * (1.0 / acc_scr[hh, HEAD:HEAD + 1, :])
                          for hh in range(nh)], axis=0)
    g = g_ref[0].astype(F32)
    o_ref[0] = _bf(jnp.transpose(ot) * (g * _sigmoid(g)))


def _fox(u3, aug, qg2, kg2, g128, *, col_q, d_b, tq, tk):
    B, S, _ = u3.shape
    fw = qg2.shape[1]
    assert S % tq == 0 and tq == tk and d_b % fw == 0
    cq, ck, cv, cg = ((col_q + i * d_b) // fw for i in range(4))
    return pl.pallas_call(
        functools.partial(_fox_kernel, tq=tq, tk=tk),
        grid=(B, d_b // fw, S // tq),
        in_specs=[
            pl.BlockSpec((1, tq, fw), lambda b, p, i: (b, i, cq + p)),
            pl.BlockSpec((1, S, fw), lambda b, p, i: (b, 0, ck + p)),
            pl.BlockSpec((1, S, fw), lambda b, p, i: (b, 0, cv + p)),
            pl.BlockSpec((1, tq, fw), lambda b, p, i: (b, i, cg + p)),
            pl.BlockSpec((1, tq, LANES), lambda b, p, i: (b, i, 0)),
            pl.BlockSpec((1, S, LANES), lambda b, p, i: (b, 0, 0)),
            pl.BlockSpec((1, fw), lambda b, p, i: (0, 0)),
            pl.BlockSpec((1, fw), lambda b, p, i: (0, 0)),
            pl.BlockSpec((LANES, LANES), lambda b, p, i: (0, 0)),
        ],
        out_specs=pl.BlockSpec((1, tq, fw), lambda b, p, i: (b, i, p)),
        out_shape=jax.ShapeDtypeStruct((B, S, d_b), BF16),
        scratch_shapes=[
            pltpu.VMEM((fw // HEAD, S, LANES), BF16),
            pltpu.VMEM((fw // HEAD * VROWS, S), BF16),
            pltpu.VMEM((fw // HEAD, tk, tq), F32),
            pltpu.VMEM((fw // HEAD, tk, tq), F32),
            pltpu.VMEM((fw // HEAD, 8, tq), F32),
            pltpu.VMEM((fw // HEAD, VROWS, tq), F32),
        ],
        compiler_params=pltpu.CompilerParams(
            dimension_semantics=("parallel", "parallel", "arbitrary"), vmem_limit_bytes=VMEM_LIMIT),
        name="fox",
    )(u3, u3, u3, u3, aug, aug, qg2, kg2, g128)


def _out_kernel(x_ref, ya_ref, yb_ref, ga_ref, gb_ref, woa_ref, wob_ref, wo_ref, fg_ref, o_ref):
    za = _dot(ya_ref[...], woa_ref[...])
    zb = _dot(yb_ref[...], wob_ref[...])
    merged = (_sigmoid(ga_ref[...].astype(F32)) * za + _sigmoid(gb_ref[...].astype(F32)) * zb)
    o = x_ref[...] + _dot(_bf(merged), wo_ref[...])
    ms = jnp.mean(o * o, axis=-1, keepdims=True)
    o_ref[...] = o * lax.rsqrt(ms + RMS_EPS) * fg_ref[...]


def _out(x2, ya2, yb2, u2, woa, wob, wo, fg, *, tm):
    T, D = x2.shape
    assert T % tm == 0
    full = lambda arr: pl.BlockSpec(arr.shape, lambda i: (0,) * arr.ndim)
    return pl.pallas_call(
        _out_kernel,
        grid=(T // tm,),
        in_specs=[
            pl.BlockSpec((tm, D), lambda i: (i, 0)),
            pl.BlockSpec((tm, ya2.shape[1]), lambda i: (i, 0)),
            pl.BlockSpec((tm, yb2.shape[1]), lambda i: (i, 0)),
            pl.BlockSpec((tm, D), lambda i: (i, 0)),
            pl.BlockSpec((tm, D), lambda i: (i, 1)),
            full(woa), full(wob), full(wo), full(fg),
        ],
        out_specs=pl.BlockSpec((tm, D), lambda i: (i, 0)),
        out_shape=jax.ShapeDtypeStruct((T, D), F32),
        compiler_params=pltpu.CompilerParams(
            dimension_semantics=("parallel",), vmem_limit_bytes=VMEM_LIMIT),
        name="outstage",
    )(x2, ya2, yb2, u2, u2, woa, wob, wo, fg)


def _block_ones(width):
    i = jnp.arange(width) // HEAD
    return (i[:, None] == i[None, :]).astype(BF16)


def _layer(x2, B, S, norm_g, w_in, shift_mu, w_lora_up, w0, a_lora_up, a0, k_k, k_a, r_k,
           lnx_w, lnx_b, f_bias, q_norm_g, k_norm_g, w_out_a, w_out_b, w_out, out_gain):
    T, D = x2.shape
    d_a = w0.shape[0]
    d_b = w_out_b.shape[0]
    rank = w_lora_up.shape[0]
    h_b = f_bias.shape[0]
    rw = 4 * d_a + 2 * rank
    fx = 4 * d_b + h_b

    ca = w_in[:, :rw]
    cb = w_in[:, rw:rw + fx]
    cg = w_in[:, rw + fx:]
    wdad = ca[:, 3 * d_a:3 * d_a + 2 * rank]
    flog = jnp.pad(cb[:, 4 * d_b:], ((0, 0), (0, LANES - h_b)))
    w_perm = _bf(jnp.concatenate(
        [cg, ca[:, :3 * d_a], ca[:, 3 * d_a + 2 * rank:], cb[:, :4 * d_b], wdad, flog], axis=1))
    col_r = 2 * D
    col_q = col_r + 4 * d_a
    col_wa = col_q + 4 * d_b

    u2, uf = _inproj(x2, norm_g.reshape(1, D), w_perm, tm=2048, tn=1280)
    u3 = u2.reshape(B, S, u2.shape[1])

    mu = shift_mu
    rows = [mu[:d_a], mu[d_a:2 * d_a], mu[2 * d_a:3 * d_a], mu[3 * d_a + 2 * rank:],
            w0, a0, k_k, k_a, r_k.reshape(-1), lnx_w, lnx_b]
    pvec = jnp.stack(rows + [jnp.zeros_like(w0)] * (16 - len(rows)), axis=0)
    mu_wa = mu[3 * d_a:3 * d_a + 2 * rank].reshape(1, 2 * rank)
    z = jnp.zeros((rank, d_a), F32)
    lora = _bf(jnp.concatenate(
        [jnp.concatenate([w_lora_up, z], axis=1), jnp.concatenate([z, a_lora_up], axis=1)], axis=0))
    ya = _rwkv(u3, pvec, mu_wa, lora, _block_ones(LANES), col_r=col_r, col_wa=col_wa, d_a=d_a)

    fb_pad = jnp.pad(f_bias, (0, LANES - h_b)).reshape(1, LANES)
    aug = _fprep(uf.reshape(B, S, LANES), fb_pad, _aug_selector(h_b))
    qg2 = jnp.tile(q_norm_g, FOX_HEADS).reshape(1, FOX_HEADS * HEAD)
    kg2 = jnp.tile(k_norm_g, FOX_HEADS).reshape(1, FOX_HEADS * HEAD)
    yb = _fox(u3, aug, qg2, kg2, _block_ones(LANES), col_q=col_q, d_b=d_b, tq=512, tk=512)

    return _out(x2, ya.reshape(T, d_a), yb.reshape(T, d_b), u2,
                _bf(w_out_a), _bf(w_out_b), _bf(w_out), out_gain.reshape(1, D), tm=512)


def kernel(x, norm_g, w_in, shift_mu, w_lora_up, w0, a_lora_up, a0, k_k, k_a, r_k, lnx_w, lnx_b,
           f_bias, q_norm_g, k_norm_g, w_out_a, w_out_b, w_out, final_norm_g):
    B, S, D = x.shape
    depth = w_in.shape[0]
    assert depth == 1, "the fused output stage applies the final norm after the single layer"
    x2 = x.reshape(B * S, D)
    out = _layer(x2, B, S, norm_g[0], w_in[0], shift_mu[0], w_lora_up[0], w0[0], a_lora_up[0],
                 a0[0], k_k[0], k_a[0], r_k[0], lnx_w[0], lnx_b[0], f_bias[0], q_norm_g[0],
                 k_norm_g[0], w_out_a[0], w_out_b[0], w_out[0], final_norm_g)
    return out.reshape(B, S, D)
```

```python
import functools

import jax
import jax.numpy as jnp
from jax import lax
from jax.experimental import pallas as pl
from jax.experimental.pallas import tpu as pltpu

F32 = jnp.float32
BF16 = jnp.bfloat16

HEAD = 64
LANES = 128
RMS_EPS = 1e-6
LNX_EPS = 64e-5
CHUNK = 64
GROUP = 4
GW = GROUP * HEAD
RWKV_NB = 8
FOX_HEADS = 4
AUG0 = HEAD
VROWS = HEAD + 16
LOG2E = 1.4426950408889634
DECAY_SCALE = -0.6065306597126334
VMEM_LIMIT = 56 * 1024 * 1024


def _bf(x):
    return x.astype(BF16)


def _dot(a, b):
    return jnp.dot(a, b, preferred_element_type=F32)


def _dot_nt(a, b):
    return lax.dot_general(a, b, (((1,), (1,)), ((), ())), preferred_element_type=F32)


def _dot_tn(a, b):
    return lax.dot_general(a, b, (((0,), (0,)), ((), ())), preferred_element_type=F32)


def _split2(x):
    hi = _bf(x)
    lo = _bf(x - hi.astype(F32))
    return hi, lo


def _split3(x):
    hi = _bf(x)
    r1 = x - hi.astype(F32)
    mid = _bf(r1)
    lo = _bf(r1 - mid.astype(F32))
    return hi, mid, lo


def _head_sums(xs, g128, two_pass):
    m, w = xs[0].shape
    nt = w // LANES
    parts = []
    for x, tp in zip(xs, two_pass):
        for piece in (_split2(x) if tp else (_bf(x),)):
            parts += [piece[:, t * LANES:(t + 1) * LANES] for t in range(nt)]
    r = _dot(jnp.concatenate(parts, axis=0), g128)
    tile = lambda i: r[i * m:(i + 1) * m]
    outs, base = [], 0
    for tp in two_pass:
        if tp:
            cols = [tile(base + t) + tile(base + nt + t) for t in range(nt)]
        else:
            cols = [tile(base + t) for t in range(nt)]
        outs.append(jnp.concatenate(cols, axis=1))
        base += (2 if tp else 1) * nt
    return outs


def _sigmoid(x):
    return 1.0 / (1.0 + jnp.exp(-x))


def _permute_cols_kernel(w_ref, o_ref, *, segs):
    total = sum(width for _, width in segs)
    if total < o_ref.shape[1]:
        t0 = total // LANES * LANES
        o_ref[:, t0:] = jnp.zeros((o_ref.shape[0], o_ref.shape[1] - t0), BF16)
    dst = 0
    for src, width in segs:
        o_ref[:, dst:dst + width] = _bf(w_ref[:, src:src + width])
        dst += width


def _permute_cols(w, segs, *, pad_to, tr):
    rows, cols = w.shape
    n_out = -(-sum(width for _, width in segs) // pad_to) * pad_to
    assert rows % tr == 0
    return pl.pallas_call(
        functools.partial(_permute_cols_kernel, segs=tuple(segs)),
        grid=(rows // tr,),
        in_specs=[pl.BlockSpec((tr, cols), lambda i: (i, 0))],
        out_specs=pl.BlockSpec((tr, n_out), lambda i: (i, 0)),
        out_shape=jax.ShapeDtypeStruct((rows, n_out), BF16),
        compiler_params=pltpu.CompilerParams(
            dimension_semantics=("parallel",), vmem_limit_bytes=VMEM_LIMIT),
        name="wperm",
    )(w)


def _inproj_kernel(x_ref, g_ref, w_ref, o_ref, of_ref, h_scr, *, n_col_tiles):
    j = pl.program_id(1)

    @pl.when(j == 0)
    def _():
        x = x_ref[...]
        ms = jnp.mean(x * x, axis=-1, keepdims=True)
        h_scr[...] = _bf(x * lax.rsqrt(ms + RMS_EPS) * g_ref[...])

    acc = _dot(h_scr[...], w_ref[...])
    o_ref[...] = _bf(acc)

    @pl.when(j == n_col_tiles - 1)
    def _():
        of_ref[...] = acc[:, acc.shape[1] - LANES:]


def _inproj(x2, norm_g, w_perm, *, tm, tn):
    T, D = x2.shape
    N = w_perm.shape[1]
    assert T % tm == 0 and N % tn == 0
    nj = N // tn
    return pl.pallas_call(
        functools.partial(_inproj_kernel, n_col_tiles=nj),
        grid=(T // tm, nj),
        in_specs=[
            pl.BlockSpec((tm, D), lambda i, j: (i, 0)),
            pl.BlockSpec((1, D), lambda i, j: (0, 0)),
            pl.BlockSpec((D, tn), lambda i, j: (0, j)),
        ],
        out_specs=[
            pl.BlockSpec((tm, tn), lambda i, j: (i, j)),
            pl.BlockSpec((tm, LANES), lambda i, j: (i, 0)),
        ],
        out_shape=[
            jax.ShapeDtypeStruct((T, N), BF16),
            jax.ShapeDtypeStruct((T, LANES), F32),
        ],
        scratch_shapes=[pltpu.VMEM((tm, D), BF16)],
        compiler_params=pltpu.CompilerParams(
            dimension_semantics=("parallel", "arbitrary"), vmem_limit_bytes=VMEM_LIMIT),
        name="inproj",
    )(x2, norm_g, w_perm)


def _fprep_kernel(f_ref, fb_ref, sel_ref, aug_ref, *, blk):
    S = f_ref.shape[1]
    row = lax.broadcasted_iota(jnp.int32, (blk, blk), 0)
    col = lax.broadcasted_iota(jnp.int32, (blk, blk), 1)
    tril = _bf(jnp.where(row >= col, 1.0, 0.0))
    carry = jnp.zeros((1, LANES), F32)
    for i in range(S // blk):
        z = f_ref[0, i * blk:(i + 1) * blk, :] + fb_ref[...]
        lf = jnp.minimum(z, 0.0) - jnp.log1p(jnp.exp(-jnp.abs(z)))
        hi, mid, lo = _split3(lf)
        c = (_dot(tril, hi) + _dot(tril, mid) + _dot(tril, lo)) + carry
        pieces = jnp.concatenate(_split3(c * LOG2E), axis=1)
        aug_ref[0, i * blk:(i + 1) * blk, :] = _bf(_dot(pieces, sel_ref[...]))
        carry = c[blk - 1:blk, :]


def _fprep(uf3, fb_pad, sel):
    B, S, _ = uf3.shape
    return pl.pallas_call(
        functools.partial(_fprep_kernel, blk=256),
        grid=(B,),
        in_specs=[
            pl.BlockSpec((1, S, LANES), lambda b: (b, 0, 0)),
            pl.BlockSpec((1, LANES), lambda b: (0, 0)),
            pl.BlockSpec(sel.shape, lambda b: (0, 0)),
        ],
        out_specs=pl.BlockSpec((1, S, LANES), lambda b: (b, 0, 0)),
        out_shape=jax.ShapeDtypeStruct((B, S, LANES), BF16),
        compiler_params=pltpu.CompilerParams(dimension_semantics=("parallel",)),
        name="fprep",
    )(uf3, fb_pad, sel)


def _aug_selector(n_heads):
    assert AUG0 + 6 * n_heads <= LANES
    r = jnp.arange(LANES)[:, None]
    c = jnp.arange(LANES)[None, :]
    blocks = []
    for i in range(3):
        m = jnp.where(c == AUG0 + 6 * r + i, -1.0, jnp.where(c == AUG0 + 6 * r + 3 + i, 1.0, 0.0))
        blocks.append(jnp.where(r < n_heads, m, 0.0))
    return _bf(jnp.concatenate(blocks, axis=0))


_P_MU_R, _P_MU_K, _P_MU_V, _P_MU_G, _P_W0, _P_A0, _P_KK, _P_KA, _P_RK, _P_LW, _P_LB = range(11)


def _shift_mix(x, carry_row, mu):
    rolled = pltpu.roll(x, shift=1, axis=0)
    row = lax.broadcasted_iota(jnp.int32, x.shape, 0)
    prev = jnp.where(row == 0, carry_row, rolled)
    return x + (prev - x) * mu


def _block_diag(x, half_masks):
    xb = _bf(x)
    n_tiles = xb.shape[1] // LANES
    zero = jnp.zeros((xb.shape[0], LANES), BF16)
    rows = []
    for h in range(xb.shape[1] // HEAD):
        t = h // 2
        piece = xb[:, t * LANES:(t + 1) * LANES] * half_masks[h % 2]
        rows.append(jnp.concatenate([piece if i == t else zero for i in range(n_tiles)], axis=1))
    return jnp.concatenate(rows, axis=0)


def _rwkv_kernel(r_ref, k_ref, v_ref, g_ref, wa_ref, pv_ref, muwa_ref, lora_ref, gm_ref,
                 o_ref, state_scr, carry_scr):
    c = pl.program_id(1)
    NB = r_ref.shape[0]
    L = r_ref.shape[1]
    DA = r_ref.shape[2]
    n_groups = DA // GW

    @pl.when(c == 0)
    def _():
        state_scr[...] = jnp.zeros_like(state_scr)
        carry_scr[...] = jnp.zeros_like(carry_scr)

    def prm(i):
        return pv_ref[i:i + 1, :]

    g128 = gm_ref[...]
    row = lax.broadcasted_iota(jnp.int32, (L, L), 0)
    col = lax.broadcasted_iota(jnp.int32, (L, L), 1)
    tril = _bf(jnp.where(row >= col, 1.0, 0.0))
    lane128 = lax.broadcasted_iota(jnp.int32, (1, LANES), 1)
    half_masks = [_bf(jnp.where(lane128 // HEAD == i, 1.0, 0.0)) for i in range(2)]
    prow = lax.broadcasted_iota(jnp.int32, (L, GW), 0)
    pcol = lax.broadcasted_iota(jnp.int32, (L, GW), 1) & (HEAD - 1)
    incl_p = prow >= pcol
    strict_p = prow > pcol
    eye_p = jnp.where(prow == pcol, 1.0, 0.0)
    st_mask = (lax.broadcasted_iota(jnp.int32, (GW, GW), 0) // HEAD
               == lax.broadcasted_iota(jnp.int32, (GW, GW), 1) // HEAD)
    bd = lambda x: _block_diag(x, half_masks)

    pre = []
    for bb in range(NB):
        raw = [ref[bb].astype(F32) for ref in (r_ref, k_ref, v_ref, g_ref, wa_ref)]
        mus = [prm(_P_MU_R), prm(_P_MU_K), prm(_P_MU_V), prm(_P_MU_G), muwa_ref[...]]
        mixed = []
        for i, x in enumerate(raw):
            wdt = x.shape[1]
            mixed.append(_shift_mix(x, carry_scr[bb, i:i + 1, 0:wdt], mus[i]))
            carry_scr[bb, i:i + 1, 0:wdt] = x[L - 1:L, :]
        r, k, v, gate, wa = mixed

        wa_act = jnp.where(lane128 < HEAD, jnp.tanh(wa), wa)
        lo = _dot(_bf(wa_act), lora_ref[...])
        ld = DECAY_SCALE * _sigmoid(prm(_P_W0) + lo[:, 0:DA])
        a = _sigmoid(prm(_P_A0) + lo[:, DA:2 * DA])

        kk = k * prm(_P_KK)
        k2 = k * (1.0 + (a - 1.0) * prm(_P_KA))
        ssq, bon = _head_sums([kk * kk, r * k2 * prm(_P_RK)], g128, (False, False))
        kk = kk * jnp.minimum(lax.rsqrt(ssq), 1e12)

        h3, m3, l3 = _split3(ld)
        cum = _dot(tril, h3) + _dot(tril, m3) + _dot(tril, l3)
        e_neg = jnp.exp(-cum)
        w_l = jnp.exp(cum[L - 1:L, :])
        a_t = -kk * jnp.exp(cum - ld)
        r_t = r * jnp.exp(cum)
        b_t = (kk * a) * e_neg
        k_t = k2 * e_neg
        pre.append(dict(v=v, gate=gate, bonus=bon * v, w_l=w_l, a_t=a_t, r_t=r_t, b_t=b_t, k_t=k_t))

    streams = [(bb, g) for bb in range(NB) for g in range(n_groups)]
    gsl = lambda g: slice(g * GW, (g + 1) * GW)
    P = lambda name, st: pre[st[0]][name][:, gsl(st[1])]
    sidx = lambda st: st[0] * n_groups + st[1]

    s0 = [state_scr[sidx(st)] for st in streams]
    s0b = [_bf(x) for x in s0]
    ar = [_bf(jnp.concatenate([P("a_t", st), P("r_t", st)], axis=0)) for st in streams]
    bk = [jnp.concatenate([bd(P("b_t", st)), bd(P("k_t", st))], axis=0) for st in streams]
    pm = [_dot_nt(ar[i], bk[i]) for i in range(len(streams))]
    a_ab = [jnp.where(strict_p, p[0:L, 0:GW], 0.0) for p in pm]
    a_ak = [jnp.where(strict_p, p[0:L, GW:2 * GW], 0.0) for p in pm]
    a_rb = [jnp.where(incl_p, p[L:2 * L, 0:GW], 0.0) for p in pm]
    a_rk = [jnp.where(incl_p, p[L:2 * L, GW:2 * GW], 0.0) for p in pm]
    vbd = [bd(P("v", st)) for st in streams]
    ars = [_dot_nt(ar[i], s0b[i]) for i in range(len(streams))]
    akv = [_dot(_bf(jnp.concatenate([a_ak[i], a_rk[i]], axis=0)), vbd[i])
           for i in range(len(streams))]

    pw = [_dot(_bf(x), bd(x)) for x in a_ab]
    tinv = [eye_p + x for x in a_ab]
    n = 2
    while n < L:
        last = 2 * n >= L
        pwb = [bd(x) for x in pw]
        if last:
            tinv = [tinv[i] + _dot(_bf(tinv[i]), pwb[i]) for i in range(len(streams))]
        else:
            both = [_dot(_bf(jnp.concatenate([pw[i], tinv[i]], axis=0)), pwb[i])
                    for i in range(len(streams))]
            pw = [x[0:L] for x in both]
            tinv = [tinv[i] + both[i][L:2 * L] for i in range(len(streams))]
        n *= 2

    u = [_dot(_bf(tinv[i]), bd(ars[i][0:L] + akv[i][0:L])) for i in range(len(streams))]
    ys = [ars[i][L:2 * L] + akv[i][L:2 * L] + _dot(_bf(a_rb[i]), bd(u[i]))
          for i in range(len(streams))]
    for i, st in enumerate(streams):
        w_l = P("w_l", st)
        uv = _bf(jnp.concatenate([u[i], P("v", st)], axis=0))
        bkh = _bf(jnp.concatenate([P("b_t", st) * w_l, P("k_t", st) * w_l], axis=0))
        state_scr[sidx(st)] = s0[i] * w_l + jnp.where(st_mask, _dot_tn(uv, bkh), 0.0)

    inv_n = 1.0 / HEAD
    y = [jnp.concatenate([ys[bb * n_groups + g] for g in range(n_groups)], axis=1) for bb in range(NB)]
    mean = [m * inv_n for m in _head_sums(y, g128, (True,) * NB)]
    yc = [y[bb] - mean[bb] for bb in range(NB)]
    var = [s * inv_n for s in _head_sums([x * x for x in yc], g128, (False,) * NB)]
    for bb in range(NB):
        yn = yc[bb] * lax.rsqrt(var[bb] + LNX_EPS) * prm(_P_LW) + prm(_P_LB)
        gate = pre[bb]["gate"]
        o_ref[bb] = _bf((yn + pre[bb]["bonus"]) * (gate * _sigmoid(gate)))


def _rwkv(u3, pvec, mu_wa, lora, g128, *, col_r, col_wa, d_a):
    B, S, _ = u3.shape
    L = CHUNK
    nb = RWKV_NB
    assert L == HEAD and d_a % GW == 0 and B % nb == 0 and S % L == 0
    cb = col_r // d_a
    blk = lambda off: pl.BlockSpec((nb, L, d_a), lambda b, c, off=off: (b, c, cb + off))
    full = lambda arr: pl.BlockSpec(arr.shape, lambda b, c: (0,) * arr.ndim)
    return pl.pallas_call(
        _rwkv_kernel,
        grid=(B // nb, S // L),
        in_specs=[
            blk(0), blk(1), blk(2), blk(3),
            pl.BlockSpec((nb, L, LANES), lambda b, c: (b, c, col_wa // LANES)),
            full(pvec), full(mu_wa), full(lora), full(g128),
        ],
        out_specs=pl.BlockSpec((nb, L, d_a), lambda b, c: (b, c, 0)),
        out_shape=jax.ShapeDtypeStruct((B, S, d_a), BF16),
        scratch_shapes=[
            pltpu.VMEM((nb * (d_a // GW), GW, GW), F32),
            pltpu.VMEM((nb, 8, d_a), F32),
        ],
        compiler_params=pltpu.CompilerParams(
            dimension_semantics=("parallel", "arbitrary"), vmem_limit_bytes=VMEM_LIMIT),
        name="rwkv7",
    )(u3, u3, u3, u3, u3, pvec, mu_wa, lora, g128)


def _fox_kernel(q_ref, k_ref, v_ref, g_ref, aq_ref, ak_ref, qg_ref, kg_ref, gm_ref,
                o_ref, kp_scr, vt_scr, sta_scr, stb_scr, m_scr, acc_scr, *, tq, tk):
    p = pl.program_id(1)
    qi = pl.program_id(2)
    S = k_ref.shape[1]
    nh = q_ref.shape[2] // HEAD
    g128 = gm_ref[...]
    lane = lax.broadcasted_iota(jnp.int32, (1, LANES), 1)
    feat = lane < HEAD

    def minus_lanes(hh):
        lo = AUG0 + 6 * (nh * p + hh)
        return (lane >= lo) & (lane < lo + 3)

    def plus_lanes(hh):
        lo = AUG0 + 6 * (nh * p + hh) + 3
        return (lane >= lo) & (lane < lo + 3)

    def head_tile(x, hh):
        t = x[:, (hh // 2) * LANES:(hh // 2 + 1) * LANES]
        return pltpu.roll(t, HEAD, axis=1) if hh % 2 else t

    @pl.when(qi == 0)
    def _():
        for i in range(S // tk):
            rows = slice(i * tk, (i + 1) * tk)
            kb = k_ref[0, rows, :].astype(F32)
            (ssq,) = _head_sums([kb * kb], g128, (False,))
            kn = kb * lax.rsqrt(ssq * (1.0 / HEAD) + RMS_EPS) * kg_ref[...]
            aug = ak_ref[0, rows, :].astype(F32)
            for hh in range(nh):
                kp = jnp.where(feat, head_tile(kn, hh), jnp.where(plus_lanes(hh), 1.0, aug))
                kp_scr[hh, rows, :] = _bf(kp)
            vt = _bf(jnp.transpose(v_ref[0, rows, :].astype(F32)))
            for hh in range(nh):
                vt_scr[hh * VROWS:hh * VROWS + HEAD, rows] = vt[hh * HEAD:(hh + 1) * HEAD, :]
                vt_scr[hh * VROWS + HEAD:(hh + 1) * VROWS, rows] = jnp.ones((VROWS - HEAD, tk), BF16)

    q = q_ref[0].astype(F32)
    (ssq,) = _head_sums([q * q], g128, (False,))
    qn = q * lax.rsqrt(ssq * (1.0 / HEAD) + RMS_EPS) * (qg_ref[...] * (HEAD ** -0.5 * LOG2E))
    augq = aq_ref[0].astype(F32)
    qp = [_bf(jnp.where(feat, head_tile(qn, hh),
                        jnp.where(minus_lanes(hh), 1.0, jnp.where(plus_lanes(hh), augq, 0.0))))
          for hh in range(nh)]

    def scores(j, st_ref):
        ks = pl.multiple_of(j * tk, tk)
        for hh in range(nh):
            st_ref[hh] = _dot_nt(kp_scr[hh, pl.ds(ks, tk), :], qp[hh])

    def softmax_pv(j, st_ref, masked):
        ks = pl.multiple_of(j * tk, tk)
        st = [st_ref[hh] for hh in range(nh)]
        if masked:
            keyi = j * tk + lax.broadcasted_iota(jnp.int32, (tk, tq), 0)
            qryi = qi * tq + lax.broadcasted_iota(jnp.int32, (tk, tq), 1)
            st = [jnp.where(qryi >= keyi, x, -1e30) for x in st]
        m_old = [m_scr[hh, 0:1, :] for hh in range(nh)]
        m_new = [jnp.maximum(m_old[hh], jnp.max(st[hh], axis=0, keepdims=True)) for hh in range(nh)]
        alpha = [jnp.exp2(m_old[hh] - m_new[hh]) for hh in range(nh)]
        pt = [jnp.exp2(_bf(st[hh] - m_new[hh])) for hh in range(nh)]
        pv = [_dot(vt_scr[hh * VROWS:(hh + 1) * VROWS, pl.ds(ks, tk)], pt[hh])
              for hh in range(nh)]
        for hh in range(nh):
            m_scr[hh, 0:1, :] = m_new[hh]
            acc_scr[hh] = alpha[hh] * acc_scr[hh] + pv[hh]

    m_scr[...] = jnp.full(m_scr.shape, -1e30, F32)
    acc_scr[...] = jnp.zeros(acc_scr.shape, F32)
    scores(0, sta_scr)
    n_pairs = qi // 2

    def body(t, _):
        scores(2 * t + 1, stb_scr)
        softmax_pv(2 * t, sta_scr, False)
        scores(2 * t + 2, sta_scr)
        softmax_pv(2 * t + 1, stb_scr, False)
        return 0

    lax.fori_loop(0, n_pairs, body, 0)

    @pl.when(qi % 2 == 1)
    def _():
        scores(qi, stb_scr)
        softmax_pv(qi - 1, sta_scr, False)
        softmax_pv(qi, stb_scr, True)

    @pl.when(qi % 2 == 0)
    def _():
        softmax_pv(qi, sta_scr, True)

    ot = jnp.concatenate([acc_scr[hh, 0:HEAD, :] * (1.0 / acc_scr[hh, HEAD:HEAD + 1, :])
                          for hh in range(nh)], axis=0)
    g = g_ref[0].astype(F32)
    o_ref[0] = _bf(jnp.transpose(ot) * (g * _sigmoid(g)))


def _fox(u3, aug, qg2, kg2, g128, *, col_q, d_b, tq, tk):
    B, S, _ = u3.shape
    fw = qg2.shape[1]
    assert S % tq == 0 and tq == tk and d_b % fw == 0
    cq, ck, cv, cg = ((col_q + i * d_b) // fw for i in range(4))
    return pl.pallas_call(
        functools.partial(_fox_kernel, tq=tq, tk=tk),
        grid=(B, d_b // fw, S // tq),
        in_specs=[
            pl.BlockSpec((1, tq, fw), lambda b, p, i: (b, i, cq + p)),
            pl.BlockSpec((1, S, fw), lambda b, p, i: (b, 0, ck + p)),
            pl.BlockSpec((1, S, fw), lambda b, p, i: (b, 0, cv + p)),
            pl.BlockSpec((1, tq, fw), lambda b, p, i: (b, i, cg + p)),
            pl.BlockSpec((1, tq, LANES), lambda b, p, i: (b, i, 0)),
            pl.BlockSpec((1, S, LANES), lambda b, p, i: (b, 0, 0)),
            pl.BlockSpec((1, fw), lambda b, p, i: (0, 0)),
            pl.BlockSpec((1, fw), lambda b, p, i: (0, 0)),
            pl.BlockSpec((LANES, LANES), lambda b, p, i: (0, 0)),
        ],
        out_specs=pl.BlockSpec((1, tq, fw), lambda b, p, i: (b, i, p)),
        out_shape=jax.ShapeDtypeStruct((B, S, d_b), BF16),
        scratch_shapes=[
            pltpu.VMEM((fw // HEAD, S, LANES), BF16),
            pltpu.VMEM((fw // HEAD * VROWS, S), BF16),
            pltpu.VMEM((fw // HEAD, tk, tq), F32),
            pltpu.VMEM((fw // HEAD, tk, tq), F32),
            pltpu.VMEM((fw // HEAD, 8, tq), F32),
            pltpu.VMEM((fw // HEAD, VROWS, tq), F32),
        ],
        compiler_params=pltpu.CompilerParams(
            dimension_semantics=("parallel", "parallel", "arbitrary"), vmem_limit_bytes=VMEM_LIMIT),
        name="fox",
    )(u3, u3, u3, u3, aug, aug, qg2, kg2, g128)


def _out_kernel(x_ref, ya_ref, yb_ref, ga_ref, gb_ref, woa_ref, wob_ref, wo_ref, fg_ref, o_ref,
                woa_s, wob_s, wo_s):
    @pl.when(pl.program_id(0) == 0)
    def _():
        woa_s[...] = _bf(woa_ref[...])
        wob_s[...] = _bf(wob_ref[...])
        wo_s[...] = _bf(wo_ref[...])

    za = _dot(ya_ref[...], woa_s[...])
    zb = _dot(yb_ref[...], wob_s[...])
    merged = (_sigmoid(ga_ref[...].astype(F32)) * za + _sigmoid(gb_ref[...].astype(F32)) * zb)
    o = x_ref[...] + _dot(_bf(merged), wo_s[...])
    ms = jnp.mean(o * o, axis=-1, keepdims=True)
    o_ref[...] = o * lax.rsqrt(ms + RMS_EPS) * fg_ref[...]


def _out(x2, ya2, yb2, u2, woa, wob, wo, fg, *, tm):
    T, D = x2.shape
    assert T % tm == 0
    full = lambda arr: pl.BlockSpec(arr.shape, lambda i: (0,) * arr.ndim)
    return pl.pallas_call(
        _out_kernel,
        grid=(T // tm,),
        in_specs=[
            pl.BlockSpec((tm, D), lambda i: (i, 0)),
            pl.BlockSpec((tm, ya2.shape[1]), lambda i: (i, 0)),
            pl.BlockSpec((tm, yb2.shape[1]), lambda i: (i, 0)),
            pl.BlockSpec((tm, D), lambda i: (i, 0)),
            pl.BlockSpec((tm, D), lambda i: (i, 1)),
            full(woa), full(wob), full(wo), full(fg),
        ],
        out_specs=pl.BlockSpec((tm, D), lambda i: (i, 0)),
        out_shape=jax.ShapeDtypeStruct((T, D), F32),
        scratch_shapes=[pltpu.VMEM(w.shape, BF16) for w in (woa, wob, wo)],
        compiler_params=pltpu.CompilerParams(
            dimension_semantics=("arbitrary",), vmem_limit_bytes=VMEM_LIMIT),
        name="outstage",
    )(x2, ya2, yb2, u2, u2, woa, wob, wo, fg)


def _block_ones(width):
    i = jnp.arange(width) // HEAD
    return (i[:, None] == i[None, :]).astype(BF16)


def _layer(x2, B, S, norm_g, w_in, shift_mu, w_lora_up, w0, a_lora_up, a0, k_k, k_a, r_k,
           lnx_w, lnx_b, f_bias, q_norm_g, k_norm_g, w_out_a, w_out_b, w_out, out_gain):
    T, D = x2.shape
    d_a = w0.shape[0]
    d_b = w_out_b.shape[0]
    rank = w_lora_up.shape[0]
    h_b = f_bias.shape[0]
    rw = 4 * d_a + 2 * rank
    fx = 4 * d_b + h_b

    segs = [(rw + fx, 2 * D), (0, 3 * d_a), (3 * d_a + 2 * rank, d_a), (rw, 4 * d_b),
            (3 * d_a, 2 * rank), (rw + 4 * d_b, h_b)]
    w_perm = _permute_cols(w_in, segs, pad_to=LANES, tr=256)
    col_r = 2 * D
    col_q = col_r + 4 * d_a
    col_wa = col_q + 4 * d_b

    u2, uf = _inproj(x2, norm_g.reshape(1, D), w_perm, tm=2048, tn=1280)
    u3 = u2.reshape(B, S, u2.shape[1])

    mu = shift_mu
    rows = [mu[:d_a], mu[d_a:2 * d_a], mu[2 * d_a:3 * d_a], mu[3 * d_a + 2 * rank:],
            w0, a0, k_k, k_a, r_k.reshape(-1), lnx_w, lnx_b]
    pvec = jnp.stack(rows + [jnp.zeros_like(w0)] * (16 - len(rows)), axis=0)
    mu_wa = mu[3 * d_a:3 * d_a + 2 * rank].reshape(1, 2 * rank)
    z = jnp.zeros((rank, d_a), F32)
    lora = _bf(jnp.concatenate(
        [jnp.concatenate([w_lora_up, z], axis=1), jnp.concatenate([z, a_lora_up], axis=1)], axis=0))
    ya = _rwkv(u3, pvec, mu_wa, lora, _block_ones(LANES), col_r=col_r, col_wa=col_wa, d_a=d_a)

    fb_pad = jnp.pad(f_bias, (0, LANES - h_b)).reshape(1, LANES)
    aug = _fprep(uf.reshape(B, S, LANES), fb_pad, _aug_selector(h_b))
    qg2 = jnp.tile(q_norm_g, FOX_HEADS).reshape(1, FOX_HEADS * HEAD)
    kg2 = jnp.tile(k_norm_g, FOX_HEADS).reshape(1, FOX_HEADS * HEAD)
    yb = _fox(u3, aug, qg2, kg2, _block_ones(LANES), col_q=col_q, d_b=d_b, tq=512, tk=512)

    return _out(x2, ya.reshape(T, d_a), yb.reshape(T, d_b), u2,
                w_out_a, w_out_b, w_out, out_gain.reshape(1, D), tm=512)


def kernel(x, norm_g, w_in, shift_mu, w_lora_up, w0, a_lora_up, a0, k_k, k_a, r_k, lnx_w, lnx_b,
           f_bias, q_norm_g, k_norm_g, w_out_a, w_out_b, w_out, final_norm_g):
    B, S, D = x.shape
    depth = w_in.shape[0]
    assert depth == 1, "the fused output stage applies the final norm after the single layer"
    x2 = x.reshape(B * S, D)
    out = _layer(x2, B, S, norm_g[0], w_in[0], shift_mu[0], w_lora_up[0], w0[0], a_lora_up[0],
                 a0[0], k_k[0], k_a[0], r_k[0], lnx_w[0], lnx_b[0], f_bias[0], q_norm_g[0],
                 k_norm_g[0], w_out_a[0], w_out_b[0], w_out[0], final_norm_g)
    return out.reshape(B, S, D)
```

```python
import functools

import jax
import jax.numpy as jnp
from jax import lax
from jax.experimental import pallas as pl
from jax.experimental.pallas import tpu as pltpu

F32 = jnp.float32
BF16 = jnp.bfloat16

HEAD = 64
LANES = 128
RMS_EPS = 1e-6
LNX_EPS = 64e-5
CHUNK = 64
GROUP = 4
GW = GROUP * HEAD
RWKV_NB = 8
FOX_HEADS = 4
AUG0 = HEAD
VROWS = HEAD + 16
LOG2E = 1.4426950408889634
DECAY_SCALE = -0.6065306597126334
VMEM_LIMIT = 56 * 1024 * 1024


def _bf(x):
    return x.astype(BF16)


def _dot(a, b):
    return jnp.dot(a, b, preferred_element_type=F32)


def _dot_nt(a, b):
    return lax.dot_general(a, b, (((1,), (1,)), ((), ())), preferred_element_type=F32)


def _dot_tn(a, b):
    return lax.dot_general(a, b, (((0,), (0,)), ((), ())), preferred_element_type=F32)


def _split2(x):
    hi = _bf(x)
    lo = _bf(x - hi.astype(F32))
    return hi, lo


def _split3(x):
    hi = _bf(x)
    r1 = x - hi.astype(F32)
    mid = _bf(r1)
    lo = _bf(r1 - mid.astype(F32))
    return hi, mid, lo


def _head_sums(xs, g128, two_pass):
    m, w = xs[0].shape
    nt = w // LANES
    parts = []
    for x, tp in zip(xs, two_pass):
        for piece in (_split2(x) if tp else (_bf(x),)):
            parts += [piece[:, t * LANES:(t + 1) * LANES] for t in range(nt)]
    r = _dot(jnp.concatenate(parts, axis=0), g128)
    tile = lambda i: r[i * m:(i + 1) * m]
    outs, base = [], 0
    for tp in two_pass:
        if tp:
            cols = [tile(base + t) + tile(base + nt + t) for t in range(nt)]
        else:
            cols = [tile(base + t) for t in range(nt)]
        outs.append(jnp.concatenate(cols, axis=1))
        base += (2 if tp else 1) * nt
    return outs


def _sigmoid(x):
    return 1.0 / (1.0 + jnp.exp(-x))


def _regroup_rows_kernel(w_ref, o_ref, *, segs):
    total = sum(width for _, width in segs)
    if total < o_ref.shape[0]:
        t0 = total // 16 * 16
        o_ref[t0:, :] = jnp.zeros((o_ref.shape[0] - t0, o_ref.shape[1]), BF16)
    dst = 0
    for src, width in segs:
        o_ref[dst:dst + width, :] = _bf(w_ref[src:src + width, :])
        dst += width


def _regroup_rows(wt, segs, *, pad_to, tc):
    rows, cols = wt.shape
    n_out = -(-sum(width for _, width in segs) // pad_to) * pad_to
    assert cols % tc == 0 and all(src % 8 == 0 for src, _ in segs)
    return pl.pallas_call(
        functools.partial(_regroup_rows_kernel, segs=tuple(segs)),
        grid=(cols // tc,),
        in_specs=[pl.BlockSpec((rows, tc), lambda i: (0, i))],
        out_specs=pl.BlockSpec((n_out, tc), lambda i: (0, i)),
        out_shape=jax.ShapeDtypeStruct((n_out, cols), BF16),
        compiler_params=pltpu.CompilerParams(
            dimension_semantics=("parallel",), vmem_limit_bytes=VMEM_LIMIT),
        name="wperm",
    )(wt)


def _inproj_kernel(x_ref, g_ref, w_ref, o_ref, of_ref, h_scr, *, n_col_tiles):
    j = pl.program_id(1)

    @pl.when(j == 0)
    def _():
        x = x_ref[...]
        ms = jnp.mean(x * x, axis=-1, keepdims=True)
        h_scr[...] = _bf(x * lax.rsqrt(ms + RMS_EPS) * g_ref[...])

    acc = _dot_nt(h_scr[...], w_ref[...])
    o_ref[...] = _bf(acc)

    @pl.when(j == n_col_tiles - 1)
    def _():
        of_ref[...] = acc[:, acc.shape[1] - LANES:]


def _inproj(x2, norm_g, w_perm_t, *, tm, tn):
    T, D = x2.shape
    N = w_perm_t.shape[0]
    assert T % tm == 0 and N % tn == 0
    nj = N // tn
    return pl.pallas_call(
        functools.partial(_inproj_kernel, n_col_tiles=nj),
        grid=(T // tm, nj),
        in_specs=[
            pl.BlockSpec((tm, D), lambda i, j: (i, 0)),
            pl.BlockSpec((1, D), lambda i, j: (0, 0)),
            pl.BlockSpec((tn, D), lambda i, j: (j, 0)),
        ],
        out_specs=[
            pl.BlockSpec((tm, tn), lambda i, j: (i, j)),
            pl.BlockSpec((tm, LANES), lambda i, j: (i, 0)),
        ],
        out_shape=[
            jax.ShapeDtypeStruct((T, N), BF16),
            jax.ShapeDtypeStruct((T, LANES), F32),
        ],
        scratch_shapes=[pltpu.VMEM((tm, D), BF16)],
        compiler_params=pltpu.CompilerParams(
            dimension_semantics=("parallel", "arbitrary"), vmem_limit_bytes=VMEM_LIMIT),
        name="inproj",
    )(x2, norm_g, w_perm_t)


def _fprep_kernel(f_ref, fb_ref, sel_ref, aug_ref, *, blk):
    S = f_ref.shape[1]
    row = lax.broadcasted_iota(jnp.int32, (blk, blk), 0)
    col = lax.broadcasted_iota(jnp.int32, (blk, blk), 1)
    tril = _bf(jnp.where(row >= col, 1.0, 0.0))
    carry = jnp.zeros((1, LANES), F32)
    for i in range(S // blk):
        z = f_ref[0, i * blk:(i + 1) * blk, :] + fb_ref[...]
        lf = jnp.minimum(z, 0.0) - jnp.log1p(jnp.exp(-jnp.abs(z)))
        hi, mid, lo = _split3(lf)
        c = (_dot(tril, hi) + _dot(tril, mid) + _dot(tril, lo)) + carry
        pieces = jnp.concatenate(_split3(c * LOG2E), axis=1)
        aug_ref[0, i * blk:(i + 1) * blk, :] = _bf(_dot(pieces, sel_ref[...]))
        carry = c[blk - 1:blk, :]


def _fprep(uf3, fb_pad, sel):
    B, S, _ = uf3.shape
    return pl.pallas_call(
        functools.partial(_fprep_kernel, blk=256),
        grid=(B,),
        in_specs=[
            pl.BlockSpec((1, S, LANES), lambda b: (b, 0, 0)),
            pl.BlockSpec((1, LANES), lambda b: (0, 0)),
            pl.BlockSpec(sel.shape, lambda b: (0, 0)),
        ],
        out_specs=pl.BlockSpec((1, S, LANES), lambda b: (b, 0, 0)),
        out_shape=jax.ShapeDtypeStruct((B, S, LANES), BF16),
        compiler_params=pltpu.CompilerParams(dimension_semantics=("parallel",)),
        name="fprep",
    )(uf3, fb_pad, sel)


def _aug_selector(n_heads):
    assert AUG0 + 6 * n_heads <= LANES
    r = jnp.arange(LANES)[:, None]
    c = jnp.arange(LANES)[None, :]
    blocks = []
    for i in range(3):
        m = jnp.where(c == AUG0 + 6 * r + i, -1.0, jnp.where(c == AUG0 + 6 * r + 3 + i, 1.0, 0.0))
        blocks.append(jnp.where(r < n_heads, m, 0.0))
    return _bf(jnp.concatenate(blocks, axis=0))


_P_MU_R, _P_MU_K, _P_MU_V, _P_MU_G, _P_W0, _P_A0, _P_KK, _P_KA, _P_RK, _P_LW, _P_LB = range(11)


def _shift_mix(x, carry_row, mu):
    rolled = pltpu.roll(x, shift=1, axis=0)
    row = lax.broadcasted_iota(jnp.int32, x.shape, 0)
    prev = jnp.where(row == 0, carry_row, rolled)
    return x + (prev - x) * mu


def _block_diag(x, half_masks):
    xb = _bf(x)
    n_tiles = xb.shape[1] // LANES
    zero = jnp.zeros((xb.shape[0], LANES), BF16)
    rows = []
    for h in range(xb.shape[1] // HEAD):
        t = h // 2
        piece = xb[:, t * LANES:(t + 1) * LANES] * half_masks[h % 2]
        rows.append(jnp.concatenate([piece if i == t else zero for i in range(n_tiles)], axis=1))
    return jnp.concatenate(rows, axis=0)


def _rwkv_kernel(r_ref, k_ref, v_ref, g_ref, wa_ref, pv_ref, muwa_ref, lora_ref, gm_ref,
                 o_ref, state_scr, carry_scr):
    c = pl.program_id(1)
    NB = r_ref.shape[0]
    L = r_ref.shape[1]
    DA = r_ref.shape[2]
    n_groups = DA // GW

    @pl.when(c == 0)
    def _():
        state_scr[...] = jnp.zeros_like(state_scr)
        carry_scr[...] = jnp.zeros_like(carry_scr)

    def prm(i):
        return pv_ref[i:i + 1, :]

    g128 = gm_ref[...]
    row = lax.broadcasted_iota(jnp.int32, (L, L), 0)
    col = lax.broadcasted_iota(jnp.int32, (L, L), 1)
    tril = _bf(jnp.where(row >= col, 1.0, 0.0))
    lane128 = lax.broadcasted_iota(jnp.int32, (1, LANES), 1)
    half_masks = [_bf(jnp.where(lane128 // HEAD == i, 1.0, 0.0)) for i in range(2)]
    prow = lax.broadcasted_iota(jnp.int32, (L, GW), 0)
    pcol = lax.broadcasted_iota(jnp.int32, (L, GW), 1) & (HEAD - 1)
    incl_p = prow >= pcol
    strict_p = prow > pcol
    eye_p = jnp.where(prow == pcol, 1.0, 0.0)
    st_mask = (lax.broadcasted_iota(jnp.int32, (GW, GW), 0) // HEAD
               == lax.broadcasted_iota(jnp.int32, (GW, GW), 1) // HEAD)
    bd = lambda x: _block_diag(x, half_masks)

    pre = []
    for bb in range(NB):
        raw = [ref[bb].astype(F32) for ref in (r_ref, k_ref, v_ref, g_ref, wa_ref)]
        mus = [prm(_P_MU_R), prm(_P_MU_K), prm(_P_MU_V), prm(_P_MU_G), muwa_ref[...]]
        mixed = []
        for i, x in enumerate(raw):
            wdt = x.shape[1]
            mixed.append(_shift_mix(x, carry_scr[bb, i:i + 1, 0:wdt], mus[i]))
            carry_scr[bb, i:i + 1, 0:wdt] = x[L - 1:L, :]
        r, k, v, gate, wa = mixed

        wa_act = jnp.where(lane128 < HEAD, jnp.tanh(wa), wa)
        lo = _dot(_bf(wa_act), lora_ref[...])
        ld = DECAY_SCALE * _sigmoid(prm(_P_W0) + lo[:, 0:DA])
        a = _sigmoid(prm(_P_A0) + lo[:, DA:2 * DA])

        kk = k * prm(_P_KK)
        k2 = k * (1.0 + (a - 1.0) * prm(_P_KA))
        ssq, bon = _head_sums([kk * kk, r * k2 * prm(_P_RK)], g128, (False, False))
        kk = kk * jnp.minimum(lax.rsqrt(ssq), 1e12)

        h3, m3, l3 = _split3(ld)
        cum = _dot(tril, h3) + _dot(tril, m3) + _dot(tril, l3)
        e_neg = jnp.exp(-cum)
        w_l = jnp.exp(cum[L - 1:L, :])
        a_t = -kk * jnp.exp(cum - ld)
        r_t = r * jnp.exp(cum)
        b_t = (kk * a) * e_neg
        k_t = k2 * e_neg
        pre.append(dict(v=v, gate=gate, bonus=bon * v, w_l=w_l, a_t=a_t, r_t=r_t, b_t=b_t, k_t=k_t))

    streams = [(bb, g) for bb in range(NB) for g in range(n_groups)]
    gsl = lambda g: slice(g * GW, (g + 1) * GW)
    P = lambda name, st: pre[st[0]][name][:, gsl(st[1])]
    sidx = lambda st: st[0] * n_groups + st[1]

    s0 = [state_scr[sidx(st)] for st in streams]
    s0b = [_bf(x) for x in s0]
    ar = [_bf(jnp.concatenate([P("a_t", st), P("r_t", st)], axis=0)) for st in streams]
    bk = [jnp.concatenate([bd(P("b_t", st)), bd(P("k_t", st))], axis=0) for st in streams]
    pm = [_dot_nt(ar[i], bk[i]) for i in range(len(streams))]
    a_ab = [jnp.where(strict_p, p[0:L, 0:GW], 0.0) for p in pm]
    a_ak = [jnp.where(strict_p, p[0:L, GW:2 * GW], 0.0) for p in pm]
    a_rb = [jnp.where(incl_p, p[L:2 * L, 0:GW], 0.0) for p in pm]
    a_rk = [jnp.where(incl_p, p[L:2 * L, GW:2 * GW], 0.0) for p in pm]
    vbd = [bd(P("v", st)) for st in streams]
    ars = [_dot_nt(ar[i], s0b[i]) for i in range(len(streams))]
    akv = [_dot(_bf(jnp.concatenate([a_ak[i], a_rk[i]], axis=0)), vbd[i])
           for i in range(len(streams))]

    pw = [_dot(_bf(x), bd(x)) for x in a_ab]
    tinv = [eye_p + x for x in a_ab]
    n = 2
    while n < L:
        last = 2 * n >= L
        pwb = [bd(x) for x in pw]
        if last:
            tinv = [tinv[i] + _dot(_bf(tinv[i]), pwb[i]) for i in range(len(streams))]
        else:
            both = [_dot(_bf(jnp.concatenate([pw[i], tinv[i]], axis=0)), pwb[i])
                    for i in range(len(streams))]
            pw = [x[0:L] for x in both]
            tinv = [tinv[i] + both[i][L:2 * L] for i in range(len(streams))]
        n *= 2

    u = [_dot(_bf(tinv[i]), bd(ars[i][0:L] + akv[i][0:L])) for i in range(len(streams))]
    ys = [ars[i][L:2 * L] + akv[i][L:2 * L] + _dot(_bf(a_rb[i]), bd(u[i]))
          for i in range(len(streams))]
    for i, st in enumerate(streams):
        w_l = P("w_l", st)
        uv = _bf(jnp.concatenate([u[i], P("v", st)], axis=0))
        bkh = _bf(jnp.concatenate([P("b_t", st) * w_l, P("k_t", st) * w_l], axis=0))
        state_scr[sidx(st)] = s0[i] * w_l + jnp.where(st_mask, _dot_tn(uv, bkh), 0.0)

    inv_n = 1.0 / HEAD
    y = [jnp.concatenate([ys[bb * n_groups + g] for g in range(n_groups)], axis=1) for bb in range(NB)]
    mean = [m * inv_n for m in _head_sums(y, g128, (True,) * NB)]
    yc = [y[bb] - mean[bb] for bb in range(NB)]
    var = [s * inv_n for s in _head_sums([x * x for x in yc], g128, (False,) * NB)]
    for bb in range(NB):
        yn = yc[bb] * lax.rsqrt(var[bb] + LNX_EPS) * prm(_P_LW) + prm(_P_LB)
        gate = pre[bb]["gate"]
        o_ref[bb] = _bf((yn + pre[bb]["bonus"]) * (gate * _sigmoid(gate)))


def _rwkv(u3, pvec, mu_wa, lora, g128, *, col_r, col_wa, d_a):
    B, S, _ = u3.shape
    L = CHUNK
    nb = RWKV_NB
    assert L == HEAD and d_a % GW == 0 and B % nb == 0 and S % L == 0
    cb = col_r // d_a
    blk = lambda off: pl.BlockSpec((nb, L, d_a), lambda b, c, off=off: (b, c, cb + off))
    full = lambda arr: pl.BlockSpec(arr.shape, lambda b, c: (0,) * arr.ndim)
    return pl.pallas_call(
        _rwkv_kernel,
        grid=(B // nb, S // L),
        in_specs=[
            blk(0), blk(1), blk(2), blk(3),
            pl.BlockSpec((nb, L, LANES), lambda b, c: (b, c, col_wa // LANES)),
            full(pvec), full(mu_wa), full(lora), full(g128),
        ],
        out_specs=pl.BlockSpec((nb, L, d_a), lambda b, c: (b, c, 0)),
        out_shape=jax.ShapeDtypeStruct((B, S, d_a), BF16),
        scratch_shapes=[
            pltpu.VMEM((nb * (d_a // GW), GW, GW), F32),
            pltpu.VMEM((nb, 8, d_a), F32),
        ],
        compiler_params=pltpu.CompilerParams(
            dimension_semantics=("parallel", "arbitrary"), vmem_limit_bytes=VMEM_LIMIT),
        name="rwkv7",
    )(u3, u3, u3, u3, u3, pvec, mu_wa, lora, g128)


def _fox_kernel(q_ref, k_ref, v_ref, g_ref, aq_ref, ak_ref, qg_ref, kg_ref, gm_ref,
                o_ref, kp_scr, vt_scr, sta_scr, stb_scr, m_scr, acc_scr, *, tq, tk):
    p = pl.program_id(1)
    qi = pl.program_id(2)
    S = k_ref.shape[1]
    nh = q_ref.shape[2] // HEAD
    g128 = gm_ref[...]
    lane = lax.broadcasted_iota(jnp.int32, (1, LANES), 1)
    feat = lane < HEAD

    def minus_lanes(hh):
        lo = AUG0 + 6 * (nh * p + hh)
        return (lane >= lo) & (lane < lo + 3)

    def plus_lanes(hh):
        lo = AUG0 + 6 * (nh * p + hh) + 3
        return (lane >= lo) & (lane < lo + 3)

    def head_tile(x, hh):
        t = x[:, (hh // 2) * LANES:(hh // 2 + 1) * LANES]
        return pltpu.roll(t, HEAD, axis=1) if hh % 2 else t

    @pl.when(qi == 0)
    def _():
        for i in range(S // tk):
            rows = slice(i * tk, (i + 1) * tk)
            kb = k_ref[0, rows, :].astype(F32)
            (ssq,) = _head_sums([kb * kb], g128, (False,))
            kn = kb * lax.rsqrt(ssq * (1.0 / HEAD) + RMS_EPS) * kg_ref[...]
            aug = ak_ref[0, rows, :].astype(F32)
            for hh in range(nh):
                kp = jnp.where(feat, head_tile(kn, hh), jnp.where(plus_lanes(hh), 1.0, aug))
                kp_scr[hh, rows, :] = _bf(kp)
            vt = _bf(jnp.transpose(v_ref[0, rows, :].astype(F32)))
            for hh in range(nh):
                vt_scr[hh * VROWS:hh * VROWS + HEAD, rows] = vt[hh * HEAD:(hh + 1) * HEAD, :]
                vt_scr[hh * VROWS + HEAD:(hh + 1) * VROWS, rows] = jnp.ones((VROWS - HEAD, tk), BF16)

    q = q_ref[0].astype(F32)
    (ssq,) = _head_sums([q * q], g128, (False,))
    qn = q * lax.rsqrt(ssq * (1.0 / HEAD) + RMS_EPS) * (qg_ref[...] * (HEAD ** -0.5 * LOG2E))
    augq = aq_ref[0].astype(F32)
    qp = [_bf(jnp.where(feat, head_tile(qn, hh),
                        jnp.where(minus_lanes(hh), 1.0, jnp.where(plus_lanes(hh), augq, 0.0))))
          for hh in range(nh)]

    def scores(j, st_ref):
        ks = pl.multiple_of(j * tk, tk)
        for hh in range(nh):
            st_ref[hh] = _dot_nt(kp_scr[hh, pl.ds(ks, tk), :], qp[hh])

    def softmax_pv(j, st_ref, masked):
        ks = pl.multiple_of(j * tk, tk)
        st = [st_ref[hh] for hh in range(nh)]
        if masked:
            keyi = j * tk + lax.broadcasted_iota(jnp.int32, (tk, tq), 0)
            qryi = qi * tq + lax.broadcasted_iota(jnp.int32, (tk, tq), 1)
            st = [jnp.where(qryi >= keyi, x, -1e30) for x in st]
        m_old = [m_scr[hh, 0:1, :] for hh in range(nh)]
        m_new = [jnp.maximum(m_old[hh], jnp.max(st[hh], axis=0, keepdims=True)) for hh in range(nh)]
        alpha = [jnp.exp2(m_old[hh] - m_new[hh]) for hh in range(nh)]
        pt = [jnp.exp2(_bf(st[hh] - m_new[hh])) for hh in range(nh)]
        pv = [_dot(vt_scr[hh * VROWS:(hh + 1) * VROWS, pl.ds(ks, tk)], pt[hh])
              for hh in range(nh)]
        for hh in range(nh):
            m_scr[hh, 0:1, :] = m_new[hh]
            acc_scr[hh] = alpha[hh] * acc_scr[hh] + pv[hh]

    m_scr[...] = jnp.full(m_scr.shape, -1e30, F32)
    acc_scr[...] = jnp.zeros(acc_scr.shape, F32)
    scores(0, sta_scr)
    n_pairs = qi // 2

    def body(t, _):
        scores(2 * t + 1, stb_scr)
        softmax_pv(2 * t, sta_scr, False)
        scores(2 * t + 2, sta_scr)
        softmax_pv(2 * t + 1, stb_scr, False)
        return 0

    lax.fori_loop(0, n_pairs, body, 0)

    @pl.when(qi % 2 == 1)
    def _():
        scores(qi, stb_scr)
        softmax_pv(qi - 1, sta_scr, False)
        softmax_pv(qi, stb_scr, True)

    @pl.when(qi % 2 == 0)
    def _():
        softmax_pv(qi, sta_scr, True)

    ot = jnp.concatenate([acc_scr[hh, 0:HEAD, :] * (1.0 / acc_scr[hh, HEAD:HEAD + 1, :])
                          for hh in range(nh)], axis=0)
    g = g_ref[0].astype(F32)
    o_ref[0] = _bf(jnp.transpose(ot) * (g * _sigmoid(g)))


def _fox(u3, aug, qg2, kg2, g128, *, col_q, d_b, tq, tk):
    B, S, _ = u3.shape
    fw = qg2.shape[1]
    assert S % tq == 0 and tq == tk and d_b % fw == 0
    cq, ck, cv, cg = ((col_q + i * d_b) // fw for i in range(4))
    return pl.pallas_call(
        functools.partial(_fox_kernel, tq=tq, tk=tk),
        grid=(B, d_b // fw, S // tq),
        in_specs=[
            pl.BlockSpec((1, tq, fw), lambda b, p, i: (b, i, cq + p)),
            pl.BlockSpec((1, S, fw), lambda b, p, i: (b, 0, ck + p)),
            pl.BlockSpec((1, S, fw), lambda b, p, i: (b, 0, cv + p)),
            pl.BlockSpec((1, tq, fw), lambda b, p, i: (b, i, cg + p)),
            pl.BlockSpec((1, tq, LANES), lambda b, p, i: (b, i, 0)),
            pl.BlockSpec((1, S, LANES), lambda b, p, i: (b, 0, 0)),
            pl.BlockSpec((1, fw), lambda b, p, i: (0, 0)),
            pl.BlockSpec((1, fw), lambda b, p, i: (0, 0)),
            pl.BlockSpec((LANES, LANES), lambda b, p, i: (0, 0)),
        ],
        out_specs=pl.BlockSpec((1, tq, fw), lambda b, p, i: (b, i, p)),
        out_shape=jax.ShapeDtypeStruct((B, S, d_b), BF16),
        scratch_shapes=[
            pltpu.VMEM((fw // HEAD, S, LANES), BF16),
            pltpu.VMEM((fw // HEAD * VROWS, S), BF16),
            pltpu.VMEM((fw // HEAD, tk, tq), F32),
            pltpu.VMEM((fw // HEAD, tk, tq), F32),
            pltpu.VMEM((fw // HEAD, 8, tq), F32),
            pltpu.VMEM((fw // HEAD, VROWS, tq), F32),
        ],
        compiler_params=pltpu.CompilerParams(
            dimension_semantics=("parallel", "parallel", "arbitrary"), vmem_limit_bytes=VMEM_LIMIT),
        name="fox",
    )(u3, u3, u3, u3, aug, aug, qg2, kg2, g128)


def _out_kernel(x_ref, ya_ref, yb_ref, ga_ref, gb_ref, woa_ref, wob_ref, wo_ref, fg_ref, o_ref,
                woa_s, wob_s, wo_s):
    @pl.when(pl.program_id(0) == 0)
    def _():
        woa_s[...] = _bf(woa_ref[...])
        wob_s[...] = _bf(wob_ref[...])
        wo_s[...] = _bf(wo_ref[...])

    za = _dot(ya_ref[...], woa_s[...])
    zb = _dot(yb_ref[...], wob_s[...])
    merged = (_sigmoid(ga_ref[...].astype(F32)) * za + _sigmoid(gb_ref[...].astype(F32)) * zb)
    o = x_ref[...] + _dot(_bf(merged), wo_s[...])
    ms = jnp.mean(o * o, axis=-1, keepdims=True)
    o_ref[...] = o * lax.rsqrt(ms + RMS_EPS) * fg_ref[...]


def _out(x2, ya2, yb2, u2, woa, wob, wo, fg, *, tm):
    T, D = x2.shape
    assert T % tm == 0
    full = lambda arr: pl.BlockSpec(arr.shape, lambda i: (0,) * arr.ndim)
    return pl.pallas_call(
        _out_kernel,
        grid=(T // tm,),
        in_specs=[
            pl.BlockSpec((tm, D), lambda i: (i, 0)),
            pl.BlockSpec((tm, ya2.shape[1]), lambda i: (i, 0)),
            pl.BlockSpec((tm, yb2.shape[1]), lambda i: (i, 0)),
            pl.BlockSpec((tm, D), lambda i: (i, 0)),
            pl.BlockSpec((tm, D), lambda i: (i, 1)),
            full(woa), full(wob), full(wo), full(fg),
        ],
        out_specs=pl.BlockSpec((tm, D), lambda i: (i, 0)),
        out_shape=jax.ShapeDtypeStruct((T, D), F32),
        scratch_shapes=[pltpu.VMEM(w.shape, BF16) for w in (woa, wob, wo)],
        compiler_params=pltpu.CompilerParams(
            dimension_semantics=("arbitrary",), vmem_limit_bytes=VMEM_LIMIT),
        name="outstage",
    )(x2, ya2, yb2, u2, u2, woa, wob, wo, fg)


def _block_ones(width):
    i = jnp.arange(width) // HEAD
    return (i[:, None] == i[None, :]).astype(BF16)


def _layer(x2, B, S, norm_g, w_in, shift_mu, w_lora_up, w0, a_lora_up, a0, k_k, k_a, r_k,
           lnx_w, lnx_b, f_bias, q_norm_g, k_norm_g, w_out_a, w_out_b, w_out, out_gain):
    T, D = x2.shape
    d_a = w0.shape[0]
    d_b = w_out_b.shape[0]
    rank = w_lora_up.shape[0]
    h_b = f_bias.shape[0]
    rw = 4 * d_a + 2 * rank
    fx = 4 * d_b + h_b

    segs = [(rw + fx, 2 * D), (0, 3 * d_a), (3 * d_a + 2 * rank, d_a), (rw, 4 * d_b),
            (3 * d_a, 2 * rank), (rw + 4 * d_b, h_b)]
    w_perm_t = _regroup_rows(jnp.transpose(w_in), segs, pad_to=LANES, tc=LANES)
    col_r = 2 * D
    col_q = col_r + 4 * d_a
    col_wa = col_q + 4 * d_b

    u2, uf = _inproj(x2, norm_g.reshape(1, D), w_perm_t, tm=2048, tn=1280)
    u3 = u2.reshape(B, S, u2.shape[1])

    mu = shift_mu
    rows = [mu[:d_a], mu[d_a:2 * d_a], mu[2 * d_a:3 * d_a], mu[3 * d_a + 2 * rank:],
            w0, a0, k_k, k_a, r_k.reshape(-1), lnx_w, lnx_b]
    pvec = jnp.stack(rows + [jnp.zeros_like(w0)] * (16 - len(rows)), axis=0)
    mu_wa = mu[3 * d_a:3 * d_a + 2 * rank].reshape(1, 2 * rank)
    z = jnp.zeros((rank, d_a), F32)
    lora = _bf(jnp.concatenate(
        [jnp.concatenate([w_lora_up, z], axis=1), jnp.concatenate([z, a_lora_up], axis=1)], axis=0))
    ya = _rwkv(u3, pvec, mu_wa, lora, _block_ones(LANES), col_r=col_r, col_wa=col_wa, d_a=d_a)

    fb_pad = jnp.pad(f_bias, (0, LANES - h_b)).reshape(1, LANES)
    aug = _fprep(uf.reshape(B, S, LANES), fb_pad, _aug_selector(h_b))
    qg2 = jnp.tile(q_norm_g, FOX_HEADS).reshape(1, FOX_HEADS * HEAD)
    kg2 = jnp.tile(k_norm_g, FOX_HEADS).reshape(1, FOX_HEADS * HEAD)
    yb = _fox(u3, aug, qg2, kg2, _block_ones(LANES), col_q=col_q, d_b=d_b, tq=512, tk=512)

    return _out(x2, ya.reshape(T, d_a), yb.reshape(T, d_b), u2,
                w_out_a, w_out_b, w_out, out_gain.reshape(1, D), tm=512)


def kernel(x, norm_g, w_in, shift_mu, w_lora_up, w0, a_lora_up, a0, k_k, k_a, r_k, lnx_w, lnx_b,
           f_bias, q_norm_g, k_norm_g, w_out_a, w_out_b, w_out, final_norm_g):
    B, S, D = x.shape
    depth = w_in.shape[0]
    assert depth == 1, "the fused output stage applies the final norm after the single layer"
    x2 = x.reshape(B * S, D)
    out = _layer(x2, B, S, norm_g[0], w_in[0], shift_mu[0], w_lora_up[0], w0[0], a_lora_up[0],
                 a0[0], k_k[0], k_a[0], r_k[0], lnx_w[0], lnx_b[0], f_bias[0], q_norm_g[0],
                 k_norm_g[0], w_out_a[0], w_out_b[0], w_out[0], final_norm_g)
    return out.reshape(B, S, D)
```

```python
import functools

import jax
import jax.numpy as jnp
from jax import lax
from jax.experimental import pallas as pl
from jax.experimental.pallas import tpu as pltpu

F32 = jnp.float32
BF16 = jnp.bfloat16

HEAD = 64
LANES = 128
RMS_EPS = 1e-6
LNX_EPS = 64e-5
CHUNK = 64
GROUP = 4
GW = GROUP * HEAD
RWKV_NB = 8
FOX_HEADS = 4
AUG0 = HEAD
VROWS = HEAD + 16
LOG2E = 1.4426950408889634
DECAY_SCALE = -0.6065306597126334
VMEM_LIMIT = 56 * 1024 * 1024


def _bf(x):
    return x.astype(BF16)


def _dot(a, b):
    return jnp.dot(a, b, preferred_element_type=F32)


def _dot_nt(a, b):
    return lax.dot_general(a, b, (((1,), (1,)), ((), ())), preferred_element_type=F32)


def _dot_tn(a, b):
    return lax.dot_general(a, b, (((0,), (0,)), ((), ())), preferred_element_type=F32)


def _split2(x):
    hi = _bf(x)
    lo = _bf(x - hi.astype(F32))
    return hi, lo


def _split3(x):
    hi = _bf(x)
    r1 = x - hi.astype(F32)
    mid = _bf(r1)
    lo = _bf(r1 - mid.astype(F32))
    return hi, mid, lo


def _head_sums(xs, g, two_pass):
    m, w = xs[0].shape
    gw = g.shape[0]
    nt = w // gw
    parts = []
    for x, tp in zip(xs, two_pass):
        for piece in (_split2(x) if tp else (_bf(x),)):
            parts += [piece[:, t * gw:(t + 1) * gw] for t in range(nt)]
    r = _dot(jnp.concatenate(parts, axis=0), g)
    tile = lambda i: r[i * m:(i + 1) * m]
    outs, base = [], 0
    for tp in two_pass:
        if tp:
            cols = [tile(base + t) + tile(base + nt + t) for t in range(nt)]
        else:
            cols = [tile(base + t) for t in range(nt)]
        outs.append(jnp.concatenate(cols, axis=1))
        base += (2 if tp else 1) * nt
    return outs


def _sigmoid(x):
    return 1.0 / (1.0 + jnp.exp(-x))


def _regroup_rows_kernel(w_ref, o_ref, *, segs):
    total = sum(width for _, width in segs)
    if total < o_ref.shape[0]:
        t0 = total // 16 * 16
        o_ref[t0:, :] = jnp.zeros((o_ref.shape[0] - t0, o_ref.shape[1]), BF16)
    dst = 0
    for src, width in segs:
        o_ref[dst:dst + width, :] = _bf(w_ref[src:src + width, :])
        dst += width


def _regroup_rows(wt, segs, *, pad_to, tc):
    rows, cols = wt.shape
    n_out = -(-sum(width for _, width in segs) // pad_to) * pad_to
    assert cols % tc == 0 and all(src % 8 == 0 for src, _ in segs)
    return pl.pallas_call(
        functools.partial(_regroup_rows_kernel, segs=tuple(segs)),
        grid=(cols // tc,),
        in_specs=[pl.BlockSpec((rows, tc), lambda i: (0, i))],
        out_specs=pl.BlockSpec((n_out, tc), lambda i: (0, i)),
        out_shape=jax.ShapeDtypeStruct((n_out, cols), BF16),
        compiler_params=pltpu.CompilerParams(
            dimension_semantics=("parallel",), vmem_limit_bytes=VMEM_LIMIT),
        name="wperm",
    )(wt)


def _inproj_kernel(x_ref, g_ref, w_ref, o_ref, of_ref, h_scr, *, n_col_tiles):
    j = pl.program_id(1)

    @pl.when(j == 0)
    def _():
        x = x_ref[...]
        ms = jnp.mean(x * x, axis=-1, keepdims=True)
        h_scr[...] = _bf(x * lax.rsqrt(ms + RMS_EPS) * g_ref[...])

    acc = _dot_nt(h_scr[...], w_ref[...])
    o_ref[...] = _bf(acc)

    @pl.when(j == n_col_tiles - 1)
    def _():
        of_ref[...] = acc[:, acc.shape[1] - LANES:]


def _inproj(x2, norm_g, w_perm_t, *, tm, tn):
    T, D = x2.shape
    N = w_perm_t.shape[0]
    assert T % tm == 0 and N % tn == 0
    nj = N // tn
    return pl.pallas_call(
        functools.partial(_inproj_kernel, n_col_tiles=nj),
        grid=(T // tm, nj),
        in_specs=[
            pl.BlockSpec((tm, D), lambda i, j: (i, 0)),
            pl.BlockSpec((1, D), lambda i, j: (0, 0)),
            pl.BlockSpec((tn, D), lambda i, j: (j, 0)),
        ],
        out_specs=[
            pl.BlockSpec((tm, tn), lambda i, j: (i, j)),
            pl.BlockSpec((tm, LANES), lambda i, j: (i, 0)),
        ],
        out_shape=[
            jax.ShapeDtypeStruct((T, N), BF16),
            jax.ShapeDtypeStruct((T, LANES), F32),
        ],
        scratch_shapes=[pltpu.VMEM((tm, D), BF16)],
        compiler_params=pltpu.CompilerParams(
            dimension_semantics=("parallel", "arbitrary"), vmem_limit_bytes=VMEM_LIMIT),
        name="inproj",
    )(x2, norm_g, w_perm_t)


def _fprep_kernel(f_ref, fb_ref, sel_ref, aug_ref, *, blk):
    S = f_ref.shape[1]
    row = lax.broadcasted_iota(jnp.int32, (blk, blk), 0)
    col = lax.broadcasted_iota(jnp.int32, (blk, blk), 1)
    tril = _bf(jnp.where(row >= col, 1.0, 0.0))
    carry = jnp.zeros((1, LANES), F32)
    for i in range(S // blk):
        z = f_ref[0, i * blk:(i + 1) * blk, :] + fb_ref[...]
        lf = jnp.minimum(z, 0.0) - jnp.log1p(jnp.exp(-jnp.abs(z)))
        hi, mid, lo = _split3(lf)
        c = (_dot(tril, hi) + _dot(tril, mid) + _dot(tril, lo)) + carry
        pieces = jnp.concatenate(_split3(c * LOG2E), axis=1)
        aug_ref[0, i * blk:(i + 1) * blk, :] = _bf(_dot(pieces, sel_ref[...]))
        carry = c[blk - 1:blk, :]


def _fprep(uf3, fb_pad, sel):
    B, S, _ = uf3.shape
    return pl.pallas_call(
        functools.partial(_fprep_kernel, blk=256),
        grid=(B,),
        in_specs=[
            pl.BlockSpec((1, S, LANES), lambda b: (b, 0, 0)),
            pl.BlockSpec((1, LANES), lambda b: (0, 0)),
            pl.BlockSpec(sel.shape, lambda b: (0, 0)),
        ],
        out_specs=pl.BlockSpec((1, S, LANES), lambda b: (b, 0, 0)),
        out_shape=jax.ShapeDtypeStruct((B, S, LANES), BF16),
        compiler_params=pltpu.CompilerParams(dimension_semantics=("parallel",)),
        name="fprep",
    )(uf3, fb_pad, sel)


def _aug_selector(n_heads):
    assert AUG0 + 6 * n_heads <= LANES
    r = jnp.arange(LANES)[:, None]
    c = jnp.arange(LANES)[None, :]
    blocks = []
    for i in range(3):
        m = jnp.where(c == AUG0 + 6 * r + i, -1.0, jnp.where(c == AUG0 + 6 * r + 3 + i, 1.0, 0.0))
        blocks.append(jnp.where(r < n_heads, m, 0.0))
    return _bf(jnp.concatenate(blocks, axis=0))


_P_MU_R, _P_MU_K, _P_MU_V, _P_MU_G, _P_W0, _P_A0, _P_KK, _P_KA, _P_RK, _P_LW, _P_LB = range(11)


def _shift_mix(x, carry_row, mu):
    rolled = pltpu.roll(x, shift=1, axis=0)
    row = lax.broadcasted_iota(jnp.int32, x.shape, 0)
    prev = jnp.where(row == 0, carry_row, rolled)
    return x + (prev - x) * mu


def _block_diag(x, half_masks):
    xb = _bf(x)
    n_tiles = xb.shape[1] // LANES
    zero = jnp.zeros((xb.shape[0], LANES), BF16)
    rows = []
    for h in range(xb.shape[1] // HEAD):
        t = h // 2
        piece = xb[:, t * LANES:(t + 1) * LANES] * half_masks[h % 2]
        rows.append(jnp.concatenate([piece if i == t else zero for i in range(n_tiles)], axis=1))
    return jnp.concatenate(rows, axis=0)


def _rwkv_kernel(r_ref, k_ref, v_ref, g_ref, wa_ref, pv_ref, muwa_ref, lora_ref, gm_ref,
                 o_ref, state_scr, carry_scr):
    c = pl.program_id(1)
    NB = r_ref.shape[0]
    L = r_ref.shape[1]
    DA = r_ref.shape[2]
    n_groups = DA // GW

    @pl.when(c == 0)
    def _():
        state_scr[...] = jnp.zeros_like(state_scr)
        carry_scr[...] = jnp.zeros_like(carry_scr)

    def prm(i):
        return pv_ref[i:i + 1, :]

    gones = gm_ref[...]
    row = lax.broadcasted_iota(jnp.int32, (L, L), 0)
    col = lax.broadcasted_iota(jnp.int32, (L, L), 1)
    tril = _bf(jnp.where(row >= col, 1.0, 0.0))
    tril3 = jnp.concatenate([tril] * 3, axis=1)
    lane128 = lax.broadcasted_iota(jnp.int32, (1, LANES), 1)
    half_masks = [_bf(jnp.where(lane128 // HEAD == i, 1.0, 0.0)) for i in range(2)]
    prow = lax.broadcasted_iota(jnp.int32, (L, GW), 0)
    pcol = lax.broadcasted_iota(jnp.int32, (L, GW), 1) & (HEAD - 1)
    incl_p = prow >= pcol
    strict_p = prow > pcol
    eye_p = jnp.where(prow == pcol, 1.0, 0.0)
    st_mask = (lax.broadcasted_iota(jnp.int32, (GW, GW), 0) // HEAD
               == lax.broadcasted_iota(jnp.int32, (GW, GW), 1) // HEAD)
    bd = lambda x: _block_diag(x, half_masks)

    def prep(bb):
        raw = [ref[bb].astype(F32) for ref in (r_ref, k_ref, v_ref, g_ref, wa_ref)]
        mus = [prm(_P_MU_R), prm(_P_MU_K), prm(_P_MU_V), prm(_P_MU_G), muwa_ref[...]]
        mixed = []
        for i, x in enumerate(raw):
            wdt = x.shape[1]
            mixed.append(_shift_mix(x, carry_scr[bb, i:i + 1, 0:wdt], mus[i]))
            carry_scr[bb, i:i + 1, 0:wdt] = x[L - 1:L, :]
        r, k, v, gate, wa = mixed

        wa_act = jnp.where(lane128 < HEAD, jnp.tanh(wa), wa)
        lo = _dot(_bf(wa_act), lora_ref[...])
        ld = DECAY_SCALE * _sigmoid(prm(_P_W0) + lo[:, 0:DA])
        a = _sigmoid(prm(_P_A0) + lo[:, DA:2 * DA])

        kk = k * prm(_P_KK)
        k2 = k * (1.0 + (a - 1.0) * prm(_P_KA))
        ssq, bon = _head_sums([kk * kk, r * k2 * prm(_P_RK)], gones, (False, False))
        kk = kk * jnp.minimum(lax.rsqrt(ssq), 1e12)

        cum = _dot(tril3, jnp.concatenate(_split3(ld), axis=0))
        e_neg = jnp.exp(-cum)
        w_l = jnp.exp(cum[L - 1:L, :])
        a_t = -kk * jnp.exp(cum - ld)
        r_t = r * jnp.exp(cum)
        b_t = (kk * a) * e_neg
        k_t = k2 * e_neg
        return dict(v=v, gate=gate, bonus=bon * v, w_l=w_l, a_t=a_t, r_t=r_t, b_t=b_t, k_t=k_t)

    def chain(bbs, pre, ys):
        streams = [(i, g) for i in range(len(bbs)) for g in range(n_groups)]
        ns = len(streams)
        gsl = lambda g: slice(g * GW, (g + 1) * GW)
        P = lambda name, st: pre[st[0]][name][:, gsl(st[1])]
        sidx = lambda st: bbs[st[0]] * n_groups + st[1]

        s0 = [state_scr[sidx(st)] for st in streams]
        s0b = [_bf(x) for x in s0]
        ar = [_bf(jnp.concatenate([P("a_t", st), P("r_t", st)], axis=0)) for st in streams]
        bk = [jnp.concatenate([bd(P("b_t", st)), bd(P("k_t", st))], axis=0) for st in streams]
        pm = [_dot_nt(ar[i], bk[i]) for i in range(ns)]
        ars = [_dot_nt(ar[i], s0b[i]) for i in range(ns)]
        a_ab = [jnp.where(strict_p, p[0:L, 0:GW], 0.0) for p in pm]
        a_ak = [jnp.where(strict_p, p[0:L, GW:2 * GW], 0.0) for p in pm]
        a_rb = [jnp.where(incl_p, p[L:2 * L, 0:GW], 0.0) for p in pm]
        a_rk = [jnp.where(incl_p, p[L:2 * L, GW:2 * GW], 0.0) for p in pm]
        vbd = [bd(P("v", st)) for st in streams]
        akv = [_dot(_bf(jnp.concatenate([a_ak[i], a_rk[i]], axis=0)), vbd[i])
               for i in range(ns)]

        pw = [_dot(_bf(x), bd(x)) for x in a_ab]
        tinv = [eye_p + x for x in a_ab]
        n = 2
        while n < L:
            last = 2 * n >= L
            pwb = [bd(x) for x in pw]
            if last:
                tinv = [tinv[i] + _dot(_bf(tinv[i]), pwb[i]) for i in range(ns)]
            else:
                both = [_dot(_bf(jnp.concatenate([pw[i], tinv[i]], axis=0)), pwb[i])
                        for i in range(ns)]
                pw = [x[0:L] for x in both]
                tinv = [tinv[i] + both[i][L:2 * L] for i in range(ns)]
            n *= 2

        u = [_dot(_bf(tinv[i]), bd(ars[i][0:L] + akv[i][0:L])) for i in range(ns)]
        ys.extend(ars[i][L:2 * L] + akv[i][L:2 * L] + _dot(_bf(a_rb[i]), bd(u[i]))
                  for i in range(ns))
        for i, st in enumerate(streams):
            w_l = P("w_l", st)
            uv = _bf(jnp.concatenate([u[i], P("v", st)], axis=0))
            bkh = _bf(jnp.concatenate([P("b_t", st) * w_l, P("k_t", st) * w_l], axis=0))
            state_scr[sidx(st)] = s0[i] * w_l + jnp.where(st_mask, _dot_tn(uv, bkh), 0.0)

    def finish(bbs, pre, ys):
        nb = len(bbs)
        inv_n = 1.0 / HEAD
        y = [jnp.concatenate([ys[i * n_groups + g] for g in range(n_groups)], axis=1)
             for i in range(nb)]
        mean = [m * inv_n for m in _head_sums(y, gones, (True,) * nb)]
        yc = [y[i] - mean[i] for i in range(nb)]
        var = [s * inv_n for s in _head_sums([x * x for x in yc], gones, (False,) * nb)]
        for i, bb in enumerate(bbs):
            yn = yc[i] * lax.rsqrt(var[i] + LNX_EPS) * prm(_P_LW) + prm(_P_LB)
            gate = pre[i]["gate"]
            o_ref[bb] = _bf((yn + pre[i]["bonus"]) * (gate * _sigmoid(gate)))

    bbs = list(range(NB))
    pre = [prep(bb) for bb in bbs]
    ys = []
    chain(bbs, pre, ys)
    finish(bbs, pre, ys)


def _rwkv(u3, pvec, mu_wa, lora, gones, *, col_r, col_wa, d_a):
    B, S, _ = u3.shape
    L = CHUNK
    nb = RWKV_NB
    assert L == HEAD and d_a % GW == 0 and B % nb == 0 and S % L == 0
    cb = col_r // d_a
    blk = lambda off: pl.BlockSpec((nb, L, d_a), lambda b, c, off=off: (b, c, cb + off))
    full = lambda arr: pl.BlockSpec(arr.shape, lambda b, c: (0,) * arr.ndim)
    return pl.pallas_call(
        _rwkv_kernel,
        grid=(B // nb, S // L),
        in_specs=[
            blk(0), blk(1), blk(2), blk(3),
            pl.BlockSpec((nb, L, LANES), lambda b, c: (b, c, col_wa // LANES)),
            full(pvec), full(mu_wa), full(lora), full(gones),
        ],
        out_specs=pl.BlockSpec((nb, L, d_a), lambda b, c: (b, c, 0)),
        out_shape=jax.ShapeDtypeStruct((B, S, d_a), BF16),
        scratch_shapes=[
            pltpu.VMEM((nb * (d_a // GW), GW, GW), F32),
            pltpu.VMEM((nb, 8, d_a), F32),
        ],
        compiler_params=pltpu.CompilerParams(
            dimension_semantics=("parallel", "arbitrary"), vmem_limit_bytes=VMEM_LIMIT),
        name="rwkv7",
    )(u3, u3, u3, u3, u3, pvec, mu_wa, lora, gones)


def _fox_kernel(q_ref, k_ref, v_ref, g_ref, ak_ref, qg_ref, kg_ref, gm_ref,
                o_ref, kp_scr, vt_scr, qp_scr, sta_scr, stb_scr, m_scr, acc_scr, *, tq, tk):
    p = pl.program_id(1)
    S = k_ref.shape[1]
    nh = q_ref.shape[2] // HEAD
    g128 = gm_ref[...]
    lane = lax.broadcasted_iota(jnp.int32, (1, LANES), 1)
    feat = lane < HEAD

    def minus_lanes(hh):
        lo = AUG0 + 6 * (nh * p + hh)
        return (lane >= lo) & (lane < lo + 3)

    def plus_lanes(hh):
        lo = AUG0 + 6 * (nh * p + hh) + 3
        return (lane >= lo) & (lane < lo + 3)

    def head_tile(x, hh):
        t = x[:, (hh // 2) * LANES:(hh // 2 + 1) * LANES]
        return pltpu.roll(t, HEAD, axis=1) if hh % 2 else t

    for i in range(S // tk):
        rows = slice(i * tk, (i + 1) * tk)
        kb = k_ref[0, rows, :].astype(F32)
        (ssq,) = _head_sums([kb * kb], g128, (False,))
        kn = kb * lax.rsqrt(ssq * (1.0 / HEAD) + RMS_EPS) * kg_ref[...]
        aug = ak_ref[0, rows, :].astype(F32)
        for hh in range(nh):
            kp = jnp.where(feat, head_tile(kn, hh), jnp.where(plus_lanes(hh), 1.0, aug))
            kp_scr[hh, rows, :] = _bf(kp)
        vt = _bf(jnp.transpose(v_ref[0, rows, :].astype(F32)))
        for hh in range(nh):
            vt_scr[hh * VROWS:hh * VROWS + HEAD, rows] = vt[hh * HEAD:(hh + 1) * HEAD, :]
            vt_scr[hh * VROWS + HEAD:(hh + 1) * VROWS, rows] = jnp.ones((VROWS - HEAD, tk), BF16)

    def q_side(qi):
        rows = slice(qi * tq, (qi + 1) * tq)
        q = q_ref[0, rows, :].astype(F32)
        (ssq,) = _head_sums([q * q], g128, (False,))
        qn = q * lax.rsqrt(ssq * (1.0 / HEAD) + RMS_EPS) * (qg_ref[...] * (HEAD ** -0.5 * LOG2E))
        augq = ak_ref[0, rows, :].astype(F32)
        for hh in range(nh):
            qp_scr[qi % 2, hh] = _bf(jnp.where(
                feat, head_tile(qn, hh),
                jnp.where(minus_lanes(hh), 1.0, jnp.where(plus_lanes(hh), augq, 0.0))))
        m_scr[qi % 2] = jnp.full(m_scr.shape[1:], -1e30, F32)
        acc_scr[qi % 2] = jnp.zeros(acc_scr.shape[1:], F32)

    def scores(qi, j, st_ref):
        for hh in range(nh):
            st_ref[hh] = _dot_nt(kp_scr[hh, j * tk:(j + 1) * tk, :], qp_scr[qi % 2, hh])

    def softmax_pv(qi, j, st_ref):
        st = [st_ref[hh] for hh in range(nh)]
        if j == qi:
            keyi = lax.broadcasted_iota(jnp.int32, (tk, tq), 0)
            qryi = lax.broadcasted_iota(jnp.int32, (tk, tq), 1)
            st = [jnp.where(qryi >= keyi, x, -1e30) for x in st]
        m_old = [m_scr[qi % 2, hh, 0:1, :] for hh in range(nh)]
        m_new = [jnp.maximum(m_old[hh], jnp.max(st[hh], axis=0, keepdims=True)) for hh in range(nh)]
        alpha = [jnp.exp2(m_old[hh] - m_new[hh]) for hh in range(nh)]
        pt = [jnp.exp2(_bf(st[hh] - m_new[hh])) for hh in range(nh)]
        pv = [_dot(vt_scr[hh * VROWS:(hh + 1) * VROWS, j * tk:(j + 1) * tk], pt[hh])
              for hh in range(nh)]
        for hh in range(nh):
            m_scr[qi % 2, hh, 0:1, :] = m_new[hh]
            acc_scr[qi % 2, hh] = alpha[hh] * acc_scr[qi % 2, hh] + pv[hh]

    def finish(qi):
        rows = slice(qi * tq, (qi + 1) * tq)
        ot = jnp.concatenate(
            [acc_scr[qi % 2, hh, 0:HEAD, :] * (1.0 / acc_scr[qi % 2, hh, HEAD:HEAD + 1, :])
             for hh in range(nh)], axis=0)
        g = g_ref[0, rows, :].astype(F32)
        o_ref[0, rows, :] = _bf(jnp.transpose(ot) * (g * _sigmoid(g)))

    pairs = [(qi, j) for qi in range(S // tq) for j in range(qi + 1)]
    bufs = (sta_scr, stb_scr)
    for n, (qi, j) in enumerate(pairs):
        if j == 0:
            q_side(qi)
        scores(qi, j, bufs[n % 2])
        if n > 0:
            pqi, pj = pairs[n - 1]
            softmax_pv(pqi, pj, bufs[(n - 1) % 2])
            if pj == pqi:
                finish(pqi)
    qi, j = pairs[-1]
    softmax_pv(qi, j, bufs[(len(pairs) - 1) % 2])
    finish(qi)


def _fox(u3, aug, qg2, kg2, g128, *, col_q, d_b, tq, tk):
    B, S, _ = u3.shape
    fw = qg2.shape[1]
    nh = fw // HEAD
    assert S % tq == 0 and tq == tk and d_b % fw == 0
    cq, ck, cv, cg = ((col_q + i * d_b) // fw for i in range(4))
    return pl.pallas_call(
        functools.partial(_fox_kernel, tq=tq, tk=tk),
        grid=(B, d_b // fw),
        in_specs=[
            pl.BlockSpec((1, S, fw), lambda b, p: (b, 0, cq + p)),
            pl.BlockSpec((1, S, fw), lambda b, p: (b, 0, ck + p)),
            pl.BlockSpec((1, S, fw), lambda b, p: (b, 0, cv + p)),
            pl.BlockSpec((1, S, fw), lambda b, p: (b, 0, cg + p)),
            pl.BlockSpec((1, S, LANES), lambda b, p: (b, 0, 0)),
            pl.BlockSpec((1, fw), lambda b, p: (0, 0)),
            pl.BlockSpec((1, fw), lambda b, p: (0, 0)),
            pl.BlockSpec((LANES, LANES), lambda b, p: (0, 0)),
        ],
        out_specs=pl.BlockSpec((1, S, fw), lambda b, p: (b, 0, p)),
        out_shape=jax.ShapeDtypeStruct((B, S, d_b), BF16),
        scratch_shapes=[
            pltpu.VMEM((nh, S, LANES), BF16),
            pltpu.VMEM((nh * VROWS, S), BF16),
            pltpu.VMEM((2, nh, tq, LANES), BF16),
            pltpu.VMEM((nh, tk, tq), F32),
            pltpu.VMEM((nh, tk, tq), F32),
            pltpu.VMEM((2, nh, 8, tq), F32),
            pltpu.VMEM((2, nh, VROWS, tq), F32),
        ],
        compiler_params=pltpu.CompilerParams(
            dimension_semantics=("parallel", "parallel"), vmem_limit_bytes=VMEM_LIMIT),
        name="fox",
    )(u3, u3, u3, u3, aug, qg2, kg2, g128)


def _out_kernel(x_ref, ya_ref, yb_ref, ga_ref, gb_ref, woa_ref, wob_ref, wo_ref, fg_ref, o_ref,
                woa_s, wob_s, wo_s):
    @pl.when(pl.program_id(0) == 0)
    def _():
        woa_s[...] = _bf(woa_ref[...])
        wob_s[...] = _bf(wob_ref[...])
        wo_s[...] = _bf(wo_ref[...])

    za = _dot(ya_ref[...], woa_s[...])
    zb = _dot(yb_ref[...], wob_s[...])
    merged = (_sigmoid(ga_ref[...].astype(F32)) * za + _sigmoid(gb_ref[...].astype(F32)) * zb)
    o = x_ref[...] + _dot(_bf(merged), wo_s[...])
    ms = jnp.mean(o * o, axis=-1, keepdims=True)
    o_ref[...] = o * lax.rsqrt(ms + RMS_EPS) * fg_ref[...]


def _out(x2, ya2, yb2, u2, woa, wob, wo, fg, *, tm):
    T, D = x2.shape
    assert T % tm == 0
    full = lambda arr: pl.BlockSpec(arr.shape, lambda i: (0,) * arr.ndim)
    return pl.pallas_call(
        _out_kernel,
        grid=(T // tm,),
        in_specs=[
            pl.BlockSpec((tm, D), lambda i: (i, 0)),
            pl.BlockSpec((tm, ya2.shape[1]), lambda i: (i, 0)),
            pl.BlockSpec((tm, yb2.shape[1]), lambda i: (i, 0)),
            pl.BlockSpec((tm, D), lambda i: (i, 0)),
            pl.BlockSpec((tm, D), lambda i: (i, 1)),
            full(woa), full(wob), full(wo), full(fg),
        ],
        out_specs=pl.BlockSpec((tm, D), lambda i: (i, 0)),
        out_shape=jax.ShapeDtypeStruct((T, D), F32),
        scratch_shapes=[pltpu.VMEM(w.shape, BF16) for w in (woa, wob, wo)],
        compiler_params=pltpu.CompilerParams(
            dimension_semantics=("arbitrary",), vmem_limit_bytes=VMEM_LIMIT),
        name="outstage",
    )(x2, ya2, yb2, u2, u2, woa, wob, wo, fg)


def _block_ones(width):
    i = jnp.arange(width) // HEAD
    return (i[:, None] == i[None, :]).astype(BF16)


def _layer(x2, B, S, norm_g, w_in, shift_mu, w_lora_up, w0, a_lora_up, a0, k_k, k_a, r_k,
           lnx_w, lnx_b, f_bias, q_norm_g, k_norm_g, w_out_a, w_out_b, w_out, out_gain):
    T, D = x2.shape
    d_a = w0.shape[0]
    d_b = w_out_b.shape[0]
    rank = w_lora_up.shape[0]
    h_b = f_bias.shape[0]
    rw = 4 * d_a + 2 * rank
    fx = 4 * d_b + h_b

    segs = [(rw + fx, 2 * D), (0, 3 * d_a), (3 * d_a + 2 * rank, d_a), (rw, 4 * d_b),
            (3 * d_a, 2 * rank), (rw + 4 * d_b, h_b)]
    w_perm_t = _regroup_rows(jnp.transpose(w_in), segs, pad_to=LANES, tc=LANES)
    col_r = 2 * D
    col_q = col_r + 4 * d_a
    col_wa = col_q + 4 * d_b

    u2, uf = _inproj(x2, norm_g.reshape(1, D), w_perm_t, tm=2048, tn=1280)
    u3 = u2.reshape(B, S, u2.shape[1])

    mu = shift_mu
    rows = [mu[:d_a], mu[d_a:2 * d_a], mu[2 * d_a:3 * d_a], mu[3 * d_a + 2 * rank:],
            w0, a0, k_k, k_a, r_k.reshape(-1), lnx_w, lnx_b]
    pvec = jnp.stack(rows + [jnp.zeros_like(w0)] * (16 - len(rows)), axis=0)
    mu_wa = mu[3 * d_a:3 * d_a + 2 * rank].reshape(1, 2 * rank)
    z = jnp.zeros((rank, d_a), F32)
    lora = _bf(jnp.concatenate(
        [jnp.concatenate([w_lora_up, z], axis=1), jnp.concatenate([z, a_lora_up], axis=1)], axis=0))
    ya = _rwkv(u3, pvec, mu_wa, lora, _block_ones(GW), col_r=col_r, col_wa=col_wa, d_a=d_a)

    fb_pad = jnp.pad(f_bias, (0, LANES - h_b)).reshape(1, LANES)
    aug = _fprep(uf.reshape(B, S, LANES), fb_pad, _aug_selector(h_b))
    qg2 = jnp.tile(q_norm_g, FOX_HEADS).reshape(1, FOX_HEADS * HEAD)
    kg2 = jnp.tile(k_norm_g, FOX_HEADS).reshape(1, FOX_HEADS * HEAD)
    yb = _fox(u3, aug, qg2, kg2, _block_ones(LANES), col_q=col_q, d_b=d_b, tq=512, tk=512)

    return _out(x2, ya.reshape(T, d_a), yb.reshape(T, d_b), u2,
                w_out_a, w_out_b, w_out, out_gain.reshape(1, D), tm=512)


def kernel(x, norm_g, w_in, shift_mu, w_lora_up, w0, a_lora_up, a0, k_k, k_a, r_k, lnx_w, lnx_b,
           f_bias, q_norm_g, k_norm_g, w_out_a, w_out_b, w_out, final_norm_g):
    B, S, D = x.shape
    depth = w_in.shape[0]
    assert depth == 1, "the fused output stage applies the final norm after the single layer"
    x2 = x.reshape(B * S, D)
    out = _layer(x2, B, S, norm_g[0], w_in[0], shift_mu[0], w_lora_up[0], w0[0], a_lora_up[0],
                 a0[0], k_k[0], k_a[0], r_k[0], lnx_w[0], lnx_b[0], f_bias[0], q_norm_g[0],
                 k_norm_g[0], w_out_a[0], w_out_b[0], w_out[0], final_norm_g)
    return out.reshape(B, S, D)
```

```python
import functools

import jax
import jax.numpy as jnp
from jax import lax
from jax.experimental import pallas as pl
from jax.experimental.pallas import tpu as pltpu

F32 = jnp.float32
BF16 = jnp.bfloat16

HEAD = 64
LANES = 128
RMS_EPS = 1e-6
LNX_EPS = 64e-5
CHUNK = 64
GROUP = 4
GW = GROUP * HEAD
RWKV_NB = 8
FOX_HEADS = 4
AUG0 = HEAD
VROWS = HEAD + 16
LOG2E = 1.4426950408889634
DECAY_SCALE = -0.6065306597126334
VMEM_LIMIT = 56 * 1024 * 1024


def _bf(x):
    return x.astype(BF16)


def _dot(a, b):
    return jnp.dot(a, b, preferred_element_type=F32)


def _dot_nt(a, b):
    return lax.dot_general(a, b, (((1,), (1,)), ((), ())), preferred_element_type=F32)


def _dot_tn(a, b):
    return lax.dot_general(a, b, (((0,), (0,)), ((), ())), preferred_element_type=F32)


def _split2(x):
    hi = _bf(x)
    lo = _bf(x - hi.astype(F32))
    return hi, lo


def _split3(x):
    hi = _bf(x)
    r1 = x - hi.astype(F32)
    mid = _bf(r1)
    lo = _bf(r1 - mid.astype(F32))
    return hi, mid, lo


def _head_sums(xs, g, two_pass):
    m, w = xs[0].shape
    gw = g.shape[0]
    nt = w // gw
    parts = []
    for x, tp in zip(xs, two_pass):
        for piece in (_split2(x) if tp else (_bf(x),)):
            parts += [piece[:, t * gw:(t + 1) * gw] for t in range(nt)]
    r = _dot(jnp.concatenate(parts, axis=0), g)
    tile = lambda i: r[i * m:(i + 1) * m]
    outs, base = [], 0
    for tp in two_pass:
        if tp:
            cols = [tile(base + t) + tile(base + nt + t) for t in range(nt)]
        else:
            cols = [tile(base + t) for t in range(nt)]
        outs.append(jnp.concatenate(cols, axis=1))
        base += (2 if tp else 1) * nt
    return outs


def _sigmoid(x):
    return 1.0 / (1.0 + jnp.exp(-x))


def _regroup_rows_kernel(w_ref, o_ref, *, segs):
    total = sum(width for _, width in segs)
    if total < o_ref.shape[0]:
        t0 = total // 16 * 16
        o_ref[t0:, :] = jnp.zeros((o_ref.shape[0] - t0, o_ref.shape[1]), BF16)
    dst = 0
    for src, width in segs:
        o_ref[dst:dst + width, :] = _bf(w_ref[src:src + width, :])
        dst += width


def _regroup_rows(wt, segs, *, pad_to, tc):
    rows, cols = wt.shape
    n_out = -(-sum(width for _, width in segs) // pad_to) * pad_to
    assert cols % tc == 0 and all(src % 8 == 0 for src, _ in segs)
    return pl.pallas_call(
        functools.partial(_regroup_rows_kernel, segs=tuple(segs)),
        grid=(cols // tc,),
        in_specs=[pl.BlockSpec((rows, tc), lambda i: (0, i))],
        out_specs=pl.BlockSpec((n_out, tc), lambda i: (0, i)),
        out_shape=jax.ShapeDtypeStruct((n_out, cols), BF16),
        compiler_params=pltpu.CompilerParams(
            dimension_semantics=("parallel",), vmem_limit_bytes=VMEM_LIMIT),
        name="wperm",
    )(wt)


def _inproj_kernel(x_ref, g_ref, w_ref, o_ref, of_ref, h_scr, *, n_col_tiles):
    j = pl.program_id(1)

    @pl.when(j == 0)
    def _():
        x = x_ref[...]
        ms = jnp.mean(x * x, axis=-1, keepdims=True)
        h_scr[...] = _bf(x * lax.rsqrt(ms + RMS_EPS) * g_ref[...])

    acc = _dot_nt(h_scr[...], w_ref[...])
    o_ref[...] = _bf(acc)

    @pl.when(j == n_col_tiles - 1)
    def _():
        of_ref[...] = acc[:, acc.shape[1] - LANES:]


def _inproj(x2, norm_g, w_perm_t, *, tm, tn):
    T, D = x2.shape
    N = w_perm_t.shape[0]
    assert T % tm == 0 and N % tn == 0
    nj = N // tn
    return pl.pallas_call(
        functools.partial(_inproj_kernel, n_col_tiles=nj),
        grid=(T // tm, nj),
        in_specs=[
            pl.BlockSpec((tm, D), lambda i, j: (i, 0)),
            pl.BlockSpec((1, D), lambda i, j: (0, 0)),
            pl.BlockSpec((tn, D), lambda i, j: (j, 0)),
        ],
        out_specs=[
            pl.BlockSpec((tm, tn), lambda i, j: (i, j)),
            pl.BlockSpec((tm, LANES), lambda i, j: (i, 0)),
        ],
        out_shape=[
            jax.ShapeDtypeStruct((T, N), BF16),
            jax.ShapeDtypeStruct((T, LANES), F32),
        ],
        scratch_shapes=[pltpu.VMEM((tm, D), BF16)],
        compiler_params=pltpu.CompilerParams(
            dimension_semantics=("parallel", "arbitrary"), vmem_limit_bytes=VMEM_LIMIT),
        name="inproj",
    )(x2, norm_g, w_perm_t)


def _fprep_kernel(f_ref, fb_ref, sel_ref, aug_ref, *, blk):
    S = f_ref.shape[1]
    row = lax.broadcasted_iota(jnp.int32, (blk, blk), 0)
    col = lax.broadcasted_iota(jnp.int32, (blk, blk), 1)
    tril = _bf(jnp.where(row >= col, 1.0, 0.0))
    carry = jnp.zeros((1, LANES), F32)
    for i in range(S // blk):
        z = f_ref[0, i * blk:(i + 1) * blk, :] + fb_ref[...]
        lf = jnp.minimum(z, 0.0) - jnp.log1p(jnp.exp(-jnp.abs(z)))
        hi, mid, lo = _split3(lf)
        c = (_dot(tril, hi) + _dot(tril, mid) + _dot(tril, lo)) + carry
        pieces = jnp.concatenate(_split3(c * LOG2E), axis=1)
        aug_ref[0, i * blk:(i + 1) * blk, :] = _bf(_dot(pieces, sel_ref[...]))
        carry = c[blk - 1:blk, :]


def _fprep(uf3, fb_pad, sel):
    B, S, _ = uf3.shape
    return pl.pallas_call(
        functools.partial(_fprep_kernel, blk=256),
        grid=(B,),
        in_specs=[
            pl.BlockSpec((1, S, LANES), lambda b: (b, 0, 0)),
            pl.BlockSpec((1, LANES), lambda b: (0, 0)),
            pl.BlockSpec(sel.shape, lambda b: (0, 0)),
        ],
        out_specs=pl.BlockSpec((1, S, LANES), lambda b: (b, 0, 0)),
        out_shape=jax.ShapeDtypeStruct((B, S, LANES), BF16),
        compiler_params=pltpu.CompilerParams(dimension_semantics=("parallel",)),
        name="fprep",
    )(uf3, fb_pad, sel)


def _aug_selector(n_heads):
    assert AUG0 + 6 * n_heads <= LANES
    r = jnp.arange(LANES)[:, None]
    c = jnp.arange(LANES)[None, :]
    blocks = []
    for i in range(3):
        m = jnp.where(c == AUG0 + 6 * r + i, -1.0, jnp.where(c == AUG0 + 6 * r + 3 + i, 1.0, 0.0))
        blocks.append(jnp.where(r < n_heads, m, 0.0))
    return _bf(jnp.concatenate(blocks, axis=0))


_P_MU_R, _P_MU_K, _P_MU_V, _P_MU_G, _P_W0, _P_A0, _P_KK, _P_KA, _P_RK, _P_LW, _P_LB = range(11)


def _shift_mix(x, carry_row, mu):
    rolled = pltpu.roll(x, shift=1, axis=0)
    row = lax.broadcasted_iota(jnp.int32, x.shape, 0)
    prev = jnp.where(row == 0, carry_row, rolled)
    return x + (prev - x) * mu


def _block_diag(x, half_masks):
    xb = _bf(x)
    n_tiles = xb.shape[1] // LANES
    zero = jnp.zeros((xb.shape[0], LANES), BF16)
    rows = []
    for h in range(xb.shape[1] // HEAD):
        t = h // 2
        piece = xb[:, t * LANES:(t + 1) * LANES] * half_masks[h % 2]
        rows.append(jnp.concatenate([piece if i == t else zero for i in range(n_tiles)], axis=1))
    return jnp.concatenate(rows, axis=0)


def _rwkv_kernel(r_ref, k_ref, v_ref, g_ref, wa_ref, pv_ref, muwa_ref, lora_ref, gm_ref,
                 o_ref, state_scr, carry_scr):
    c = pl.program_id(1)
    NB = r_ref.shape[0]
    L = r_ref.shape[1]
    DA = r_ref.shape[2]
    n_groups = DA // GW

    @pl.when(c == 0)
    def _():
        state_scr[...] = jnp.zeros_like(state_scr)
        carry_scr[...] = jnp.zeros_like(carry_scr)

    def prm(i):
        return pv_ref[i:i + 1, :]

    gones = gm_ref[...]
    row = lax.broadcasted_iota(jnp.int32, (L, L), 0)
    col = lax.broadcasted_iota(jnp.int32, (L, L), 1)
    tril = _bf(jnp.where(row >= col, 1.0, 0.0))
    lane128 = lax.broadcasted_iota(jnp.int32, (1, LANES), 1)
    half_masks = [_bf(jnp.where(lane128 // HEAD == i, 1.0, 0.0)) for i in range(2)]
    prow = lax.broadcasted_iota(jnp.int32, (L, GW), 0)
    pcol = lax.broadcasted_iota(jnp.int32, (L, GW), 1) & (HEAD - 1)
    incl_p = prow >= pcol
    strict_p = prow > pcol
    eye_p = jnp.where(prow == pcol, 1.0, 0.0)
    st_mask = (lax.broadcasted_iota(jnp.int32, (GW, GW), 0) // HEAD
               == lax.broadcasted_iota(jnp.int32, (GW, GW), 1) // HEAD)
    bd = lambda x: _block_diag(x, half_masks)

    def prep(bb):
        raw = [ref[bb].astype(F32) for ref in (r_ref, k_ref, v_ref, g_ref, wa_ref)]
        mus = [prm(_P_MU_R), prm(_P_MU_K), prm(_P_MU_V), prm(_P_MU_G), muwa_ref[...]]
        mixed = []
        for i, x in enumerate(raw):
            wdt = x.shape[1]
            mixed.append(_shift_mix(x, carry_scr[bb, i:i + 1, 0:wdt], mus[i]))
            carry_scr[bb, i:i + 1, 0:wdt] = x[L - 1:L, :]
        r, k, v, gate, wa = mixed

        wa_act = jnp.where(lane128 < HEAD, jnp.tanh(wa), wa)
        lo = _dot(_bf(wa_act), lora_ref[...])
        ld = DECAY_SCALE * _sigmoid(prm(_P_W0) + lo[:, 0:DA])
        a = _sigmoid(prm(_P_A0) + lo[:, DA:2 * DA])

        kk = k * prm(_P_KK)
        k2 = k * (1.0 + (a - 1.0) * prm(_P_KA))
        ssq, bon = _head_sums([kk * kk, r * k2 * prm(_P_RK)], gones, (False, False))
        kk = kk * jnp.minimum(lax.rsqrt(ssq), 1e12)

        h3, m3, l3 = _split3(ld)
        cum = _dot(tril, h3) + _dot(tril, m3) + _dot(tril, l3)
        e_neg = jnp.exp(-cum)
        w_l = jnp.exp(cum[L - 1:L, :])
        a_t = -kk * jnp.exp(cum - ld)
        r_t = r * jnp.exp(cum)
        b_t = (kk * a) * e_neg
        k_t = k2 * e_neg
        return dict(v=v, gate=gate, bonus=bon * v, w_l=w_l, a_t=a_t, r_t=r_t, b_t=b_t, k_t=k_t)

    def chain(bbs, pre, ys):
        streams = [(i, g) for i in range(len(bbs)) for g in range(n_groups)]
        ns = len(streams)
        gsl = lambda g: slice(g * GW, (g + 1) * GW)
        P = lambda name, st: pre[st[0]][name][:, gsl(st[1])]
        sidx = lambda st: bbs[st[0]] * n_groups + st[1]

        s0 = [state_scr[sidx(st)] for st in streams]
        s0b = [_bf(x) for x in s0]
        ar = [_bf(jnp.concatenate([P("a_t", st), P("r_t", st)], axis=0)) for st in streams]
        bk = [jnp.concatenate([bd(P("b_t", st)), bd(P("k_t", st))], axis=0) for st in streams]
        pm = [_dot_nt(ar[i], bk[i]) for i in range(ns)]
        ars = [_dot_nt(ar[i], s0b[i]) for i in range(ns)]
        a_ab = [jnp.where(strict_p, p[0:L, 0:GW], 0.0) for p in pm]
        a_ak = [jnp.where(strict_p, p[0:L, GW:2 * GW], 0.0) for p in pm]
        a_rb = [jnp.where(incl_p, p[L:2 * L, 0:GW], 0.0) for p in pm]
        a_rk = [jnp.where(incl_p, p[L:2 * L, GW:2 * GW], 0.0) for p in pm]
        vbd = [bd(P("v", st)) for st in streams]
        akv = [_dot(_bf(jnp.concatenate([a_ak[i], a_rk[i]], axis=0)), vbd[i])
               for i in range(ns)]

        pw = [_dot(_bf(x), bd(x)) for x in a_ab]
        tinv = [eye_p + x for x in a_ab]
        n = 2
        while n < L:
            last = 2 * n >= L
            pwb = [bd(x) for x in pw]
            if last:
                tinv = [tinv[i] + _dot(_bf(tinv[i]), pwb[i]) for i in range(ns)]
            else:
                both = [_dot(_bf(jnp.concatenate([pw[i], tinv[i]], axis=0)), pwb[i])
                        for i in range(ns)]
                pw = [x[0:L] for x in both]
                tinv = [tinv[i] + both[i][L:2 * L] for i in range(ns)]
            n *= 2

        u = [_dot(_bf(tinv[i]), bd(ars[i][0:L] + akv[i][0:L])) for i in range(ns)]
        ys.extend(ars[i][L:2 * L] + akv[i][L:2 * L] + _dot(_bf(a_rb[i]), bd(u[i]))
                  for i in range(ns))
        for i, st in enumerate(streams):
            w_l = P("w_l", st)
            uv = _bf(jnp.concatenate([u[i], P("v", st)], axis=0))
            bkh = _bf(jnp.concatenate([P("b_t", st) * w_l, P("k_t", st) * w_l], axis=0))
            state_scr[sidx(st)] = s0[i] * w_l + jnp.where(st_mask, _dot_tn(uv, bkh), 0.0)

    def finish(bbs, pre, ys):
        nb = len(bbs)
        inv_n = 1.0 / HEAD
        y = [jnp.concatenate([ys[i * n_groups + g] for g in range(n_groups)], axis=1)
             for i in range(nb)]
        mean = [m * inv_n for m in _head_sums(y, gones, (True,) * nb)]
        yc = [y[i] - mean[i] for i in range(nb)]
        var = [s * inv_n for s in _head_sums([x * x for x in yc], gones, (False,) * nb)]
        for i, bb in enumerate(bbs):
            yn = yc[i] * lax.rsqrt(var[i] + LNX_EPS) * prm(_P_LW) + prm(_P_LB)
            gate = pre[i]["gate"]
            o_ref[bb] = _bf((yn + pre[i]["bonus"]) * (gate * _sigmoid(gate)))

    bbs = list(range(NB))
    pre = [prep(bb) for bb in bbs]
    ys = []
    chain(bbs, pre, ys)
    finish(bbs, pre, ys)


def _rwkv(u3, pvec, mu_wa, lora, gones, *, col_r, col_wa, d_a):
    B, S, _ = u3.shape
    L = CHUNK
    nb = RWKV_NB
    assert L == HEAD and d_a % GW == 0 and B % nb == 0 and S % L == 0
    cb = col_r // d_a
    blk = lambda off: pl.BlockSpec((nb, L, d_a), lambda b, c, off=off: (b, c, cb + off))
    full = lambda arr: pl.BlockSpec(arr.shape, lambda b, c: (0,) * arr.ndim)
    return pl.pallas_call(
        _rwkv_kernel,
        grid=(B // nb, S // L),
        in_specs=[
            blk(0), blk(1), blk(2), blk(3),
            pl.BlockSpec((nb, L, LANES), lambda b, c: (b, c, col_wa // LANES)),
            full(pvec), full(mu_wa), full(lora), full(gones),
        ],
        out_specs=pl.BlockSpec((nb, L, d_a), lambda b, c: (b, c, 0)),
        out_shape=jax.ShapeDtypeStruct((B, S, d_a), BF16),
        scratch_shapes=[
            pltpu.VMEM((nb * (d_a // GW), GW, GW), F32),
            pltpu.VMEM((nb, 8, d_a), F32),
        ],
        compiler_params=pltpu.CompilerParams(
            dimension_semantics=("parallel", "arbitrary"), vmem_limit_bytes=VMEM_LIMIT),
        name="rwkv7",
    )(u3, u3, u3, u3, u3, pvec, mu_wa, lora, gones)


def _sublane_max(x):
    for s in (4, 2, 1):
        x = jnp.maximum(x, pltpu.roll(x, s, axis=0))
    return x


def _fox_kernel(q_ref, k_ref, v_ref, g_ref, ak_ref, qg_ref, kg_ref, gm_ref,
                o_ref, kp_scr, vt_scr, qp_scr, sta_scr, stb_scr, m_scr, acc_scr, *, tq, tk):
    p = pl.program_id(1)
    S = k_ref.shape[1]
    nh = q_ref.shape[2] // HEAD
    g128 = gm_ref[...]
    lane = lax.broadcasted_iota(jnp.int32, (1, LANES), 1)
    feat = lane < HEAD

    def minus_lanes(hh):
        lo = AUG0 + 6 * (nh * p + hh)
        return (lane >= lo) & (lane < lo + 3)

    def plus_lanes(hh):
        lo = AUG0 + 6 * (nh * p + hh) + 3
        return (lane >= lo) & (lane < lo + 3)

    def head_tile(x, hh):
        t = x[:, (hh // 2) * LANES:(hh // 2 + 1) * LANES]
        return pltpu.roll(t, HEAD, axis=1) if hh % 2 else t

    for i in range(S // tk):
        rows = slice(i * tk, (i + 1) * tk)
        kb = k_ref[0, rows, :].astype(F32)
        (ssq,) = _head_sums([kb * kb], g128, (False,))
        kn = kb * lax.rsqrt(ssq * (1.0 / HEAD) + RMS_EPS) * kg_ref[...]
        aug = ak_ref[0, rows, :].astype(F32)
        for hh in range(nh):
            kp = jnp.where(feat, head_tile(kn, hh), jnp.where(plus_lanes(hh), 1.0, aug))
            kp_scr[hh, rows, :] = _bf(kp)
        vt = _bf(jnp.transpose(v_ref[0, rows, :].astype(F32)))
        for hh in range(nh):
            vt_scr[hh * VROWS:hh * VROWS + HEAD, rows] = vt[hh * HEAD:(hh + 1) * HEAD, :]
            vt_scr[hh * VROWS + HEAD:(hh + 1) * VROWS, rows] = jnp.ones((VROWS - HEAD, tk), BF16)

    def q_side(qi):
        rows = slice(qi * tq, (qi + 1) * tq)
        q = q_ref[0, rows, :].astype(F32)
        (ssq,) = _head_sums([q * q], g128, (False,))
        qn = q * lax.rsqrt(ssq * (1.0 / HEAD) + RMS_EPS) * (qg_ref[...] * (HEAD ** -0.5 * LOG2E))
        augq = ak_ref[0, rows, :].astype(F32)
        for hh in range(nh):
            qp_scr[qi % 2, hh] = _bf(jnp.where(
                feat, head_tile(qn, hh),
                jnp.where(minus_lanes(hh), 1.0, jnp.where(plus_lanes(hh), augq, 0.0))))
        m_scr[qi % 2] = jnp.full(m_scr.shape[1:], -1e30, F32)
        acc_scr[qi % 2] = jnp.zeros(acc_scr.shape[1:], F32)

    def scores(qi, j, st_ref):
        for hh in range(nh):
            st_ref[hh] = _dot_nt(kp_scr[hh, j * tk:(j + 1) * tk, :], qp_scr[qi % 2, hh])

    def softmax_pv(qi, j, st_ref):
        st = [st_ref[hh] for hh in range(nh)]
        if j == qi:
            keyi = lax.broadcasted_iota(jnp.int32, (tk, tq), 0)
            qryi = lax.broadcasted_iota(jnp.int32, (tk, tq), 1)
            st = [jnp.where(qryi >= keyi, x, -1e30) for x in st]
        st = [x.reshape(tk // 8, 8, tq) for x in st]
        m_old = [m_scr[qi % 2, hh] for hh in range(nh)]
        m_new = [jnp.maximum(m_old[hh], _sublane_max(jnp.max(st[hh], axis=0))) for hh in range(nh)]
        alpha = [jnp.exp2(m_old[hh] - m_new[hh]) for hh in range(nh)]
        pt = [_bf(jnp.exp2(st[hh] - m_new[hh][None]).reshape(tk, tq)) for hh in range(nh)]
        pv = [_dot(vt_scr[hh * VROWS:(hh + 1) * VROWS, j * tk:(j + 1) * tk], pt[hh])
              for hh in range(nh)]
        for hh in range(nh):
            m_scr[qi % 2, hh] = m_new[hh]
            acc = acc_scr[qi % 2, hh].reshape(VROWS // 8, 8, tq) * alpha[hh][None]
            acc_scr[qi % 2, hh] = acc.reshape(VROWS, tq) + pv[hh]

    def finish(qi):
        rows = slice(qi * tq, (qi + 1) * tq)
        ot = jnp.concatenate(
            [(acc_scr[qi % 2, hh, 0:HEAD, :].reshape(HEAD // 8, 8, tq)
              * (1.0 / acc_scr[qi % 2, hh, HEAD:HEAD + 8, :])[None]).reshape(HEAD, tq)
             for hh in range(nh)], axis=0)
        g = g_ref[0, rows, :].astype(F32)
        o_ref[0, rows, :] = _bf(jnp.transpose(ot) * (g * _sigmoid(g)))

    pairs = [(qi, j) for qi in range(S // tq) for j in range(qi + 1)]
    bufs = (sta_scr, stb_scr)
    for n, (qi, j) in enumerate(pairs):
        if j == 0:
            q_side(qi)
        scores(qi, j, bufs[n % 2])
        if n > 0:
            pqi, pj = pairs[n - 1]
            softmax_pv(pqi, pj, bufs[(n - 1) % 2])
            if pj == pqi:
                finish(pqi)
    qi, j = pairs[-1]
    softmax_pv(qi, j, bufs[(len(pairs) - 1) % 2])
    finish(qi)


def _fox(u3, aug, qg2, kg2, g128, *, col_q, d_b, tq, tk):
    B, S, _ = u3.shape
    fw = qg2.shape[1]
    nh = fw // HEAD
    assert S % tq == 0 and tq == tk and d_b % fw == 0
    cq, ck, cv, cg = ((col_q + i * d_b) // fw for i in range(4))
    return pl.pallas_call(
        functools.partial(_fox_kernel, tq=tq, tk=tk),
        grid=(B, d_b // fw),
        in_specs=[
            pl.BlockSpec((1, S, fw), lambda b, p: (b, 0, cq + p)),
            pl.BlockSpec((1, S, fw), lambda b, p: (b, 0, ck + p)),
            pl.BlockSpec((1, S, fw), lambda b, p: (b, 0, cv + p)),
            pl.BlockSpec((1, S, fw), lambda b, p: (b, 0, cg + p)),
            pl.BlockSpec((1, S, LANES), lambda b, p: (b, 0, 0)),
            pl.BlockSpec((1, fw), lambda b, p: (0, 0)),
            pl.BlockSpec((1, fw), lambda b, p: (0, 0)),
            pl.BlockSpec((LANES, LANES), lambda b, p: (0, 0)),
        ],
        out_specs=pl.BlockSpec((1, S, fw), lambda b, p: (b, 0, p)),
        out_shape=jax.ShapeDtypeStruct((B, S, d_b), BF16),
        scratch_shapes=[
            pltpu.VMEM((nh, S, LANES), BF16),
            pltpu.VMEM((nh * VROWS, S), BF16),
            pltpu.VMEM((2, nh, tq, LANES), BF16),
            pltpu.VMEM((nh, tk, tq), F32),
            pltpu.VMEM((nh, tk, tq), F32),
            pltpu.VMEM((2, nh, 8, tq), F32),
            pltpu.VMEM((2, nh, VROWS, tq), F32),
        ],
        compiler_params=pltpu.CompilerParams(
            dimension_semantics=("parallel", "parallel"), vmem_limit_bytes=VMEM_LIMIT),
        name="fox",
    )(u3, u3, u3, u3, aug, qg2, kg2, g128)


def _out_kernel(x_ref, ya_ref, yb_ref, ga_ref, gb_ref, woa_ref, wob_ref, wo_ref, fg_ref, o_ref,
                woa_s, wob_s, wo_s):
    @pl.when(pl.program_id(0) == 0)
    def _():
        woa_s[...] = _bf(woa_ref[...])
        wob_s[...] = _bf(wob_ref[...])
        wo_s[...] = _bf(wo_ref[...])

    za = _dot(ya_ref[...], woa_s[...])
    zb = _dot(yb_ref[...], wob_s[...])
    merged = (_sigmoid(ga_ref[...].astype(F32)) * za + _sigmoid(gb_ref[...].astype(F32)) * zb)
    o = x_ref[...] + _dot(_bf(merged), wo_s[...])
    ms = jnp.mean(o * o, axis=-1, keepdims=True)
    o_ref[...] = o * lax.rsqrt(ms + RMS_EPS) * fg_ref[...]


def _out(x2, ya2, yb2, u2, woa, wob, wo, fg, *, tm):
    T, D = x2.shape
    assert T % tm == 0
    full = lambda arr: pl.BlockSpec(arr.shape, lambda i: (0,) * arr.ndim)
    return pl.pallas_call(
        _out_kernel,
        grid=(T // tm,),
        in_specs=[
            pl.BlockSpec((tm, D), lambda i: (i, 0)),
            pl.BlockSpec((tm, ya2.shape[1]), lambda i: (i, 0)),
            pl.BlockSpec((tm, yb2.shape[1]), lambda i: (i, 0)),
            pl.BlockSpec((tm, D), lambda i: (i, 0)),
            pl.BlockSpec((tm, D), lambda i: (i, 1)),
            full(woa), full(wob), full(wo), full(fg),
        ],
        out_specs=pl.BlockSpec((tm, D), lambda i: (i, 0)),
        out_shape=jax.ShapeDtypeStruct((T, D), F32),
        scratch_shapes=[pltpu.VMEM(w.shape, BF16) for w in (woa, wob, wo)],
        compiler_params=pltpu.CompilerParams(
            dimension_semantics=("arbitrary",), vmem_limit_bytes=VMEM_LIMIT),
        name="outstage",
    )(x2, ya2, yb2, u2, u2, woa, wob, wo, fg)


def _block_ones(width):
    i = jnp.arange(width) // HEAD
    return (i[:, None] == i[None, :]).astype(BF16)


def _layer(x2, B, S, norm_g, w_in, shift_mu, w_lora_up, w0, a_lora_up, a0, k_k, k_a, r_k,
           lnx_w, lnx_b, f_bias, q_norm_g, k_norm_g, w_out_a, w_out_b, w_out, out_gain):
    T, D = x2.shape
    d_a = w0.shape[0]
    d_b = w_out_b.shape[0]
    rank = w_lora_up.shape[0]
    h_b = f_bias.shape[0]
    rw = 4 * d_a + 2 * rank
    fx = 4 * d_b + h_b

    segs = [(rw + fx, 2 * D), (0, 3 * d_a), (3 * d_a + 2 * rank, d_a), (rw, 4 * d_b),
            (3 * d_a, 2 * rank), (rw + 4 * d_b, h_b)]
    w_perm_t = _regroup_rows(jnp.transpose(w_in), segs, pad_to=LANES, tc=LANES)
    col_r = 2 * D
    col_q = col_r + 4 * d_a
    col_wa = col_q + 4 * d_b

    u2, uf = _inproj(x2, norm_g.reshape(1, D), w_perm_t, tm=2048, tn=1280)
    u3 = u2.reshape(B, S, u2.shape[1])

    mu = shift_mu
    rows = [mu[:d_a], mu[d_a:2 * d_a], mu[2 * d_a:3 * d_a], mu[3 * d_a + 2 * rank:],
            w0, a0, k_k, k_a, r_k.reshape(-1), lnx_w, lnx_b]
    pvec = jnp.stack(rows + [jnp.zeros_like(w0)] * (16 - len(rows)), axis=0)
    mu_wa = mu[3 * d_a:3 * d_a + 2 * rank].reshape(1, 2 * rank)
    z = jnp.zeros((rank, d_a), F32)
    lora = _bf(jnp.concatenate(
        [jnp.concatenate([w_lora_up, z], axis=1), jnp.concatenate([z, a_lora_up], axis=1)], axis=0))
    ya = _rwkv(u3, pvec, mu_wa, lora, _block_ones(LANES), col_r=col_r, col_wa=col_wa, d_a=d_a)

    fb_pad = jnp.pad(f_bias, (0, LANES - h_b)).reshape(1, LANES)
    aug = _fprep(uf.reshape(B, S, LANES), fb_pad, _aug_selector(h_b))
    qg2 = jnp.tile(q_norm_g, FOX_HEADS).reshape(1, FOX_HEADS * HEAD)
    kg2 = jnp.tile(k_norm_g, FOX_HEADS).reshape(1, FOX_HEADS * HEAD)
    yb = _fox(u3, aug, qg2, kg2, _block_ones(LANES), col_q=col_q, d_b=d_b, tq=512, tk=512)

    return _out(x2, ya.reshape(T, d_a), yb.reshape(T, d_b), u2,
                w_out_a, w_out_b, w_out, out_gain.reshape(1, D), tm=512)


def kernel(x, norm_g, w_in, shift_mu, w_lora_up, w0, a_lora_up, a0, k_k, k_a, r_k, lnx_w, lnx_b,
           f_bias, q_norm_g, k_norm_g, w_out_a, w_out_b, w_out, final_norm_g):
    B, S, D = x.shape
    depth = w_in.shape[0]
    assert depth == 1, "the fused output stage applies the final norm after the single layer"
    x2 = x.reshape(B * S, D)
    out = _layer(x2, B, S, norm_g[0], w_in[0], shift_mu[0], w_lora_up[0], w0[0], a_lora_up[0],
                 a0[0], k_k[0], k_a[0], r_k[0], lnx_w[0], lnx_b[0], f_bias[0], q_norm_g[0],
                 k_norm_g[0], w_out_a[0], w_out_b[0], w_out[0], final_norm_g)
    return out.reshape(B, S, D)
```

```python
import functools

import jax
import jax.numpy as jnp
from jax import lax
from jax.experimental import pallas as pl
from jax.experimental.pallas import tpu as pltpu

F32 = jnp.float32
BF16 = jnp.bfloat16

HEAD = 64
LANES = 128
RMS_EPS = 1e-6
LNX_EPS = 64e-5
CHUNK = 64
GROUP = 4
GW = GROUP * HEAD
RWKV_NB = 8
FOX_HEADS = 4
AUG0 = HEAD
NORM_ROWS = 512
OUT_SUB = 512
VROWS = HEAD + 16
LOG2E = 1.4426950408889634
DECAY_SCALE = -0.6065306597126334
VMEM_LIMIT = 56 * 1024 * 1024


def _bf(x):
    return x.astype(BF16)


def _dot(a, b):
    return jnp.dot(a, b, preferred_element_type=F32)


def _dot_nt(a, b):
    return lax.dot_general(a, b, (((1,), (1,)), ((), ())), preferred_element_type=F32)


def _dot_tn(a, b):
    return lax.dot_general(a, b, (((0,), (0,)), ((), ())), preferred_element_type=F32)


def _split2(x):
    hi = _bf(x)
    lo = _bf(x - hi.astype(F32))
    return hi, lo


def _split3(x):
    hi = _bf(x)
    r1 = x - hi.astype(F32)
    mid = _bf(r1)
    lo = _bf(r1 - mid.astype(F32))
    return hi, mid, lo


def _head_sums(xs, g, two_pass):
    m, w = xs[0].shape
    gw = g.shape[0]
    nt = w // gw
    parts = []
    for x, tp in zip(xs, two_pass):
        for piece in (_split2(x) if tp else (_bf(x),)):
            parts += [piece[:, t * gw:(t + 1) * gw] for t in range(nt)]
    r = _dot(jnp.concatenate(parts, axis=0), g)
    tile = lambda i: r[i * m:(i + 1) * m]
    outs, base = [], 0
    for tp in two_pass:
        if tp:
            cols = [tile(base + t) + tile(base + nt + t) for t in range(nt)]
        else:
            cols = [tile(base + t) for t in range(nt)]
        outs.append(jnp.concatenate(cols, axis=1))
        base += (2 if tp else 1) * nt
    return outs


def _sigmoid(x):
    return 1.0 / (1.0 + jnp.exp(-x))


def _regroup_rows_kernel(w_ref, o_ref, *, segs):
    total = sum(width for _, width in segs)
    if total < o_ref.shape[0]:
        t0 = total // 16 * 16
        o_ref[t0:, :] = jnp.zeros((o_ref.shape[0] - t0, o_ref.shape[1]), BF16)
    dst = 0
    for src, width in segs:
        o_ref[dst:dst + width, :] = _bf(w_ref[src:src + width, :])
        dst += width


def _regroup_rows(wt, segs, *, pad_to, tc):
    rows, cols = wt.shape
    n_out = -(-sum(width for _, width in segs) // pad_to) * pad_to
    assert cols % tc == 0 and all(src % 8 == 0 for src, _ in segs)
    return pl.pallas_call(
        functools.partial(_regroup_rows_kernel, segs=tuple(segs)),
        grid=(cols // tc,),
        in_specs=[pl.BlockSpec((rows, tc), lambda i: (0, i))],
        out_specs=pl.BlockSpec((n_out, tc), lambda i: (0, i)),
        out_shape=jax.ShapeDtypeStruct((n_out, cols), BF16),
        compiler_params=pltpu.CompilerParams(
            dimension_semantics=("parallel",), vmem_limit_bytes=VMEM_LIMIT),
        name="wperm",
    )(wt)


def _inproj_kernel(x_ref, g_ref, w_ref, o_ref, of_ref, h_scr, *, n_col_tiles):
    j = pl.program_id(1)

    @pl.when(j == 0)
    def _():
        for c in range(x_ref.shape[0] // NORM_ROWS):
            rows = slice(c * NORM_ROWS, (c + 1) * NORM_ROWS)
            x = x_ref[rows, :]
            ms = jnp.mean(x * x, axis=-1, keepdims=True)
            h = _bf(x * lax.rsqrt(ms + RMS_EPS) * g_ref[...])
            h_scr[rows, :] = h
            o_ref[rows, :] = _bf(_dot_nt(h, w_ref[...]))

    @pl.when(j > 0)
    def _():
        acc = _dot_nt(h_scr[...], w_ref[...])
        o_ref[...] = _bf(acc)

        @pl.when(j == n_col_tiles - 1)
        def _():
            of_ref[...] = acc[:, acc.shape[1] - LANES:]


def _inproj(x2, norm_g, w_perm_t, *, tm, tn):
    T, D = x2.shape
    N = w_perm_t.shape[0]
    nj = N // tn
    assert T % tm == 0 and N % tn == 0 and tm % NORM_ROWS == 0 and nj > 1
    return pl.pallas_call(
        functools.partial(_inproj_kernel, n_col_tiles=nj),
        grid=(T // tm, nj),
        in_specs=[
            pl.BlockSpec((tm, D), lambda i, j: (i, 0)),
            pl.BlockSpec((1, D), lambda i, j: (0, 0)),
            pl.BlockSpec((tn, D), lambda i, j: (j, 0)),
        ],
        out_specs=[
            pl.BlockSpec((tm, tn), lambda i, j: (i, j)),
            pl.BlockSpec((tm, LANES), lambda i, j: (i, 0)),
        ],
        out_shape=[
            jax.ShapeDtypeStruct((T, N), BF16),
            jax.ShapeDtypeStruct((T, LANES), F32),
        ],
        scratch_shapes=[pltpu.VMEM((tm, D), BF16)],
        compiler_params=pltpu.CompilerParams(
            dimension_semantics=("parallel", "arbitrary"), vmem_limit_bytes=VMEM_LIMIT),
        name="inproj",
    )(x2, norm_g, w_perm_t)


def _fprep_kernel(f_ref, fb_ref, sel_ref, aug_ref, *, blk):
    S = f_ref.shape[1]
    row = lax.broadcasted_iota(jnp.int32, (blk, blk), 0)
    col = lax.broadcasted_iota(jnp.int32, (blk, blk), 1)
    tril = _bf(jnp.where(row >= col, 1.0, 0.0))
    carry = jnp.zeros((1, LANES), F32)
    for i in range(S // blk):
        z = f_ref[0, i * blk:(i + 1) * blk, :] + fb_ref[...]
        lf = jnp.minimum(z, 0.0) - jnp.log1p(jnp.exp(-jnp.abs(z)))
        hi, mid, lo = _split3(lf)
        c = (_dot(tril, hi) + _dot(tril, mid) + _dot(tril, lo)) + carry
        pieces = jnp.concatenate(_split3(c * LOG2E), axis=1)
        aug_ref[0, i * blk:(i + 1) * blk, :] = _bf(_dot(pieces, sel_ref[...]))
        carry = c[blk - 1:blk, :]


def _fprep(uf3, fb_pad, sel):
    B, S, _ = uf3.shape
    return pl.pallas_call(
        functools.partial(_fprep_kernel, blk=256),
        grid=(B,),
        in_specs=[
            pl.BlockSpec((1, S, LANES), lambda b: (b, 0, 0)),
            pl.BlockSpec((1, LANES), lambda b: (0, 0)),
            pl.BlockSpec(sel.shape, lambda b: (0, 0)),
        ],
        out_specs=pl.BlockSpec((1, S, LANES), lambda b: (b, 0, 0)),
        out_shape=jax.ShapeDtypeStruct((B, S, LANES), BF16),
        compiler_params=pltpu.CompilerParams(dimension_semantics=("parallel",)),
        name="fprep",
    )(uf3, fb_pad, sel)


def _aug_selector(n_heads):
    assert AUG0 + 6 * n_heads <= LANES
    r = jnp.arange(LANES)[:, None]
    c = jnp.arange(LANES)[None, :]
    blocks = []
    for i in range(3):
        m = jnp.where(c == AUG0 + 6 * r + i, -1.0, jnp.where(c == AUG0 + 6 * r + 3 + i, 1.0, 0.0))
        blocks.append(jnp.where(r < n_heads, m, 0.0))
    return _bf(jnp.concatenate(blocks, axis=0))


_P_MU_R, _P_MU_K, _P_MU_V, _P_MU_G, _P_W0, _P_A0, _P_KK, _P_KA, _P_RK, _P_LW, _P_LB = range(11)


def _shift_mix(x, carry_row, mu):
    rolled = pltpu.roll(x, shift=1, axis=0)
    head = rolled[0:8]
    row = lax.broadcasted_iota(jnp.int32, head.shape, 0)
    prev = jnp.concatenate([jnp.where(row == 0, carry_row, head), rolled[8:]], axis=0)
    return x + (prev - x) * mu


def _block_diag(x, half_masks):
    xb = _bf(x)
    n_tiles = xb.shape[1] // LANES
    zero = jnp.zeros((xb.shape[0], LANES), BF16)
    rows = []
    for h in range(xb.shape[1] // HEAD):
        t = h // 2
        piece = xb[:, t * LANES:(t + 1) * LANES] * half_masks[h % 2]
        rows.append(jnp.concatenate([piece if i == t else zero for i in range(n_tiles)], axis=1))
    return jnp.concatenate(rows, axis=0)


def _rwkv_kernel(r_ref, k_ref, v_ref, g_ref, wa_ref, pv_ref, muwa_ref, lora_ref, gm_ref,
                 o_ref, state_scr, carry_scr):
    c = pl.program_id(1)
    NB = r_ref.shape[0]
    L = r_ref.shape[1]
    DA = r_ref.shape[2]
    n_groups = DA // GW

    @pl.when(c == 0)
    def _():
        state_scr[...] = jnp.zeros_like(state_scr)
        carry_scr[...] = jnp.zeros_like(carry_scr)

    def prm(i):
        return pv_ref[i:i + 1, :]

    gones = gm_ref[...]
    row = lax.broadcasted_iota(jnp.int32, (L, L), 0)
    col = lax.broadcasted_iota(jnp.int32, (L, L), 1)
    tril = _bf(jnp.where(row >= col, 1.0, 0.0))
    lane128 = lax.broadcasted_iota(jnp.int32, (1, LANES), 1)
    half_masks = [_bf(jnp.where(lane128 // HEAD == i, 1.0, 0.0)) for i in range(2)]
    prow = lax.broadcasted_iota(jnp.int32, (L, GW), 0)
    pcol = lax.broadcasted_iota(jnp.int32, (L, GW), 1) & (HEAD - 1)
    incl_p = prow >= pcol
    strict_p = prow > pcol
    eye_p = jnp.where(prow == pcol, 1.0, 0.0)
    st_mask = (lax.broadcasted_iota(jnp.int32, (GW, GW), 0) // HEAD
               == lax.broadcasted_iota(jnp.int32, (GW, GW), 1) // HEAD)
    bd = lambda x: _block_diag(x, half_masks)

    def prep(bb):
        raw = [ref[bb].astype(F32) for ref in (r_ref, k_ref, v_ref, g_ref, wa_ref)]
        mus = [prm(_P_MU_R), prm(_P_MU_K), prm(_P_MU_V), prm(_P_MU_G), muwa_ref[...]]
        mixed = []
        for i, x in enumerate(raw):
            wdt = x.shape[1]
            mixed.append(_shift_mix(x, carry_scr[bb, i:i + 1, 0:wdt], mus[i]))
            carry_scr[bb, i:i + 1, 0:wdt] = x[L - 1:L, :]
        r, k, v, gate, wa = mixed

        wa_act = jnp.where(lane128 < HEAD, jnp.tanh(wa), wa)
        lo = _dot(_bf(wa_act), lora_ref[...])
        ld = DECAY_SCALE * _sigmoid(prm(_P_W0) + lo[:, 0:DA])
        a = _sigmoid(prm(_P_A0) + lo[:, DA:2 * DA])

        kk = k * prm(_P_KK)
        k2 = k * (1.0 + (a - 1.0) * prm(_P_KA))
        ssq, bon = _head_sums([kk * kk, r * k2 * prm(_P_RK)], gones, (False, False))
        kk = kk * jnp.minimum(lax.rsqrt(ssq), 1e12)

        h3, m3, l3 = _split3(ld)
        cum = _dot(tril, h3) + _dot(tril, m3) + _dot(tril, l3)
        e_neg = jnp.exp(-cum)
        w_l = jnp.exp(cum[L - 1:L, :])
        a_t = -kk * jnp.exp(cum - ld)
        r_t = r * jnp.exp(cum)
        b_t = (kk * a) * e_neg
        k_t = k2 * e_neg
        return dict(v=v, gate=gate, bonus=bon * v, w_l=w_l, a_t=a_t, r_t=r_t, b_t=b_t, k_t=k_t)

    def chain(bbs, pre, ys):
        streams = [(i, g) for i in range(len(bbs)) for g in range(n_groups)]
        ns = len(streams)
        gsl = lambda g: slice(g * GW, (g + 1) * GW)
        P = lambda name, st: pre[st[0]][name][:, gsl(st[1])]
        sidx = lambda st: bbs[st[0]] * n_groups + st[1]

        s0 = [state_scr[sidx(st)] for st in streams]
        s0b = [_bf(x) for x in s0]
        ar = [_bf(jnp.concatenate([P("a_t", st), P("r_t", st)], axis=0)) for st in streams]
        bk = [jnp.concatenate([bd(P("b_t", st)), bd(P("k_t", st))], axis=0) for st in streams]
        pm = [_dot_nt(ar[i], bk[i]) for i in range(ns)]
        ars = [_dot_nt(ar[i], s0b[i]) for i in range(ns)]
        a_ab = [jnp.where(strict_p, p[0:L, 0:GW], 0.0) for p in pm]
        a_ak = [jnp.where(strict_p, p[0:L, GW:2 * GW], 0.0) for p in pm]
        a_rb = [jnp.where(incl_p, p[L:2 * L, 0:GW], 0.0) for p in pm]
        a_rk = [jnp.where(incl_p, p[L:2 * L, GW:2 * GW], 0.0) for p in pm]
        vbd = [bd(P("v", st)) for st in streams]
        akv = [_dot(_bf(jnp.concatenate([a_ak[i], a_rk[i]], axis=0)), vbd[i])
               for i in range(ns)]

        pw = [_dot(_bf(x), bd(x)) for x in a_ab]
        tinv = [eye_p + x for x in a_ab]
        n = 2
        while n < L:
            last = 2 * n >= L
            pwb = [bd(x) for x in pw]
            if last:
                tinv = [tinv[i] + _dot(_bf(tinv[i]), pwb[i]) for i in range(ns)]
            else:
                both = [_dot(_bf(jnp.concatenate([pw[i], tinv[i]], axis=0)), pwb[i])
                        for i in range(ns)]
                pw = [x[0:L] for x in both]
                tinv = [tinv[i] + both[i][L:2 * L] for i in range(ns)]
            n *= 2

        u = [_dot(_bf(tinv[i]), bd(ars[i][0:L] + akv[i][0:L])) for i in range(ns)]
        ys.extend(ars[i][L:2 * L] + akv[i][L:2 * L] + _dot(_bf(a_rb[i]), bd(u[i]))
                  for i in range(ns))
        for i, st in enumerate(streams):
            w_l = P("w_l", st)
            uv = _bf(jnp.concatenate([u[i], P("v", st)], axis=0))
            bkh = _bf(jnp.concatenate([P("b_t", st) * w_l, P("k_t", st) * w_l], axis=0))
            state_scr[sidx(st)] = s0[i] * w_l + jnp.where(st_mask, _dot_tn(uv, bkh), 0.0)

    def finish(bbs, pre, ys):
        nb = len(bbs)
        inv_n = 1.0 / HEAD
        y = [jnp.concatenate([ys[i * n_groups + g] for g in range(n_groups)], axis=1)
             for i in range(nb)]
        mean = [m * inv_n for m in _head_sums(y, gones, (True,) * nb)]
        yc = [y[i] - mean[i] for i in range(nb)]
        var = [s * inv_n for s in _head_sums([x * x for x in yc], gones, (False,) * nb)]
        for i, bb in enumerate(bbs):
            yn = yc[i] * lax.rsqrt(var[i] + LNX_EPS) * prm(_P_LW) + prm(_P_LB)
            gate = pre[i]["gate"]
            o_ref[bb] = _bf((yn + pre[i]["bonus"]) * (gate * _sigmoid(gate)))

    bbs = list(range(NB))
    pre = [prep(bb) for bb in bbs]
    ys = []
    chain(bbs, pre, ys)
    finish(bbs, pre, ys)


def _rwkv(u3, pvec, mu_wa, lora, gones, *, col_r, col_wa, d_a):
    B, S, _ = u3.shape
    L = CHUNK
    nb = RWKV_NB
    assert L == HEAD and d_a % GW == 0 and B % nb == 0 and S % L == 0
    cb = col_r // d_a
    blk = lambda off: pl.BlockSpec((nb, L, d_a), lambda b, c, off=off: (b, c, cb + off))
    full = lambda arr: pl.BlockSpec(arr.shape, lambda b, c: (0,) * arr.ndim)
    return pl.pallas_call(
        _rwkv_kernel,
        grid=(B // nb, S // L),
        in_specs=[
            blk(0), blk(1), blk(2), blk(3),
            pl.BlockSpec((nb, L, LANES), lambda b, c: (b, c, col_wa // LANES)),
            full(pvec), full(mu_wa), full(lora), full(gones),
        ],
        out_specs=pl.BlockSpec((nb, L, d_a), lambda b, c: (b, c, 0)),
        out_shape=jax.ShapeDtypeStruct((B, S, d_a), BF16),
        scratch_shapes=[
            pltpu.VMEM((nb * (d_a // GW), GW, GW), F32),
            pltpu.VMEM((nb, 8, d_a), F32),
        ],
        compiler_params=pltpu.CompilerParams(
            dimension_semantics=("parallel", "arbitrary"), vmem_limit_bytes=VMEM_LIMIT),
        name="rwkv7",
    )(u3, u3, u3, u3, u3, pvec, mu_wa, lora, gones)


def _sublane_max(x):
    for s in (4, 2, 1):
        x = jnp.maximum(x, pltpu.roll(x, s, axis=0))
    return x


def _fox_kernel(q_ref, k_ref, v_ref, g_ref, ak_ref, qg_ref, kg_ref, gm_ref,
                o_ref, kp_scr, vt_scr, qp_scr, sta_scr, stb_scr, m_scr, acc_scr, *, tq, tk):
    p = pl.program_id(1)
    S = k_ref.shape[1]
    nh = q_ref.shape[2] // HEAD
    g128 = gm_ref[...]
    lane = lax.broadcasted_iota(jnp.int32, (1, LANES), 1)
    feat = lane < HEAD

    def minus_lanes(hh):
        lo = AUG0 + 6 * (nh * p + hh)
        return (lane >= lo) & (lane < lo + 3)

    def plus_lanes(hh):
        lo = AUG0 + 6 * (nh * p + hh) + 3
        return (lane >= lo) & (lane < lo + 3)

    def head_tile(x, hh):
        t = x[:, (hh // 2) * LANES:(hh // 2 + 1) * LANES]
        return pltpu.roll(t, HEAD, axis=1) if hh % 2 else t

    for i in range(S // tk):
        rows = slice(i * tk, (i + 1) * tk)
        kb = k_ref[0, rows, :].astype(F32)
        (ssq,) = _head_sums([kb * kb], g128, (False,))
        kn = kb * lax.rsqrt(ssq * (1.0 / HEAD) + RMS_EPS) * kg_ref[...]
        aug = ak_ref[0, rows, :].astype(F32)
        for hh in range(nh):
            kp = jnp.where(feat, head_tile(kn, hh), jnp.where(plus_lanes(hh), 1.0, aug))
            kp_scr[hh, rows, :] = _bf(kp)
        vt = _bf(jnp.transpose(v_ref[0, rows, :].astype(F32)))
        for hh in range(nh):
            vt_scr[hh * VROWS:hh * VROWS + HEAD, rows] = vt[hh * HEAD:(hh + 1) * HEAD, :]
            vt_scr[hh * VROWS + HEAD:(hh + 1) * VROWS, rows] = jnp.ones((VROWS - HEAD, tk), BF16)

    def q_side(qi):
        rows = slice(qi * tq, (qi + 1) * tq)
        q = q_ref[0, rows, :].astype(F32)
        (ssq,) = _head_sums([q * q], g128, (False,))
        qn = q * lax.rsqrt(ssq * (1.0 / HEAD) + RMS_EPS) * (qg_ref[...] * (HEAD ** -0.5 * LOG2E))
        augq = ak_ref[0, rows, :].astype(F32)
        for hh in range(nh):
            qp_scr[qi % 2, hh] = _bf(jnp.where(
                feat, head_tile(qn, hh),
                jnp.where(minus_lanes(hh), 1.0, jnp.where(plus_lanes(hh), augq, 0.0))))
        m_scr[qi % 2] = jnp.full(m_scr.shape[1:], -1e30, F32)
        acc_scr[qi % 2] = jnp.zeros(acc_scr.shape[1:], F32)

    def scores(qi, j, st_ref):
        for hh in range(nh):
            st_ref[hh] = _dot_nt(kp_scr[hh, j * tk:(j + 1) * tk, :], qp_scr[qi % 2, hh])

    def softmax_pv(qi, j, st_ref):
        st = [st_ref[hh] for hh in range(nh)]
        if j == qi:
            keyi = lax.broadcasted_iota(jnp.int32, (tk, tq), 0)
            qryi = lax.broadcasted_iota(jnp.int32, (tk, tq), 1)
            st = [jnp.where(qryi >= keyi, x, -1e30) for x in st]
        st = [x.reshape(tk // 8, 8, tq) for x in st]
        m_old = [m_scr[qi % 2, hh] for hh in range(nh)]
        m_new = [jnp.maximum(m_old[hh], _sublane_max(jnp.max(st[hh], axis=0))) for hh in range(nh)]
        alpha = [jnp.exp2(m_old[hh] - m_new[hh]) for hh in range(nh)]
        pt = [_bf(jnp.exp2(st[hh] - m_new[hh][None]).reshape(tk, tq)) for hh in range(nh)]
        pv = [_dot(vt_scr[hh * VROWS:(hh + 1) * VROWS, j * tk:(j + 1) * tk], pt[hh])
              for hh in range(nh)]
        for hh in range(nh):
            m_scr[qi % 2, hh] = m_new[hh]
            acc = acc_scr[qi % 2, hh].reshape(VROWS // 8, 8, tq) * alpha[hh][None]
            acc_scr[qi % 2, hh] = acc.reshape(VROWS, tq) + pv[hh]

    def finish(qi):
        rows = slice(qi * tq, (qi + 1) * tq)
        ot = jnp.concatenate(
            [(acc_scr[qi % 2, hh, 0:HEAD, :].reshape(HEAD // 8, 8, tq)
              * (1.0 / acc_scr[qi % 2, hh, HEAD:HEAD + 8, :])[None]).reshape(HEAD, tq)
             for hh in range(nh)], axis=0)
        g = g_ref[0, rows, :].astype(F32)
        o_ref[0, rows, :] = _bf(jnp.transpose(ot) * (g * _sigmoid(g)))

    pairs = [(qi, j) for qi in range(S // tq) for j in range(qi + 1)]
    bufs = (sta_scr, stb_scr)
    for n, (qi, j) in enumerate(pairs):
        if j == 0:
            q_side(qi)
        scores(qi, j, bufs[n % 2])
        if n > 0:
            pqi, pj = pairs[n - 1]
            softmax_pv(pqi, pj, bufs[(n - 1) % 2])
            if pj == pqi:
                finish(pqi)
    qi, j = pairs[-1]
    softmax_pv(qi, j, bufs[(len(pairs) - 1) % 2])
    finish(qi)


def _fox(u3, aug, qg2, kg2, g128, *, col_q, d_b, tq, tk):
    B, S, _ = u3.shape
    fw = qg2.shape[1]
    nh = fw // HEAD
    assert S % tq == 0 and tq == tk and d_b % fw == 0
    cq, ck, cv, cg = ((col_q + i * d_b) // fw for i in range(4))
    return pl.pallas_call(
        functools.partial(_fox_kernel, tq=tq, tk=tk),
        grid=(B, d_b // fw),
        in_specs=[
            pl.BlockSpec((1, S, fw), lambda b, p: (b, 0, cq + p)),
            pl.BlockSpec((1, S, fw), lambda b, p: (b, 0, ck + p)),
            pl.BlockSpec((1, S, fw), lambda b, p: (b, 0, cv + p)),
            pl.BlockSpec((1, S, fw), lambda b, p: (b, 0, cg + p)),
            pl.BlockSpec((1, S, LANES), lambda b, p: (b, 0, 0)),
            pl.BlockSpec((1, fw), lambda b, p: (0, 0)),
            pl.BlockSpec((1, fw), lambda b, p: (0, 0)),
            pl.BlockSpec((LANES, LANES), lambda b, p: (0, 0)),
        ],
        out_specs=pl.BlockSpec((1, S, fw), lambda b, p: (b, 0, p)),
        out_shape=jax.ShapeDtypeStruct((B, S, d_b), BF16),
        scratch_shapes=[
            pltpu.VMEM((nh, S, LANES), BF16),
            pltpu.VMEM((nh * VROWS, S), BF16),
            pltpu.VMEM((2, nh, tq, LANES), BF16),
            pltpu.VMEM((nh, tk, tq), F32),
            pltpu.VMEM((nh, tk, tq), F32),
            pltpu.VMEM((2, nh, 8, tq), F32),
            pltpu.VMEM((2, nh, VROWS, tq), F32),
        ],
        compiler_params=pltpu.CompilerParams(
            dimension_semantics=("parallel", "parallel"), vmem_limit_bytes=VMEM_LIMIT),
        name="fox",
    )(u3, u3, u3, u3, aug, qg2, kg2, g128)


def _out_kernel(x_ref, ya_ref, yb_ref, ga_ref, gb_ref, woa_ref, wob_ref, wo_ref, fg_ref, o_ref,
                woa_s, wob_s, wo_s):
    @pl.when(pl.program_id(0) == 0)
    def _():
        woa_s[...] = _bf(woa_ref[...])
        wob_s[...] = _bf(wob_ref[...])
        wo_s[...] = _bf(wo_ref[...])

    for s in range(x_ref.shape[0] // OUT_SUB):
        rows = slice(s * OUT_SUB, (s + 1) * OUT_SUB)
        za = _dot(ya_ref[rows, :], woa_s[...])
        zb = _dot(yb_ref[rows, :], wob_s[...])
        merged = (_sigmoid(ga_ref[rows, :].astype(F32)) * za
                  + _sigmoid(gb_ref[rows, :].astype(F32)) * zb)
        o = x_ref[rows, :] + _dot(_bf(merged), wo_s[...])
        ms = jnp.mean(o * o, axis=-1, keepdims=True)
        o_ref[rows, :] = o * lax.rsqrt(ms + RMS_EPS) * fg_ref[...]


def _out(x2, ya2, yb2, u2, woa, wob, wo, fg, *, tm):
    T, D = x2.shape
    assert T % tm == 0 and tm % OUT_SUB == 0
    full = lambda arr: pl.BlockSpec(arr.shape, lambda i: (0,) * arr.ndim)
    return pl.pallas_call(
        _out_kernel,
        grid=(T // tm,),
        in_specs=[
            pl.BlockSpec((tm, D), lambda i: (i, 0)),
            pl.BlockSpec((tm, ya2.shape[1]), lambda i: (i, 0)),
            pl.BlockSpec((tm, yb2.shape[1]), lambda i: (i, 0)),
            pl.BlockSpec((tm, D), lambda i: (i, 0)),
            pl.BlockSpec((tm, D), lambda i: (i, 1)),
            full(woa), full(wob), full(wo), full(fg),
        ],
        out_specs=pl.BlockSpec((tm, D), lambda i: (i, 0)),
        out_shape=jax.ShapeDtypeStruct((T, D), F32),
        scratch_shapes=[pltpu.VMEM(w.shape, BF16) for w in (woa, wob, wo)],
        compiler_params=pltpu.CompilerParams(
            dimension_semantics=("arbitrary",), vmem_limit_bytes=VMEM_LIMIT),
        name="outstage",
    )(x2, ya2, yb2, u2, u2, woa, wob, wo, fg)


def _block_ones(width):
    i = jnp.arange(width) // HEAD
    return (i[:, None] == i[None, :]).astype(BF16)


def _layer(x2, B, S, norm_g, w_in, shift_mu, w_lora_up, w0, a_lora_up, a0, k_k, k_a, r_k,
           lnx_w, lnx_b, f_bias, q_norm_g, k_norm_g, w_out_a, w_out_b, w_out, out_gain):
    T, D = x2.shape
    d_a = w0.shape[0]
    d_b = w_out_b.shape[0]
    rank = w_lora_up.shape[0]
    h_b = f_bias.shape[0]
    rw = 4 * d_a + 2 * rank
    fx = 4 * d_b + h_b

    segs = [(rw + fx, 2 * D), (0, 3 * d_a), (3 * d_a + 2 * rank, d_a), (rw, 4 * d_b),
            (3 * d_a, 2 * rank), (rw + 4 * d_b, h_b)]
    w_perm_t = _regroup_rows(jnp.transpose(w_in), segs, pad_to=LANES, tc=LANES)
    col_r = 2 * D
    col_q = col_r + 4 * d_a
    col_wa = col_q + 4 * d_b

    u2, uf = _inproj(x2, norm_g.reshape(1, D), w_perm_t, tm=2048, tn=1280)
    u3 = u2.reshape(B, S, u2.shape[1])

    mu = shift_mu
    rows = [mu[:d_a], mu[d_a:2 * d_a], mu[2 * d_a:3 * d_a], mu[3 * d_a + 2 * rank:],
            w0, a0, k_k, k_a, r_k.reshape(-1), lnx_w, lnx_b]
    pvec = jnp.stack(rows + [jnp.zeros_like(w0)] * (16 - len(rows)), axis=0)
    mu_wa = mu[3 * d_a:3 * d_a + 2 * rank].reshape(1, 2 * rank)
    z = jnp.zeros((rank, d_a), F32)
    lora = _bf(jnp.concatenate(
        [jnp.concatenate([w_lora_up, z], axis=1), jnp.concatenate([z, a_lora_up], axis=1)], axis=0))
    ya = _rwkv(u3, pvec, mu_wa, lora, _block_ones(LANES), col_r=col_r, col_wa=col_wa, d_a=d_a)

    fb_pad = jnp.pad(f_bias, (0, LANES - h_b)).reshape(1, LANES)
    aug = _fprep(uf.reshape(B, S, LANES), fb_pad, _aug_selector(h_b))
    qg2 = jnp.tile(q_norm_g, FOX_HEADS).reshape(1, FOX_HEADS * HEAD)
    kg2 = jnp.tile(k_norm_g, FOX_HEADS).reshape(1, FOX_HEADS * HEAD)
    yb = _fox(u3, aug, qg2, kg2, _block_ones(LANES), col_q=col_q, d_b=d_b, tq=512, tk=512)

    return _out(x2, ya.reshape(T, d_a), yb.reshape(T, d_b), u2,
                w_out_a, w_out_b, w_out, out_gain.reshape(1, D), tm=1024)


def kernel(x, norm_g, w_in, shift_mu, w_lora_up, w0, a_lora_up, a0, k_k, k_a, r_k, lnx_w, lnx_b,
           f_bias, q_norm_g, k_norm_g, w_out_a, w_out_b, w_out, final_norm_g):
    B, S, D = x.shape
    depth = w_in.shape[0]
    assert depth == 1, "the fused output stage applies the final norm after the single layer"
    x2 = x.reshape(B * S, D)
    out = _layer(x2, B, S, norm_g[0], w_in[0], shift_mu[0], w_lora_up[0], w0[0], a_lora_up[0],
                 a0[0], k_k[0], k_a[0], r_k[0], lnx_w[0], lnx_b[0], f_bias[0], q_norm_g[0],
                 k_norm_g[0], w_out_a[0], w_out_b[0], w_out[0], final_norm_g)
    return out.reshape(B, S, D)
```

```python
import functools

import jax
import jax.numpy as jnp
from jax import lax
from jax.experimental import pallas as pl
from jax.experimental.pallas import tpu as pltpu

F32 = jnp.float32
BF16 = jnp.bfloat16

HEAD = 64
LANES = 128
RMS_EPS = 1e-6
LNX_EPS = 64e-5
CHUNK = 64
GROUP = 4
GW = GROUP * HEAD
RWKV_NB = 8
FOX_HEADS = 4
AUG0 = HEAD
NORM_ROWS = 512
OUT_SUB = 512
VROWS = HEAD + 16
LOG2E = 1.4426950408889634
DECAY_SCALE = -0.6065306597126334
VMEM_LIMIT = 56 * 1024 * 1024


def _bf(x):
    return x.astype(BF16)


def _dot(a, b):
    return jnp.dot(a, b, preferred_element_type=F32)


def _dot_nt(a, b):
    return lax.dot_general(a, b, (((1,), (1,)), ((), ())), preferred_element_type=F32)


def _dot_tn(a, b):
    return lax.dot_general(a, b, (((0,), (0,)), ((), ())), preferred_element_type=F32)


def _split2(x):
    hi = _bf(x)
    lo = _bf(x - hi.astype(F32))
    return hi, lo


def _split3(x):
    hi = _bf(x)
    r1 = x - hi.astype(F32)
    mid = _bf(r1)
    lo = _bf(r1 - mid.astype(F32))
    return hi, mid, lo


def _head_sums(xs, g, two_pass):
    m, w = xs[0].shape
    gw = g.shape[0]
    nt = w // gw
    parts = []
    for x, tp in zip(xs, two_pass):
        for piece in (_split2(x) if tp else (_bf(x),)):
            parts += [piece[:, t * gw:(t + 1) * gw] for t in range(nt)]
    r = _dot(jnp.concatenate(parts, axis=0), g)
    tile = lambda i: r[i * m:(i + 1) * m]
    outs, base = [], 0
    for tp in two_pass:
        if tp:
            cols = [tile(base + t) + tile(base + nt + t) for t in range(nt)]
        else:
            cols = [tile(base + t) for t in range(nt)]
        outs.append(jnp.concatenate(cols, axis=1))
        base += (2 if tp else 1) * nt
    return outs


def _sigmoid(x):
    return 1.0 / (1.0 + jnp.exp2(x * (-LOG2E)))


def _regroup_rows_kernel(w_ref, o_ref, *, segs):
    total = sum(width for _, width in segs)
    if total < o_ref.shape[0]:
        t0 = total // 16 * 16
        o_ref[t0:, :] = jnp.zeros((o_ref.shape[0] - t0, o_ref.shape[1]), BF16)
    dst = 0
    for src, width in segs:
        o_ref[dst:dst + width, :] = _bf(w_ref[src:src + width, :])
        dst += width


def _regroup_rows(wt, segs, *, pad_to, tc):
    rows, cols = wt.shape
    n_out = -(-sum(width for _, width in segs) // pad_to) * pad_to
    assert cols % tc == 0 and all(src % 8 == 0 for src, _ in segs)
    return pl.pallas_call(
        functools.partial(_regroup_rows_kernel, segs=tuple(segs)),
        grid=(cols // tc,),
        in_specs=[pl.BlockSpec((rows, tc), lambda i: (0, i))],
        out_specs=pl.BlockSpec((n_out, tc), lambda i: (0, i)),
        out_shape=jax.ShapeDtypeStruct((n_out, cols), BF16),
        compiler_params=pltpu.CompilerParams(
            dimension_semantics=("parallel",), vmem_limit_bytes=VMEM_LIMIT),
        name="wperm",
    )(wt)


def _inproj_kernel(x_ref, g_ref, w_ref, o_ref, of_ref, h_scr, *, n_col_tiles):
    j = pl.program_id(1)

    @pl.when(j == 0)
    def _():
        for c in range(x_ref.shape[0] // NORM_ROWS):
            rows = slice(c * NORM_ROWS, (c + 1) * NORM_ROWS)
            x = x_ref[rows, :]
            ms = jnp.mean(x * x, axis=-1, keepdims=True)
            h = _bf(x * lax.rsqrt(ms + RMS_EPS) * g_ref[...])
            h_scr[rows, :] = h
            o_ref[rows, :] = _bf(_dot_nt(h, w_ref[...]))

    @pl.when(j > 0)
    def _():
        acc = _dot_nt(h_scr[...], w_ref[...])
        o_ref[...] = _bf(acc)

        @pl.when(j == n_col_tiles - 1)
        def _():
            of_ref[...] = acc[:, acc.shape[1] - LANES:]


def _inproj(x2, norm_g, w_perm_t, *, tm, tn):
    T, D = x2.shape
    N = w_perm_t.shape[0]
    nj = N // tn
    assert T % tm == 0 and N % tn == 0 and tm % NORM_ROWS == 0 and nj > 1
    return pl.pallas_call(
        functools.partial(_inproj_kernel, n_col_tiles=nj),
        grid=(T // tm, nj),
        in_specs=[
            pl.BlockSpec((tm, D), lambda i, j: (i, 0)),
            pl.BlockSpec((1, D), lambda i, j: (0, 0)),
            pl.BlockSpec((tn, D), lambda i, j: (j, 0)),
        ],
        out_specs=[
            pl.BlockSpec((tm, tn), lambda i, j: (i, j)),
            pl.BlockSpec((tm, LANES), lambda i, j: (i, 0)),
        ],
        out_shape=[
            jax.ShapeDtypeStruct((T, N), BF16),
            jax.ShapeDtypeStruct((T, LANES), F32),
        ],
        scratch_shapes=[pltpu.VMEM((tm, D), BF16)],
        compiler_params=pltpu.CompilerParams(
            dimension_semantics=("parallel", "arbitrary"), vmem_limit_bytes=VMEM_LIMIT),
        name="inproj",
    )(x2, norm_g, w_perm_t)


def _fprep_kernel(f_ref, fb_ref, sel_ref, aug_ref, *, blk):
    S = f_ref.shape[1]
    row = lax.broadcasted_iota(jnp.int32, (blk, blk), 0)
    col = lax.broadcasted_iota(jnp.int32, (blk, blk), 1)
    tril = _bf(jnp.where(row >= col, 1.0, 0.0))
    local = []
    for i in range(S // blk):
        z = f_ref[0, i * blk:(i + 1) * blk, :] + fb_ref[...]
        lf = jnp.minimum(z, 0.0) - jnp.log1p(jnp.exp(-jnp.abs(z)))
        hi, mid, lo = _split3(lf)
        local.append(_dot(tril, hi) + _dot(tril, mid) + _dot(tril, lo))
    carry = jnp.zeros((1, LANES), F32)
    cums = []
    for x in local:
        cums.append(x + carry)
        carry = cums[-1][blk - 1:blk, :]
    for i, c in enumerate(cums):
        pieces = jnp.concatenate(_split3(c * LOG2E), axis=1)
        aug_ref[0, i * blk:(i + 1) * blk, :] = _bf(_dot(pieces, sel_ref[...]))


def _fprep(uf3, fb_pad, sel):
    B, S, _ = uf3.shape
    return pl.pallas_call(
        functools.partial(_fprep_kernel, blk=256),
        grid=(B,),
        in_specs=[
            pl.BlockSpec((1, S, LANES), lambda b: (b, 0, 0)),
            pl.BlockSpec((1, LANES), lambda b: (0, 0)),
            pl.BlockSpec(sel.shape, lambda b: (0, 0)),
        ],
        out_specs=pl.BlockSpec((1, S, LANES), lambda b: (b, 0, 0)),
        out_shape=jax.ShapeDtypeStruct((B, S, LANES), BF16),
        compiler_params=pltpu.CompilerParams(dimension_semantics=("parallel",)),
        name="fprep",
    )(uf3, fb_pad, sel)


def _aug_selector(n_heads):
    assert AUG0 + 6 * n_heads <= LANES
    r = jnp.arange(LANES)[:, None]
    c = jnp.arange(LANES)[None, :]
    blocks = []
    for i in range(3):
        m = jnp.where(c == AUG0 + 6 * r + i, -1.0, jnp.where(c == AUG0 + 6 * r + 3 + i, 1.0, 0.0))
        blocks.append(jnp.where(r < n_heads, m, 0.0))
    return _bf(jnp.concatenate(blocks, axis=0))


_P_MU_R, _P_MU_K, _P_MU_V, _P_MU_G, _P_W0, _P_A0, _P_KK, _P_KA, _P_RK, _P_LW, _P_LB = range(11)


def _shift_mix(x, carry_row, mu):
    rolled = pltpu.roll(x, shift=1, axis=0)
    head = rolled[0:8]
    row = lax.broadcasted_iota(jnp.int32, head.shape, 0)
    prev = jnp.concatenate([jnp.where(row == 0, carry_row, head), rolled[8:]], axis=0)
    return x + (prev - x) * mu


def _block_diag(x, half_masks):
    xb = _bf(x)
    n_tiles = xb.shape[1] // LANES
    zero = jnp.zeros((xb.shape[0], LANES), BF16)
    rows = []
    for h in range(xb.shape[1] // HEAD):
        t = h // 2
        piece = xb[:, t * LANES:(t + 1) * LANES] * half_masks[h % 2]
        rows.append(jnp.concatenate([piece if i == t else zero for i in range(n_tiles)], axis=1))
    return jnp.concatenate(rows, axis=0)


def _rwkv_kernel(r_ref, k_ref, v_ref, g_ref, wa_ref, pv_ref, muwa_ref, lora_ref, gm_ref,
                 o_ref, state_scr, carry_scr):
    c = pl.program_id(1)
    NB = r_ref.shape[0]
    L = r_ref.shape[1]
    DA = r_ref.shape[2]
    n_groups = DA // GW

    @pl.when(c == 0)
    def _():
        state_scr[...] = jnp.zeros_like(state_scr)
        carry_scr[...] = jnp.zeros_like(carry_scr)

    def prm(i):
        return pv_ref[i:i + 1, :]

    gones = gm_ref[...]
    row = lax.broadcasted_iota(jnp.int32, (L, L), 0)
    col = lax.broadcasted_iota(jnp.int32, (L, L), 1)
    tril = _bf(jnp.where(row >= col, 1.0, 0.0))
    lane128 = lax.broadcasted_iota(jnp.int32, (1, LANES), 1)
    half_masks = [_bf(jnp.where(lane128 // HEAD == i, 1.0, 0.0)) for i in range(2)]
    prow = lax.broadcasted_iota(jnp.int32, (L, GW), 0)
    pcol = lax.broadcasted_iota(jnp.int32, (L, GW), 1) & (HEAD - 1)
    incl_p = prow >= pcol
    strict_p = prow > pcol
    eye_p = jnp.where(prow == pcol, 1.0, 0.0)
    st_mask = (lax.broadcasted_iota(jnp.int32, (GW, GW), 0) // HEAD
               == lax.broadcasted_iota(jnp.int32, (GW, GW), 1) // HEAD)
    bd = lambda x: _block_diag(x, half_masks)

    def prep(bb):
        raw = [ref[bb].astype(F32) for ref in (r_ref, k_ref, v_ref, g_ref, wa_ref)]
        mus = [prm(_P_MU_R), prm(_P_MU_K), prm(_P_MU_V), prm(_P_MU_G), muwa_ref[...]]
        mixed = []
        for i, x in enumerate(raw):
            wdt = x.shape[1]
            mixed.append(_shift_mix(x, carry_scr[bb, i:i + 1, 0:wdt], mus[i]))
            carry_scr[bb, i:i + 1, 0:wdt] = x[L - 1:L, :]
        r, k, v, gate, wa = mixed

        wa_act = jnp.where(lane128 < HEAD, jnp.tanh(wa), wa)
        lo = _dot(_bf(wa_act), lora_ref[...])
        ld = (DECAY_SCALE * LOG2E) * _sigmoid(prm(_P_W0) + lo[:, 0:DA])
        a = _sigmoid(prm(_P_A0) + lo[:, DA:2 * DA])

        kk = k * prm(_P_KK)
        k2 = k * (1.0 + (a - 1.0) * prm(_P_KA))
        ssq, bon = _head_sums([kk * kk, r * k2 * prm(_P_RK)], gones, (False, False))
        kk = kk * jnp.minimum(lax.rsqrt(ssq), 1e12)

        h3, m3, l3 = _split3(ld)
        cum = _dot(tril, h3) + _dot(tril, m3) + _dot(tril, l3)
        e_neg = jnp.exp2(-cum)
        w_l = jnp.exp2(cum[L - 1:L, :])
        a_t = -kk * jnp.exp2(cum - ld)
        r_t = r * jnp.exp2(cum)
        b_t = (kk * a) * e_neg
        k_t = k2 * e_neg
        return dict(v=v, gate=gate, bonus=bon * v, w_l=w_l, a_t=a_t, r_t=r_t, b_t=b_t, k_t=k_t)

    def chain(bbs, pre, ys):
        streams = [(i, g) for i in range(len(bbs)) for g in range(n_groups)]
        ns = len(streams)
        gsl = lambda g: slice(g * GW, (g + 1) * GW)
        P = lambda name, st: pre[st[0]][name][:, gsl(st[1])]
        sidx = lambda st: bbs[st[0]] * n_groups + st[1]

        s0 = [state_scr[sidx(st)] for st in streams]
        s0b = [_bf(x) for x in s0]
        ar = [_bf(jnp.concatenate([P("a_t", st), P("r_t", st)], axis=0)) for st in streams]
        bk = [jnp.concatenate([bd(P("b_t", st)), bd(P("k_t", st))], axis=0) for st in streams]
        pm = [_dot_nt(ar[i], bk[i]) for i in range(ns)]
        ars = [_dot_nt(ar[i], s0b[i]) for i in range(ns)]
        a_ab = [jnp.where(strict_p, p[0:L, 0:GW], 0.0) for p in pm]
        a_ak = [jnp.where(strict_p, p[0:L, GW:2 * GW], 0.0) for p in pm]
        a_rb = [jnp.where(incl_p, p[L:2 * L, 0:GW], 0.0) for p in pm]
        a_rk = [jnp.where(incl_p, p[L:2 * L, GW:2 * GW], 0.0) for p in pm]
        vbd = [bd(P("v", st)) for st in streams]
        akv = [_dot(_bf(jnp.concatenate([a_ak[i], a_rk[i]], axis=0)), vbd[i])
               for i in range(ns)]

        pw = [_dot(_bf(x), bd(x)) for x in a_ab]
        tinv = [eye_p + x for x in a_ab]
        n = 2
        while n < L:
            last = 2 * n >= L
            pwb = [bd(x) for x in pw]
            if last:
                tinv = [tinv[i] + _dot(_bf(tinv[i]), pwb[i]) for i in range(ns)]
            else:
                both = [_dot(_bf(jnp.concatenate([pw[i], tinv[i]], axis=0)), pwb[i])
                        for i in range(ns)]
                pw = [x[0:L] for x in both]
                tinv = [tinv[i] + both[i][L:2 * L] for i in range(ns)]
            n *= 2

        u = [_dot(_bf(tinv[i]), bd(ars[i][0:L] + akv[i][0:L])) for i in range(ns)]
        ys.extend(ars[i][L:2 * L] + akv[i][L:2 * L] + _dot(_bf(a_rb[i]), bd(u[i]))
                  for i in range(ns))
        for i, st in enumerate(streams):
            w_l = P("w_l", st)
            uv = _bf(jnp.concatenate([u[i], P("v", st)], axis=0))
            bkh = _bf(jnp.concatenate([P("b_t", st) * w_l, P("k_t", st) * w_l], axis=0))
            state_scr[sidx(st)] = s0[i] * w_l + jnp.where(st_mask, _dot_tn(uv, bkh), 0.0)

    def finish(bbs, pre, ys):
        nb = len(bbs)
        inv_n = 1.0 / HEAD
        y = [jnp.concatenate([ys[i * n_groups + g] for g in range(n_groups)], axis=1)
             for i in range(nb)]
        mean = [m * inv_n for m in _head_sums(y, gones, (True,) * nb)]
        yc = [y[i] - mean[i] for i in range(nb)]
        var = [s * inv_n for s in _head_sums([x * x for x in yc], gones, (False,) * nb)]
        for i, bb in enumerate(bbs):
            yn = yc[i] * lax.rsqrt(var[i] + LNX_EPS) * prm(_P_LW) + prm(_P_LB)
            gate = pre[i]["gate"]
            o_ref[bb] = _bf((yn + pre[i]["bonus"]) * (gate * _sigmoid(gate)))

    bbs = list(range(NB))
    pre = [prep(bb) for bb in bbs]
    ys = []
    chain(bbs, pre, ys)
    finish(bbs, pre, ys)


def _rwkv(u3, pvec, mu_wa, lora, gones, *, col_r, col_wa, d_a):
    B, S, _ = u3.shape
    L = CHUNK
    nb = RWKV_NB
    assert L == HEAD and d_a % GW == 0 and B % nb == 0 and S % L == 0
    cb = col_r // d_a
    blk = lambda off: pl.BlockSpec((nb, L, d_a), lambda b, c, off=off: (b, c, cb + off))
    full = lambda arr: pl.BlockSpec(arr.shape, lambda b, c: (0,) * arr.ndim)
    return pl.pallas_call(
        _rwkv_kernel,
        grid=(B // nb, S // L),
        in_specs=[
            blk(0), blk(1), blk(2), blk(3),
            pl.BlockSpec((nb, L, LANES), lambda b, c: (b, c, col_wa // LANES)),
            full(pvec), full(mu_wa), full(lora), full(gones),
        ],
        out_specs=pl.BlockSpec((nb, L, d_a), lambda b, c: (b, c, 0)),
        out_shape=jax.ShapeDtypeStruct((B, S, d_a), BF16),
        scratch_shapes=[
            pltpu.VMEM((nb * (d_a // GW), GW, GW), F32),
            pltpu.VMEM((nb, 8, d_a), F32),
        ],
        compiler_params=pltpu.CompilerParams(
            dimension_semantics=("parallel", "arbitrary"), vmem_limit_bytes=VMEM_LIMIT),
        name="rwkv7",
    )(u3, u3, u3, u3, u3, pvec, mu_wa, lora, gones)


def _sublane_max(x):
    for s in (4, 2, 1):
        x = jnp.maximum(x, pltpu.roll(x, s, axis=0))
    return x


def _fox_kernel(q_ref, k_ref, v_ref, g_ref, ak_ref, qg_ref, kg_ref, gm_ref,
                o_ref, kp_scr, vt_scr, qp_scr, sta_scr, stb_scr, m_scr, acc_scr, *, tq, tk):
    p = pl.program_id(1)
    S = k_ref.shape[1]
    nh = q_ref.shape[2] // HEAD
    g128 = gm_ref[...]
    lane = lax.broadcasted_iota(jnp.int32, (1, LANES), 1)
    feat = lane < HEAD

    def minus_lanes(hh):
        lo = AUG0 + 6 * (nh * p + hh)
        return (lane >= lo) & (lane < lo + 3)

    def plus_lanes(hh):
        lo = AUG0 + 6 * (nh * p + hh) + 3
        return (lane >= lo) & (lane < lo + 3)

    def head_tile(x, hh):
        t = x[:, (hh // 2) * LANES:(hh // 2 + 1) * LANES]
        return pltpu.roll(t, HEAD, axis=1) if hh % 2 else t

    for i in range(S // tk):
        rows = slice(i * tk, (i + 1) * tk)
        kb = k_ref[0, rows, :].astype(F32)
        (ssq,) = _head_sums([kb * kb], g128, (False,))
        kn = kb * lax.rsqrt(ssq * (1.0 / HEAD) + RMS_EPS) * kg_ref[...]
        aug = ak_ref[0, rows, :].astype(F32)
        for hh in range(nh):
            kp = jnp.where(feat, head_tile(kn, hh), jnp.where(plus_lanes(hh), 1.0, aug))
            kp_scr[hh, rows, :] = _bf(kp)
        vt = _bf(jnp.transpose(v_ref[0, rows, :].astype(F32)))
        for hh in range(nh):
            vt_scr[hh * VROWS:hh * VROWS + HEAD, rows] = vt[hh * HEAD:(hh + 1) * HEAD, :]
            vt_scr[hh * VROWS + HEAD:(hh + 1) * VROWS, rows] = jnp.ones((VROWS - HEAD, tk), BF16)

    def q_side(qi):
        rows = slice(qi * tq, (qi + 1) * tq)
        q = q_ref[0, rows, :].astype(F32)
        (ssq,) = _head_sums([q * q], g128, (False,))
        qn = q * lax.rsqrt(ssq * (1.0 / HEAD) + RMS_EPS) * (qg_ref[...] * (HEAD ** -0.5 * LOG2E))
        augq = ak_ref[0, rows, :].astype(F32)
        for hh in range(nh):
            qp_scr[qi % 2, hh] = _bf(jnp.where(
                feat, head_tile(qn, hh),
                jnp.where(minus_lanes(hh), 1.0, jnp.where(plus_lanes(hh), augq, 0.0))))
        m_scr[qi % 2] = jnp.full(m_scr.shape[1:], -1e30, F32)
        acc_scr[qi % 2] = jnp.zeros(acc_scr.shape[1:], F32)

    def scores(qi, j, st_ref):
        for hh in range(nh):
            st_ref[hh] = _dot_nt(kp_scr[hh, j * tk:(j + 1) * tk, :], qp_scr[qi % 2, hh])

    def softmax_pv(qi, j, st_ref):
        st = [st_ref[hh] for hh in range(nh)]
        if j == qi:
            keyi = lax.broadcasted_iota(jnp.int32, (tk, tq), 0)
            qryi = lax.broadcasted_iota(jnp.int32, (tk, tq), 1)
            st = [jnp.where(qryi >= keyi, x, -1e30) for x in st]
        st = [x.reshape(tk // 8, 8, tq) for x in st]
        m_old = [m_scr[qi % 2, hh] for hh in range(nh)]
        m_new = [jnp.maximum(m_old[hh], _sublane_max(jnp.max(st[hh], axis=0))) for hh in range(nh)]
        alpha = [jnp.exp2(m_old[hh] - m_new[hh]) for hh in range(nh)]
        pt = [_bf(jnp.exp2(st[hh] - m_new[hh][None]).reshape(tk, tq)) for hh in range(nh)]
        pv = [_dot(vt_scr[hh * VROWS:(hh + 1) * VROWS, j * tk:(j + 1) * tk], pt[hh])
              for hh in range(nh)]
        for hh in range(nh):
            m_scr[qi % 2, hh] = m_new[hh]
            acc = acc_scr[qi % 2, hh].reshape(VROWS // 8, 8, tq) * alpha[hh][None]
            acc_scr[qi % 2, hh] = acc.reshape(VROWS, tq) + pv[hh]

    def finish(qi):
        rows = slice(qi * tq, (qi + 1) * tq)
        ot = jnp.concatenate(
            [(acc_scr[qi % 2, hh, 0:HEAD, :].reshape(HEAD // 8, 8, tq)
              * (1.0 / acc_scr[qi % 2, hh, HEAD:HEAD + 8, :])[None]).reshape(HEAD, tq)
             for hh in range(nh)], axis=0)
        g = g_ref[0, rows, :].astype(F32)
        o_ref[0, rows, :] = _bf(jnp.transpose(ot) * (g * _sigmoid(g)))

    pairs = [(qi, j) for qi in range(S // tq) for j in range(qi + 1)]
    bufs = (sta_scr, stb_scr)
    for n, (qi, j) in enumerate(pairs):
        if j == 0:
            q_side(qi)
        scores(qi, j, bufs[n % 2])
        if n > 0:
            pqi, pj = pairs[n - 1]
            softmax_pv(pqi, pj, bufs[(n - 1) % 2])
            if pj == pqi:
                finish(pqi)
    qi, j = pairs[-1]
    softmax_pv(qi, j, bufs[(len(pairs) - 1) % 2])
    finish(qi)


def _fox(u3, aug, qg2, kg2, g128, *, col_q, d_b, tq, tk):
    B, S, _ = u3.shape
    fw = qg2.shape[1]
    nh = fw // HEAD
    assert S % tq == 0 and tq == tk and d_b % fw == 0
    cq, ck, cv, cg = ((col_q + i * d_b) // fw for i in range(4))
    return pl.pallas_call(
        functools.partial(_fox_kernel, tq=tq, tk=tk),
        grid=(B, d_b // fw),
        in_specs=[
            pl.BlockSpec((1, S, fw), lambda b, p: (b, 0, cq + p)),
            pl.BlockSpec((1, S, fw), lambda b, p: (b, 0, ck + p)),
            pl.BlockSpec((1, S, fw), lambda b, p: (b, 0, cv + p)),
            pl.BlockSpec((1, S, fw), lambda b, p: (b, 0, cg + p)),
            pl.BlockSpec((1, S, LANES), lambda b, p: (b, 0, 0)),
            pl.BlockSpec((1, fw), lambda b, p: (0, 0)),
            pl.BlockSpec((1, fw), lambda b, p: (0, 0)),
            pl.BlockSpec((LANES, LANES), lambda b, p: (0, 0)),
        ],
        out_specs=pl.BlockSpec((1, S, fw), lambda b, p: (b, 0, p)),
        out_shape=jax.ShapeDtypeStruct((B, S, d_b), BF16),
        scratch_shapes=[
            pltpu.VMEM((nh, S, LANES), BF16),
            pltpu.VMEM((nh * VROWS, S), BF16),
            pltpu.VMEM((2, nh, tq, LANES), BF16),
            pltpu.VMEM((nh, tk, tq), F32),
            pltpu.VMEM((nh, tk, tq), F32),
            pltpu.VMEM((2, nh, 8, tq), F32),
            pltpu.VMEM((2, nh, VROWS, tq), F32),
        ],
        compiler_params=pltpu.CompilerParams(
            dimension_semantics=("parallel", "parallel"), vmem_limit_bytes=VMEM_LIMIT),
        name="fox",
    )(u3, u3, u3, u3, aug, qg2, kg2, g128)


def _out_kernel(x_ref, ya_ref, yb_ref, ga_ref, gb_ref, woa_ref, wob_ref, wo_ref, fg_ref, o_ref,
                woa_s, wob_s, wo_s):
    @pl.when(pl.program_id(0) == 0)
    def _():
        woa_s[...] = _bf(woa_ref[...])
        wob_s[...] = _bf(wob_ref[...])
        wo_s[...] = _bf(wo_ref[...])

    for s in range(x_ref.shape[0] // OUT_SUB):
        rows = slice(s * OUT_SUB, (s + 1) * OUT_SUB)
        za = _dot(ya_ref[rows, :], woa_s[...])
        zb = _dot(yb_ref[rows, :], wob_s[...])
        merged = (_sigmoid(ga_ref[rows, :].astype(F32)) * za
                  + _sigmoid(gb_ref[rows, :].astype(F32)) * zb)
        o = x_ref[rows, :] + _dot(_bf(merged), wo_s[...])
        ms = jnp.mean(o * o, axis=-1, keepdims=True)
        o_ref[rows, :] = o * lax.rsqrt(ms + RMS_EPS) * fg_ref[...]


def _out(x2, ya2, yb2, u2, woa, wob, wo, fg, *, tm):
    T, D = x2.shape
    assert T % tm == 0 and tm % OUT_SUB == 0
    full = lambda arr: pl.BlockSpec(arr.shape, lambda i: (0,) * arr.ndim)
    return pl.pallas_call(
        _out_kernel,
        grid=(T // tm,),
        in_specs=[
            pl.BlockSpec((tm, D), lambda i: (i, 0)),
            pl.BlockSpec((tm, ya2.shape[1]), lambda i: (i, 0)),
            pl.BlockSpec((tm, yb2.shape[1]), lambda i: (i, 0)),
            pl.BlockSpec((tm, D), lambda i: (i, 0)),
            pl.BlockSpec((tm, D), lambda i: (i, 1)),
            full(woa), full(wob), full(wo), full(fg),
        ],
        out_specs=pl.BlockSpec((tm, D), lambda i: (i, 0)),
        out_shape=jax.ShapeDtypeStruct((T, D), F32),
        scratch_shapes=[pltpu.VMEM(w.shape, BF16) for w in (woa, wob, wo)],
        compiler_params=pltpu.CompilerParams(
            dimension_semantics=("arbitrary",), vmem_limit_bytes=VMEM_LIMIT),
        name="outstage",
    )(x2, ya2, yb2, u2, u2, woa, wob, wo, fg)


def _block_ones(width):
    i = jnp.arange(width) // HEAD
    return (i[:, None] == i[None, :]).astype(BF16)


def _layer(x2, B, S, norm_g, w_in, shift_mu, w_lora_up, w0, a_lora_up, a0, k_k, k_a, r_k,
           lnx_w, lnx_b, f_bias, q_norm_g, k_norm_g, w_out_a, w_out_b, w_out, out_gain):
    T, D = x2.shape
    d_a = w0.shape[0]
    d_b = w_out_b.shape[0]
    rank = w_lora_up.shape[0]
    h_b = f_bias.shape[0]
    rw = 4 * d_a + 2 * rank
    fx = 4 * d_b + h_b

    segs = [(rw + fx, 2 * D), (0, 3 * d_a), (3 * d_a + 2 * rank, d_a), (rw, 4 * d_b),
            (3 * d_a, 2 * rank), (rw + 4 * d_b, h_b)]
    w_perm_t = _regroup_rows(jnp.transpose(w_in), segs, pad_to=LANES, tc=LANES)
    col_r = 2 * D
    col_q = col_r + 4 * d_a
    col_wa = col_q + 4 * d_b

    u2, uf = _inproj(x2, norm_g.reshape(1, D), w_perm_t, tm=2048, tn=1280)
    u3 = u2.reshape(B, S, u2.shape[1])

    mu = shift_mu
    rows = [mu[:d_a], mu[d_a:2 * d_a], mu[2 * d_a:3 * d_a], mu[3 * d_a + 2 * rank:],
            w0, a0, k_k, k_a, r_k.reshape(-1), lnx_w, lnx_b]
    pvec = jnp.stack(rows + [jnp.zeros_like(w0)] * (16 - len(rows)), axis=0)
    mu_wa = mu[3 * d_a:3 * d_a + 2 * rank].reshape(1, 2 * rank)
    z = jnp.zeros((rank, d_a), F32)
    lora = _bf(jnp.concatenate(
        [jnp.concatenate([w_lora_up, z], axis=1), jnp.concatenate([z, a_lora_up], axis=1)], axis=0))
    ya = _rwkv(u3, pvec, mu_wa, lora, _block_ones(LANES), col_r=col_r, col_wa=col_wa, d_a=d_a)

    fb_pad = jnp.pad(f_bias, (0, LANES - h_b)).reshape(1, LANES)
    aug = _fprep(uf.reshape(B, S, LANES), fb_pad, _aug_selector(h_b))
    qg2 = jnp.tile(q_norm_g, FOX_HEADS).reshape(1, FOX_HEADS * HEAD)
    kg2 = jnp.tile(k_norm_g, FOX_HEADS).reshape(1, FOX_HEADS * HEAD)
    yb = _fox(u3, aug, qg2, kg2, _block_ones(LANES), col_q=col_q, d_b=d_b, tq=256, tk=256)

    return _out(x2, ya.reshape(T, d_a), yb.reshape(T, d_b), u2,
                w_out_a, w_out_b, w_out, out_gain.reshape(1, D), tm=1024)


def kernel(x, norm_g, w_in, shift_mu, w_lora_up, w0, a_lora_up, a0, k_k, k_a, r_k, lnx_w, lnx_b,
           f_bias, q_norm_g, k_norm_g, w_out_a, w_out_b, w_out, final_norm_g):
    B, S, D = x.shape
    depth = w_in.shape[0]
    assert depth == 1, "the fused output stage applies the final norm after the single layer"
    x2 = x.reshape(B * S, D)
    out = _layer(x2, B, S, norm_g[0], w_in[0], shift_mu[0], w_lora_up[0], w0[0], a_lora_up[0],
                 a0[0], k_k[0], k_a[0], r_k[0], lnx_w[0], lnx_b[0], f_bias[0], q_norm_g[0],
                 k_norm_g[0], w_out_a[0], w_out_b[0], w_out[0], final_norm_g)
    return out.reshape(B, S, D)
```

```python
import functools

import jax
import jax.numpy as jnp
from jax import lax
from jax.experimental import pallas as pl
from jax.experimental.pallas import tpu as pltpu

F32 = jnp.float32
BF16 = jnp.bfloat16

HEAD = 64
LANES = 128
RMS_EPS = 1e-6
LNX_EPS = 64e-5
CHUNK = 64
GROUP = 4
GW = GROUP * HEAD
RWKV_NB = 8
FOX_HEADS = 4
AUG0 = HEAD
NORM_ROWS = 512
OUT_SUB = 512
VROWS = HEAD + 16
LOG2E = 1.4426950408889634
DECAY_SCALE = -0.6065306597126334
VMEM_LIMIT = 56 * 1024 * 1024


def _bf(x):
    return x.astype(BF16)


def _dot(a, b):
    return jnp.dot(a, b, preferred_element_type=F32)


def _dot_nt(a, b):
    return lax.dot_general(a, b, (((1,), (1,)), ((), ())), preferred_element_type=F32)


def _dot_tn(a, b):
    return lax.dot_general(a, b, (((0,), (0,)), ((), ())), preferred_element_type=F32)


def _split2(x):
    hi = _bf(x)
    lo = _bf(x - hi.astype(F32))
    return hi, lo


def _split3(x):
    hi = _bf(x)
    r1 = x - hi.astype(F32)
    mid = _bf(r1)
    lo = _bf(r1 - mid.astype(F32))
    return hi, mid, lo


def _head_sums(xs, g, two_pass):
    m, w = xs[0].shape
    gw = g.shape[0]
    nt = w // gw
    parts = []
    for x, tp in zip(xs, two_pass):
        for piece in (_split2(x) if tp else (_bf(x),)):
            parts += [piece[:, t * gw:(t + 1) * gw] for t in range(nt)]
    r = _dot(jnp.concatenate(parts, axis=0), g)
    tile = lambda i: r[i * m:(i + 1) * m]
    outs, base = [], 0
    for tp in two_pass:
        if tp:
            cols = [tile(base + t) + tile(base + nt + t) for t in range(nt)]
        else:
            cols = [tile(base + t) for t in range(nt)]
        outs.append(jnp.concatenate(cols, axis=1))
        base += (2 if tp else 1) * nt
    return outs


def _sigmoid(x):
    return 1.0 / (1.0 + jnp.exp2(x * (-LOG2E)))


def _regroup_rows_kernel(w_ref, o_ref, *, segs):
    total = sum(width for _, width in segs)
    if total < o_ref.shape[0]:
        t0 = total // 16 * 16
        o_ref[t0:, :] = jnp.zeros((o_ref.shape[0] - t0, o_ref.shape[1]), BF16)
    dst = 0
    for src, width in segs:
        o_ref[dst:dst + width, :] = _bf(w_ref[src:src + width, :])
        dst += width


def _regroup_rows(wt, segs, *, pad_to, tc):
    rows, cols = wt.shape
    n_out = -(-sum(width for _, width in segs) // pad_to) * pad_to
    assert cols % tc == 0 and all(src % 8 == 0 for src, _ in segs)
    return pl.pallas_call(
        functools.partial(_regroup_rows_kernel, segs=tuple(segs)),
        grid=(cols // tc,),
        in_specs=[pl.BlockSpec((rows, tc), lambda i: (0, i))],
        out_specs=pl.BlockSpec((n_out, tc), lambda i: (0, i)),
        out_shape=jax.ShapeDtypeStruct((n_out, cols), BF16),
        compiler_params=pltpu.CompilerParams(
            dimension_semantics=("parallel",), vmem_limit_bytes=VMEM_LIMIT),
        name="wperm",
    )(wt)


def _inproj_kernel(x_ref, g_ref, w_ref, o_ref, of_ref, *, tn):
    n_cols = w_ref.shape[0]
    for c in range(x_ref.shape[0] // NORM_ROWS):
        rows = slice(c * NORM_ROWS, (c + 1) * NORM_ROWS)
        x = x_ref[rows, :]
        ms = jnp.mean(x * x, axis=-1, keepdims=True)
        h = _bf(x * lax.rsqrt(ms + RMS_EPS) * g_ref[...])
        for j in range(n_cols // tn):
            acc = _dot_nt(h, w_ref[j * tn:(j + 1) * tn, :])
            o_ref[rows, j * tn:(j + 1) * tn] = _bf(acc)
        of_ref[rows, :] = acc[:, tn - LANES:]


def _inproj(x2, norm_g, w_perm_t, *, tm, tn):
    T, D = x2.shape
    N = w_perm_t.shape[0]
    assert T % tm == 0 and N % tn == 0 and tm % NORM_ROWS == 0
    return pl.pallas_call(
        functools.partial(_inproj_kernel, tn=tn),
        grid=(T // tm,),
        in_specs=[
            pl.BlockSpec((tm, D), lambda i: (i, 0)),
            pl.BlockSpec((1, D), lambda i: (0, 0)),
            pl.BlockSpec((N, D), lambda i: (0, 0), pipeline_mode=pl.Buffered(1)),
        ],
        out_specs=[
            pl.BlockSpec((tm, N), lambda i: (i, 0)),
            pl.BlockSpec((tm, LANES), lambda i: (i, 0)),
        ],
        out_shape=[
            jax.ShapeDtypeStruct((T, N), BF16),
            jax.ShapeDtypeStruct((T, LANES), F32),
        ],
        compiler_params=pltpu.CompilerParams(
            dimension_semantics=("parallel",), vmem_limit_bytes=VMEM_LIMIT),
        name="inproj",
    )(x2, norm_g, w_perm_t)


def _fprep_kernel(f_ref, fb_ref, sel_ref, aug_ref, *, blk):
    S = f_ref.shape[1]
    row = lax.broadcasted_iota(jnp.int32, (blk, blk), 0)
    col = lax.broadcasted_iota(jnp.int32, (blk, blk), 1)
    tril = _bf(jnp.where(row >= col, 1.0, 0.0))
    local = []
    for i in range(S // blk):
        z = f_ref[0, i * blk:(i + 1) * blk, :] + fb_ref[...]
        lf = jnp.minimum(z, 0.0) - jnp.log1p(jnp.exp(-jnp.abs(z)))
        hi, mid, lo = _split3(lf)
        local.append(_dot(tril, hi) + _dot(tril, mid) + _dot(tril, lo))
    carry = jnp.zeros((1, LANES), F32)
    cums = []
    for x in local:
        cums.append(x + carry)
        carry = cums[-1][blk - 1:blk, :]
    for i, c in enumerate(cums):
        pieces = jnp.concatenate(_split3(c * LOG2E), axis=1)
        aug_ref[0, i * blk:(i + 1) * blk, :] = _bf(_dot(pieces, sel_ref[...]))


def _fprep(uf3, fb_pad, sel):
    B, S, _ = uf3.shape
    return pl.pallas_call(
        functools.partial(_fprep_kernel, blk=256),
        grid=(B,),
        in_specs=[
            pl.BlockSpec((1, S, LANES), lambda b: (b, 0, 0)),
            pl.BlockSpec((1, LANES), lambda b: (0, 0)),
            pl.BlockSpec(sel.shape, lambda b: (0, 0)),
        ],
        out_specs=pl.BlockSpec((1, S, LANES), lambda b: (b, 0, 0)),
        out_shape=jax.ShapeDtypeStruct((B, S, LANES), BF16),
        compiler_params=pltpu.CompilerParams(dimension_semantics=("parallel",)),
        name="fprep",
    )(uf3, fb_pad, sel)


def _aug_selector(n_heads):
    assert AUG0 + 6 * n_heads <= LANES
    r = jnp.arange(LANES)[:, None]
    c = jnp.arange(LANES)[None, :]
    blocks = []
    for i in range(3):
        m = jnp.where(c == AUG0 + 6 * r + i, -1.0, jnp.where(c == AUG0 + 6 * r + 3 + i, 1.0, 0.0))
        blocks.append(jnp.where(r < n_heads, m, 0.0))
    return _bf(jnp.concatenate(blocks, axis=0))


_P_MU_R, _P_MU_K, _P_MU_V, _P_MU_G, _P_W0, _P_A0, _P_KK, _P_KA, _P_RK, _P_LW, _P_LB = range(11)


def _shift_mix(x, carry_row, mu):
    rolled = pltpu.roll(x, shift=1, axis=0)
    head = rolled[0:8]
    row = lax.broadcasted_iota(jnp.int32, head.shape, 0)
    prev = jnp.concatenate([jnp.where(row == 0, carry_row, head), rolled[8:]], axis=0)
    return x + (prev - x) * mu


def _block_diag(x, half_masks):
    xb = _bf(x)
    n_tiles = xb.shape[1] // LANES
    zero = jnp.zeros((xb.shape[0], LANES), BF16)
    rows = []
    for h in range(xb.shape[1] // HEAD):
        t = h // 2
        piece = xb[:, t * LANES:(t + 1) * LANES] * half_masks[h % 2]
        rows.append(jnp.concatenate([piece if i == t else zero for i in range(n_tiles)], axis=1))
    return jnp.concatenate(rows, axis=0)


def _rwkv_kernel(r_ref, k_ref, v_ref, g_ref, wa_ref, pv_ref, muwa_ref, lora_ref, gm_ref,
                 o_ref, state_scr, carry_scr):
    c = pl.program_id(1)
    NB = r_ref.shape[0]
    L = r_ref.shape[1]
    DA = r_ref.shape[2]
    n_groups = DA // GW

    @pl.when(c == 0)
    def _():
        state_scr[...] = jnp.zeros_like(state_scr)
        carry_scr[...] = jnp.zeros_like(carry_scr)

    def prm(i):
        return pv_ref[i:i + 1, :]

    gones = gm_ref[...]
    row = lax.broadcasted_iota(jnp.int32, (L, L), 0)
    col = lax.broadcasted_iota(jnp.int32, (L, L), 1)
    tril = _bf(jnp.where(row >= col, 1.0, 0.0))
    lane128 = lax.broadcasted_iota(jnp.int32, (1, LANES), 1)
    half_masks = [_bf(jnp.where(lane128 // HEAD == i, 1.0, 0.0)) for i in range(2)]
    prow = lax.broadcasted_iota(jnp.int32, (L, GW), 0)
    pcol = lax.broadcasted_iota(jnp.int32, (L, GW), 1) & (HEAD - 1)
    incl_p = prow >= pcol
    strict_p = prow > pcol
    eye_p = jnp.where(prow == pcol, 1.0, 0.0)
    st_mask = (lax.broadcasted_iota(jnp.int32, (GW, GW), 0) // HEAD
               == lax.broadcasted_iota(jnp.int32, (GW, GW), 1) // HEAD)
    bd = lambda x: _block_diag(x, half_masks)

    def prep(bb):
        raw = [ref[bb].astype(F32) for ref in (r_ref, k_ref, v_ref, g_ref, wa_ref)]
        mus = [prm(_P_MU_R), prm(_P_MU_K), prm(_P_MU_V), prm(_P_MU_G), muwa_ref[...]]
        mixed = []
        for i, x in enumerate(raw):
            wdt = x.shape[1]
            mixed.append(_shift_mix(x, carry_scr[bb, i:i + 1, 0:wdt], mus[i]))
            carry_scr[bb, i:i + 1, 0:wdt] = x[L - 1:L, :]
        r, k, v, gate, wa = mixed

        wa_act = jnp.where(lane128 < HEAD, jnp.tanh(wa), wa)
        lo = _dot(_bf(wa_act), lora_ref[...])
        ld = (DECAY_SCALE * LOG2E) * _sigmoid(prm(_P_W0) + lo[:, 0:DA])
        a = _sigmoid(prm(_P_A0) + lo[:, DA:2 * DA])

        kk = k * prm(_P_KK)
        k2 = k * (1.0 + (a - 1.0) * prm(_P_KA))
        ssq, bon = _head_sums([kk * kk, r * k2 * prm(_P_RK)], gones, (False, False))
        kk = kk * jnp.minimum(lax.rsqrt(ssq), 1e12)

        h3, m3, l3 = _split3(ld)
        cum = _dot(tril, h3) + _dot(tril, m3) + _dot(tril, l3)
        e_neg = jnp.exp2(-cum)
        w_l = jnp.exp2(cum[L - 1:L, :])
        a_t = -kk * jnp.exp2(cum - ld)
        r_t = r * jnp.exp2(cum)
        b_t = (kk * a) * e_neg
        k_t = k2 * e_neg
        return dict(v=v, gate=gate, bonus=bon * v, w_l=w_l, a_t=a_t, r_t=r_t, b_t=b_t, k_t=k_t)

    def chain(bbs, pre, ys):
        streams = [(i, g) for i in range(len(bbs)) for g in range(n_groups)]
        ns = len(streams)
        gsl = lambda g: slice(g * GW, (g + 1) * GW)
        P = lambda name, st: pre[st[0]][name][:, gsl(st[1])]
        sidx = lambda st: bbs[st[0]] * n_groups + st[1]

        s0 = [state_scr[sidx(st)] for st in streams]
        s0b = [_bf(x) for x in s0]
        ar = [_bf(jnp.concatenate([P("a_t", st), P("r_t", st)], axis=0)) for st in streams]
        bk = [jnp.concatenate([bd(P("b_t", st)), bd(P("k_t", st))], axis=0) for st in streams]
        pm = [_dot_nt(ar[i], bk[i]) for i in range(ns)]
        ars = [_dot_nt(ar[i], s0b[i]) for i in range(ns)]
        a_ab = [jnp.where(strict_p, p[0:L, 0:GW], 0.0) for p in pm]
        a_ak = [jnp.where(strict_p, p[0:L, GW:2 * GW], 0.0) for p in pm]
        a_rb = [jnp.where(incl_p, p[L:2 * L, 0:GW], 0.0) for p in pm]
        a_rk = [jnp.where(incl_p, p[L:2 * L, GW:2 * GW], 0.0) for p in pm]
        vbd = [bd(P("v", st)) for st in streams]
        akv = [_dot(_bf(jnp.concatenate([a_ak[i], a_rk[i]], axis=0)), vbd[i])
               for i in range(ns)]

        pw = [_dot(_bf(x), bd(x)) for x in a_ab]
        tinv = [eye_p + x for x in a_ab]
        n = 2
        while n < L:
            last = 2 * n >= L
            pwb = [bd(x) for x in pw]
            if last:
                tinv = [tinv[i] + _dot(_bf(tinv[i]), pwb[i]) for i in range(ns)]
            else:
                both = [_dot(_bf(jnp.concatenate([pw[i], tinv[i]], axis=0)), pwb[i])
                        for i in range(ns)]
                pw = [x[0:L] for x in both]
                tinv = [tinv[i] + both[i][L:2 * L] for i in range(ns)]
            n *= 2

        u = [_dot(_bf(tinv[i]), bd(ars[i][0:L] + akv[i][0:L])) for i in range(ns)]
        ys.extend(ars[i][L:2 * L] + akv[i][L:2 * L] + _dot(_bf(a_rb[i]), bd(u[i]))
                  for i in range(ns))
        for i, st in enumerate(streams):
            w_l = P("w_l", st)
            uv = _bf(jnp.concatenate([u[i], P("v", st)], axis=0))
            bkh = _bf(jnp.concatenate([P("b_t", st) * w_l, P("k_t", st) * w_l], axis=0))
            state_scr[sidx(st)] = s0[i] * w_l + jnp.where(st_mask, _dot_tn(uv, bkh), 0.0)

    def finish(bbs, pre, ys):
        nb = len(bbs)
        inv_n = 1.0 / HEAD
        y = [jnp.concatenate([ys[i * n_groups + g] for g in range(n_groups)], axis=1)
             for i in range(nb)]
        mean = [m * inv_n for m in _head_sums(y, gones, (True,) * nb)]
        yc = [y[i] - mean[i] for i in range(nb)]
        var = [s * inv_n for s in _head_sums([x * x for x in yc], gones, (False,) * nb)]
        for i, bb in enumerate(bbs):
            yn = yc[i] * lax.rsqrt(var[i] + LNX_EPS) * prm(_P_LW) + prm(_P_LB)
            gate = pre[i]["gate"]
            o_ref[bb] = _bf((yn + pre[i]["bonus"]) * (gate * _sigmoid(gate)))

    bbs = list(range(NB))
    pre = [prep(bb) for bb in bbs]
    ys = []
    chain(bbs, pre, ys)
    finish(bbs, pre, ys)


def _rwkv(u3, pvec, mu_wa, lora, gones, *, col_r, col_wa, d_a):
    B, S, _ = u3.shape
    L = CHUNK
    nb = RWKV_NB
    assert L == HEAD and d_a % GW == 0 and B % nb == 0 and S % L == 0
    cb = col_r // d_a
    blk = lambda off: pl.BlockSpec((nb, L, d_a), lambda b, c, off=off: (b, c, cb + off))
    full = lambda arr: pl.BlockSpec(arr.shape, lambda b, c: (0,) * arr.ndim)
    return pl.pallas_call(
        _rwkv_kernel,
        grid=(B // nb, S // L),
        in_specs=[
            blk(0), blk(1), blk(2), blk(3),
            pl.BlockSpec((nb, L, LANES), lambda b, c: (b, c, col_wa // LANES)),
            full(pvec), full(mu_wa), full(lora), full(gones),
        ],
        out_specs=pl.BlockSpec((nb, L, d_a), lambda b, c: (b, c, 0)),
        out_shape=jax.ShapeDtypeStruct((B, S, d_a), BF16),
        scratch_shapes=[
            pltpu.VMEM((nb * (d_a // GW), GW, GW), F32),
            pltpu.VMEM((nb, 8, d_a), F32),
        ],
        compiler_params=pltpu.CompilerParams(
            dimension_semantics=("parallel", "arbitrary"), vmem_limit_bytes=VMEM_LIMIT),
        name="rwkv7",
    )(u3, u3, u3, u3, u3, pvec, mu_wa, lora, gones)


def _sublane_max(x):
    for s in (4, 2, 1):
        x = jnp.maximum(x, pltpu.roll(x, s, axis=0))
    return x


def _fox_kernel(q_ref, k_ref, v_ref, g_ref, ak_ref, qg_ref, kg_ref, gm_ref,
                o_ref, kp_scr, vt_scr, qp_scr, sta_scr, stb_scr, m_scr, acc_scr, *, tq, tk):
    p = pl.program_id(1)
    S = k_ref.shape[1]
    nh = q_ref.shape[2] // HEAD
    g128 = gm_ref[...]
    lane = lax.broadcasted_iota(jnp.int32, (1, LANES), 1)
    feat = lane < HEAD

    def minus_lanes(hh):
        lo = AUG0 + 6 * (nh * p + hh)
        return (lane >= lo) & (lane < lo + 3)

    def plus_lanes(hh):
        lo = AUG0 + 6 * (nh * p + hh) + 3
        return (lane >= lo) & (lane < lo + 3)

    def head_tile(x, hh):
        t = x[:, (hh // 2) * LANES:(hh // 2 + 1) * LANES]
        return pltpu.roll(t, HEAD, axis=1) if hh % 2 else t

    for i in range(S // tk):
        rows = slice(i * tk, (i + 1) * tk)
        kb = k_ref[0, rows, :].astype(F32)
        (ssq,) = _head_sums([kb * kb], g128, (False,))
        kn = kb * lax.rsqrt(ssq * (1.0 / HEAD) + RMS_EPS) * kg_ref[...]
        aug = ak_ref[0, rows, :].astype(F32)
        for hh in range(nh):
            kp = jnp.where(feat, head_tile(kn, hh), jnp.where(plus_lanes(hh), 1.0, aug))
            kp_scr[hh, rows, :] = _bf(kp)
        vt = _bf(jnp.transpose(v_ref[0, rows, :].astype(F32)))
        for hh in range(nh):
            vt_scr[hh * VROWS:hh * VROWS + HEAD, rows] = vt[hh * HEAD:(hh + 1) * HEAD, :]
            vt_scr[hh * VROWS + HEAD:(hh + 1) * VROWS, rows] = jnp.ones((VROWS - HEAD, tk), BF16)

    def q_side(qi):
        rows = slice(qi * tq, (qi + 1) * tq)
        q = q_ref[0, rows, :].astype(F32)
        (ssq,) = _head_sums([q * q], g128, (False,))
        qn = q * lax.rsqrt(ssq * (1.0 / HEAD) + RMS_EPS) * (qg_ref[...] * (HEAD ** -0.5 * LOG2E))
        augq = ak_ref[0, rows, :].astype(F32)
        for hh in range(nh):
            qp_scr[qi % 2, hh] = _bf(jnp.where(
                feat, head_tile(qn, hh),
                jnp.where(minus_lanes(hh), 1.0, jnp.where(plus_lanes(hh), augq, 0.0))))
        m_scr[qi % 2] = jnp.full(m_scr.shape[1:], -1e30, F32)
        acc_scr[qi % 2] = jnp.zeros(acc_scr.shape[1:], F32)

    def scores(qi, j, st_ref):
        for hh in range(nh):
            st_ref[hh] = _dot_nt(kp_scr[hh, j * tk:(j + 1) * tk, :], qp_scr[qi % 2, hh])

    def softmax_pv(qi, j, st_ref):
        st = [st_ref[hh] for hh in range(nh)]
        if j == qi:
            keyi = lax.broadcasted_iota(jnp.int32, (tk, tq), 0)
            qryi = lax.broadcasted_iota(jnp.int32, (tk, tq), 1)
            st = [jnp.where(qryi >= keyi, x, -1e30) for x in st]
        st = [x.reshape(tk // 8, 8, tq) for x in st]
        m_old = [m_scr[qi % 2, hh] for hh in range(nh)]
        m_new = [jnp.maximum(m_old[hh], _sublane_max(jnp.max(st[hh], axis=0))) for hh in range(nh)]
        alpha = [jnp.exp2(m_old[hh] - m_new[hh]) for hh in range(nh)]
        pt = [_bf(jnp.exp2(st[hh] - m_new[hh][None]).reshape(tk, tq)) for hh in range(nh)]
        pv = [_dot(vt_scr[hh * VROWS:(hh + 1) * VROWS, j * tk:(j + 1) * tk], pt[hh])
              for hh in range(nh)]
        for hh in range(nh):
            m_scr[qi % 2, hh] = m_new[hh]
            acc = acc_scr[qi % 2, hh].reshape(VROWS // 8, 8, tq) * alpha[hh][None]
            acc_scr[qi % 2, hh] = acc.reshape(VROWS, tq) + pv[hh]

    def finish(qi):
        rows = slice(qi * tq, (qi + 1) * tq)
        ot = jnp.concatenate(
            [(acc_scr[qi % 2, hh, 0:HEAD, :].reshape(HEAD // 8, 8, tq)
              * (1.0 / acc_scr[qi % 2, hh, HEAD:HEAD + 8, :])[None]).reshape(HEAD, tq)
             for hh in range(nh)], axis=0)
        g = g_ref[0, rows, :].astype(F32)
        o_ref[0, rows, :] = _bf(jnp.transpose(ot) * (g * _sigmoid(g)))

    pairs = [(qi, j) for qi in range(S // tq) for j in range(qi + 1)]
    bufs = (sta_scr, stb_scr)
    for n, (qi, j) in enumerate(pairs):
        if j == 0:
            q_side(qi)
        scores(qi, j, bufs[n % 2])
        if n > 0:
            pqi, pj = pairs[n - 1]
            softmax_pv(pqi, pj, bufs[(n - 1) % 2])
            if pj == pqi:
                finish(pqi)
    qi, j = pairs[-1]
    softmax_pv(qi, j, bufs[(len(pairs) - 1) % 2])
    finish(qi)


def _fox(u3, aug, qg2, kg2, g128, *, col_q, d_b, tq, tk):
    B, S, _ = u3.shape
    fw = qg2.shape[1]
    nh = fw // HEAD
    assert S % tq == 0 and tq == tk and d_b % fw == 0
    cq, ck, cv, cg = ((col_q + i * d_b) // fw for i in range(4))
    return pl.pallas_call(
        functools.partial(_fox_kernel, tq=tq, tk=tk),
        grid=(B, d_b // fw),
        in_specs=[
            pl.BlockSpec((1, S, fw), lambda b, p: (b, 0, cq + p)),
            pl.BlockSpec((1, S, fw), lambda b, p: (b, 0, ck + p)),
            pl.BlockSpec((1, S, fw), lambda b, p: (b, 0, cv + p)),
            pl.BlockSpec((1, S, fw), lambda b, p: (b, 0, cg + p)),
            pl.BlockSpec((1, S, LANES), lambda b, p: (b, 0, 0)),
            pl.BlockSpec((1, fw), lambda b, p: (0, 0)),
            pl.BlockSpec((1, fw), lambda b, p: (0, 0)),
            pl.BlockSpec((LANES, LANES), lambda b, p: (0, 0)),
        ],
        out_specs=pl.BlockSpec((1, S, fw), lambda b, p: (b, 0, p)),
        out_shape=jax.ShapeDtypeStruct((B, S, d_b), BF16),
        scratch_shapes=[
            pltpu.VMEM((nh, S, LANES), BF16),
            pltpu.VMEM((nh * VROWS, S), BF16),
            pltpu.VMEM((2, nh, tq, LANES), BF16),
            pltpu.VMEM((nh, tk, tq), F32),
            pltpu.VMEM((nh, tk, tq), F32),
            pltpu.VMEM((2, nh, 8, tq), F32),
            pltpu.VMEM((2, nh, VROWS, tq), F32),
        ],
        compiler_params=pltpu.CompilerParams(
            dimension_semantics=("parallel", "parallel"), vmem_limit_bytes=VMEM_LIMIT),
        name="fox",
    )(u3, u3, u3, u3, aug, qg2, kg2, g128)


def _out_kernel(x_ref, ya_ref, yb_ref, ga_ref, gb_ref, woa_ref, wob_ref, wo_ref, fg_ref, o_ref,
                woa_s, wob_s, wo_s):
    @pl.when(pl.program_id(0) == 0)
    def _():
        woa_s[...] = _bf(woa_ref[...])
        wob_s[...] = _bf(wob_ref[...])
        wo_s[...] = _bf(wo_ref[...])

    for s in range(x_ref.shape[0] // OUT_SUB):
        rows = slice(s * OUT_SUB, (s + 1) * OUT_SUB)
        za = _dot(ya_ref[rows, :], woa_s[...])
        zb = _dot(yb_ref[rows, :], wob_s[...])
        merged = (_sigmoid(ga_ref[rows, :].astype(F32)) * za
                  + _sigmoid(gb_ref[rows, :].astype(F32)) * zb)
        o = x_ref[rows, :] + _dot(_bf(merged), wo_s[...])
        ms = jnp.mean(o * o, axis=-1, keepdims=True)
        o_ref[rows, :] = o * lax.rsqrt(ms + RMS_EPS) * fg_ref[...]


def _out(x2, ya2, yb2, u2, woa, wob, wo, fg, *, tm):
    T, D = x2.shape
    assert T % tm == 0 and tm % OUT_SUB == 0
    full = lambda arr: pl.BlockSpec(arr.shape, lambda i: (0,) * arr.ndim)
    return pl.pallas_call(
        _out_kernel,
        grid=(T // tm,),
        in_specs=[
            pl.BlockSpec((tm, D), lambda i: (i, 0)),
            pl.BlockSpec((tm, ya2.shape[1]), lambda i: (i, 0)),
            pl.BlockSpec((tm, yb2.shape[1]), lambda i: (i, 0)),
            pl.BlockSpec((tm, D), lambda i: (i, 0)),
            pl.BlockSpec((tm, D), lambda i: (i, 1)),
            full(woa), full(wob), full(wo), full(fg),
        ],
        out_specs=pl.BlockSpec((tm, D), lambda i: (i, 0)),
        out_shape=jax.ShapeDtypeStruct((T, D), F32),
        scratch_shapes=[pltpu.VMEM(w.shape, BF16) for w in (woa, wob, wo)],
        compiler_params=pltpu.CompilerParams(
            dimension_semantics=("arbitrary",), vmem_limit_bytes=VMEM_LIMIT),
        name="outstage",
    )(x2, ya2, yb2, u2, u2, woa, wob, wo, fg)


def _block_ones(width):
    i = jnp.arange(width) // HEAD
    return (i[:, None] == i[None, :]).astype(BF16)


def _layer(x2, B, S, norm_g, w_in, shift_mu, w_lora_up, w0, a_lora_up, a0, k_k, k_a, r_k,
           lnx_w, lnx_b, f_bias, q_norm_g, k_norm_g, w_out_a, w_out_b, w_out, out_gain):
    T, D = x2.shape
    d_a = w0.shape[0]
    d_b = w_out_b.shape[0]
    rank = w_lora_up.shape[0]
    h_b = f_bias.shape[0]
    rw = 4 * d_a + 2 * rank
    fx = 4 * d_b + h_b

    segs = [(rw + fx, 2 * D), (0, 3 * d_a), (3 * d_a + 2 * rank, d_a), (rw, 4 * d_b),
            (3 * d_a, 2 * rank), (rw + 4 * d_b, h_b)]
    w_perm_t = _regroup_rows(jnp.transpose(w_in), segs, pad_to=LANES, tc=LANES)
    col_r = 2 * D
    col_q = col_r + 4 * d_a
    col_wa = col_q + 4 * d_b

    u2, uf = _inproj(x2, norm_g.reshape(1, D), w_perm_t, tm=1024, tn=1280)
    u3 = u2.reshape(B, S, u2.shape[1])

    mu = shift_mu
    rows = [mu[:d_a], mu[d_a:2 * d_a], mu[2 * d_a:3 * d_a], mu[3 * d_a + 2 * rank:],
            w0, a0, k_k, k_a, r_k.reshape(-1), lnx_w, lnx_b]
    pvec = jnp.stack(rows + [jnp.zeros_like(w0)] * (16 - len(rows)), axis=0)
    mu_wa = mu[3 * d_a:3 * d_a + 2 * rank].reshape(1, 2 * rank)
    z = jnp.zeros((rank, d_a), F32)
    lora = _bf(jnp.concatenate(
        [jnp.concatenate([w_lora_up, z], axis=1), jnp.concatenate([z, a_lora_up], axis=1)], axis=0))
    ya = _rwkv(u3, pvec, mu_wa, lora, _block_ones(LANES), col_r=col_r, col_wa=col_wa, d_a=d_a)

    fb_pad = jnp.pad(f_bias, (0, LANES - h_b)).reshape(1, LANES)
    aug = _fprep(uf.reshape(B, S, LANES), fb_pad, _aug_selector(h_b))
    qg2 = jnp.tile(q_norm_g, FOX_HEADS).reshape(1, FOX_HEADS * HEAD)
    kg2 = jnp.tile(k_norm_g, FOX_HEADS).reshape(1, FOX_HEADS * HEAD)
    yb = _fox(u3, aug, qg2, kg2, _block_ones(LANES), col_q=col_q, d_b=d_b, tq=256, tk=256)

    return _out(x2, ya.reshape(T, d_a), yb.reshape(T, d_b), u2,
                w_out_a, w_out_b, w_out, out_gain.reshape(1, D), tm=1024)


def kernel(x, norm_g, w_in, shift_mu, w_lora_up, w0, a_lora_up, a0, k_k, k_a, r_k, lnx_w, lnx_b,
           f_bias, q_norm_g, k_norm_g, w_out_a, w_out_b, w_out, final_norm_g):
    B, S, D = x.shape
    depth = w_in.shape[0]
    assert depth == 1, "the fused output stage applies the final norm after the single layer"
    x2 = x.reshape(B * S, D)
    out = _layer(x2, B, S, norm_g[0], w_in[0], shift_mu[0], w_lora_up[0], w0[0], a_lora_up[0],
                 a0[0], k_k[0], k_a[0], r_k[0], lnx_w[0], lnx_b[0], f_bias[0], q_norm_g[0],
                 k_norm_g[0], w_out_a[0], w_out_b[0], w_out[0], final_norm_g)
    return out.reshape(B, S, D)
```

```python
import functools

import jax
import jax.numpy as jnp
from jax import lax
from jax.experimental import pallas as pl
from jax.experimental.pallas import tpu as pltpu

F32 = jnp.float32
BF16 = jnp.bfloat16

HEAD = 64
LANES = 128
RMS_EPS = 1e-6
LNX_EPS = 64e-5
CHUNK = 64
GROUP = 2
GW = GROUP * HEAD
RWKV_NB = 8
FOX_HEADS = 4
AUG0 = HEAD
NORM_ROWS = 512
OUT_SUB = 512
VROWS = HEAD + 16
LOG2E = 1.4426950408889634
DECAY_SCALE = -0.6065306597126334
VMEM_LIMIT = 56 * 1024 * 1024


def _bf(x):
    return x.astype(BF16)


def _dot(a, b):
    return jnp.dot(a, b, preferred_element_type=F32)


def _dot_nt(a, b):
    return lax.dot_general(a, b, (((1,), (1,)), ((), ())), preferred_element_type=F32)


def _dot_tn(a, b):
    return lax.dot_general(a, b, (((0,), (0,)), ((), ())), preferred_element_type=F32)


def _split2(x):
    hi = _bf(x)
    lo = _bf(x - hi.astype(F32))
    return hi, lo


def _split3(x):
    hi = _bf(x)
    r1 = x - hi.astype(F32)
    mid = _bf(r1)
    lo = _bf(r1 - mid.astype(F32))
    return hi, mid, lo


def _head_sums(xs, g, two_pass):
    m, w = xs[0].shape
    gw = g.shape[0]
    nt = w // gw
    parts = []
    for x, tp in zip(xs, two_pass):
        for piece in (_split2(x) if tp else (_bf(x),)):
            parts += [piece[:, t * gw:(t + 1) * gw] for t in range(nt)]
    r = _dot(jnp.concatenate(parts, axis=0), g)
    tile = lambda i: r[i * m:(i + 1) * m]
    outs, base = [], 0
    for tp in two_pass:
        if tp:
            cols = [tile(base + t) + tile(base + nt + t) for t in range(nt)]
        else:
            cols = [tile(base + t) for t in range(nt)]
        outs.append(jnp.concatenate(cols, axis=1))
        base += (2 if tp else 1) * nt
    return outs


def _sigmoid(x):
    return 1.0 / (1.0 + jnp.exp2(x * (-LOG2E)))


def _regroup_rows_kernel(w_ref, o_ref, *, segs):
    total = sum(width for _, width in segs)
    if total < o_ref.shape[0]:
        t0 = total // 16 * 16
        o_ref[t0:, :] = jnp.zeros((o_ref.shape[0] - t0, o_ref.shape[1]), BF16)
    dst = 0
    for src, width in segs:
        o_ref[dst:dst + width, :] = _bf(w_ref[src:src + width, :])
        dst += width


def _regroup_rows(wt, segs, *, pad_to, tc):
    rows, cols = wt.shape
    n_out = -(-sum(width for _, width in segs) // pad_to) * pad_to
    assert cols % tc == 0 and all(src % 8 == 0 for src, _ in segs)
    return pl.pallas_call(
        functools.partial(_regroup_rows_kernel, segs=tuple(segs)),
        grid=(cols // tc,),
        in_specs=[pl.BlockSpec((rows, tc), lambda i: (0, i))],
        out_specs=pl.BlockSpec((n_out, tc), lambda i: (0, i)),
        out_shape=jax.ShapeDtypeStruct((n_out, cols), BF16),
        compiler_params=pltpu.CompilerParams(
            dimension_semantics=("parallel",), vmem_limit_bytes=VMEM_LIMIT),
        name="wperm",
    )(wt)


def _inproj_kernel(x_ref, g_ref, w_ref, o_ref, of_ref, *, tn):
    n_cols = w_ref.shape[0]
    for c in range(x_ref.shape[0] // NORM_ROWS):
        rows = slice(c * NORM_ROWS, (c + 1) * NORM_ROWS)
        x = x_ref[rows, :]
        ms = jnp.mean(x * x, axis=-1, keepdims=True)
        h = _bf(x * lax.rsqrt(ms + RMS_EPS) * g_ref[...])
        for j in range(n_cols // tn):
            acc = _dot_nt(h, w_ref[j * tn:(j + 1) * tn, :])
            o_ref[rows, j * tn:(j + 1) * tn] = _bf(acc)
        of_ref[rows, :] = acc[:, tn - LANES:]


def _inproj(x2, norm_g, w_perm_t, *, tm, tn):
    T, D = x2.shape
    N = w_perm_t.shape[0]
    assert T % tm == 0 and N % tn == 0 and tm % NORM_ROWS == 0
    return pl.pallas_call(
        functools.partial(_inproj_kernel, tn=tn),
        grid=(T // tm,),
        in_specs=[
            pl.BlockSpec((tm, D), lambda i: (i, 0)),
            pl.BlockSpec((1, D), lambda i: (0, 0)),
            pl.BlockSpec((N, D), lambda i: (0, 0), pipeline_mode=pl.Buffered(1)),
        ],
        out_specs=[
            pl.BlockSpec((tm, N), lambda i: (i, 0)),
            pl.BlockSpec((tm, LANES), lambda i: (i, 0)),
        ],
        out_shape=[
            jax.ShapeDtypeStruct((T, N), BF16),
            jax.ShapeDtypeStruct((T, LANES), F32),
        ],
        compiler_params=pltpu.CompilerParams(
            dimension_semantics=("parallel",), vmem_limit_bytes=VMEM_LIMIT),
        name="inproj",
    )(x2, norm_g, w_perm_t)


def _fprep_kernel(f_ref, fb_ref, sel_ref, aug_ref, *, blk):
    S = f_ref.shape[1]
    row = lax.broadcasted_iota(jnp.int32, (blk, blk), 0)
    col = lax.broadcasted_iota(jnp.int32, (blk, blk), 1)
    tril = _bf(jnp.where(row >= col, 1.0, 0.0))
    local = []
    for i in range(S // blk):
        z = f_ref[0, i * blk:(i + 1) * blk, :] + fb_ref[...]
        lf = jnp.minimum(z, 0.0) - jnp.log1p(jnp.exp(-jnp.abs(z)))
        hi, mid, lo = _split3(lf)
        local.append(_dot(tril, hi) + _dot(tril, mid) + _dot(tril, lo))
    carry = jnp.zeros((1, LANES), F32)
    cums = []
    for x in local:
        cums.append(x + carry)
        carry = cums[-1][blk - 1:blk, :]
    for i, c in enumerate(cums):
        pieces = jnp.concatenate(_split3(c * LOG2E), axis=1)
        aug_ref[0, i * blk:(i + 1) * blk, :] = _bf(_dot(pieces, sel_ref[...]))


def _fprep(uf3, fb_pad, sel):
    B, S, _ = uf3.shape
    return pl.pallas_call(
        functools.partial(_fprep_kernel, blk=256),
        grid=(B,),
        in_specs=[
            pl.BlockSpec((1, S, LANES), lambda b: (b, 0, 0)),
            pl.BlockSpec((1, LANES), lambda b: (0, 0)),
            pl.BlockSpec(sel.shape, lambda b: (0, 0)),
        ],
        out_specs=pl.BlockSpec((1, S, LANES), lambda b: (b, 0, 0)),
        out_shape=jax.ShapeDtypeStruct((B, S, LANES), BF16),
        compiler_params=pltpu.CompilerParams(dimension_semantics=("parallel",)),
        name="fprep",
    )(uf3, fb_pad, sel)


def _aug_selector(n_heads):
    assert AUG0 + 6 * n_heads <= LANES
    r = jnp.arange(LANES)[:, None]
    c = jnp.arange(LANES)[None, :]
    blocks = []
    for i in range(3):
        m = jnp.where(c == AUG0 + 6 * r + i, -1.0, jnp.where(c == AUG0 + 6 * r + 3 + i, 1.0, 0.0))
        blocks.append(jnp.where(r < n_heads, m, 0.0))
    return _bf(jnp.concatenate(blocks, axis=0))


_P_MU_R, _P_MU_K, _P_MU_V, _P_MU_G, _P_W0, _P_A0, _P_KK, _P_KA, _P_RK, _P_LW, _P_LB = range(11)


def _shift_mix(x, carry_row, mu):
    rolled = pltpu.roll(x, shift=1, axis=0)
    head = rolled[0:8]
    row = lax.broadcasted_iota(jnp.int32, head.shape, 0)
    prev = jnp.concatenate([jnp.where(row == 0, carry_row, head), rolled[8:]], axis=0)
    return x + (prev - x) * mu


def _block_diag(x, half_masks):
    xb = _bf(x)
    n_tiles = xb.shape[1] // LANES
    zero = jnp.zeros((xb.shape[0], LANES), BF16)
    rows = []
    for h in range(xb.shape[1] // HEAD):
        t = h // 2
        piece = xb[:, t * LANES:(t + 1) * LANES] * half_masks[h % 2]
        rows.append(jnp.concatenate([piece if i == t else zero for i in range(n_tiles)], axis=1))
    return jnp.concatenate(rows, axis=0)


def _rwkv_kernel(r_ref, k_ref, v_ref, g_ref, wa_ref, pv_ref, muwa_ref, lora_ref, gm_ref,
                 o_ref, state_scr, carry_scr):
    c = pl.program_id(1)
    NB = r_ref.shape[0]
    L = r_ref.shape[1]
    DA = r_ref.shape[2]
    n_groups = DA // GW

    @pl.when(c == 0)
    def _():
        state_scr[...] = jnp.zeros_like(state_scr)
        carry_scr[...] = jnp.zeros_like(carry_scr)

    def prm(i):
        return pv_ref[i:i + 1, :]

    gones = gm_ref[...]
    row = lax.broadcasted_iota(jnp.int32, (L, L), 0)
    col = lax.broadcasted_iota(jnp.int32, (L, L), 1)
    tril = _bf(jnp.where(row >= col, 1.0, 0.0))
    lane128 = lax.broadcasted_iota(jnp.int32, (1, LANES), 1)
    half_masks = [_bf(jnp.where(lane128 // HEAD == i, 1.0, 0.0)) for i in range(2)]
    prow = lax.broadcasted_iota(jnp.int32, (L, GW), 0)
    pcol = lax.broadcasted_iota(jnp.int32, (L, GW), 1) & (HEAD - 1)
    incl_p = prow >= pcol
    strict_p = prow > pcol
    eye_p = jnp.where(prow == pcol, 1.0, 0.0)
    st_mask = (lax.broadcasted_iota(jnp.int32, (GW, GW), 0) // HEAD
               == lax.broadcasted_iota(jnp.int32, (GW, GW), 1) // HEAD)
    bd = lambda x: _block_diag(x, half_masks)

    def prep(bb):
        raw = [ref[bb].astype(F32) for ref in (r_ref, k_ref, v_ref, g_ref, wa_ref)]
        mus = [prm(_P_MU_R), prm(_P_MU_K), prm(_P_MU_V), prm(_P_MU_G), muwa_ref[...]]
        mixed = []
        for i, x in enumerate(raw):
            wdt = x.shape[1]
            mixed.append(_shift_mix(x, carry_scr[bb, i:i + 1, 0:wdt], mus[i]))
            carry_scr[bb, i:i + 1, 0:wdt] = x[L - 1:L, :]
        r, k, v, gate, wa = mixed

        wa_act = jnp.where(lane128 < HEAD, jnp.tanh(wa), wa)
        lo = _dot(_bf(wa_act), lora_ref[...])
        ld = (DECAY_SCALE * LOG2E) * _sigmoid(prm(_P_W0) + lo[:, 0:DA])
        a = _sigmoid(prm(_P_A0) + lo[:, DA:2 * DA])

        kk = k * prm(_P_KK)
        k2 = k * (1.0 + (a - 1.0) * prm(_P_KA))
        ssq, bon = _head_sums([kk * kk, r * k2 * prm(_P_RK)], gones, (False, False))
        kk = kk * jnp.minimum(lax.rsqrt(ssq), 1e12)

        h3, m3, l3 = _split3(ld)
        cum = _dot(tril, h3) + _dot(tril, m3) + _dot(tril, l3)
        e_neg = jnp.exp2(-cum)
        w_l = jnp.exp2(cum[L - 1:L, :])
        a_t = -kk * jnp.exp2(cum - ld)
        r_t = r * jnp.exp2(cum)
        b_t = (kk * a) * e_neg
        k_t = k2 * e_neg
        return dict(v=v, gate=gate, bonus=bon * v, w_l=w_l, a_t=a_t, r_t=r_t, b_t=b_t, k_t=k_t)

    def chain(bbs, pre, ys):
        streams = [(i, g) for i in range(len(bbs)) for g in range(n_groups)]
        ns = len(streams)
        gsl = lambda g: slice(g * GW, (g + 1) * GW)
        P = lambda name, st: pre[st[0]][name][:, gsl(st[1])]
        sidx = lambda st: bbs[st[0]] * n_groups + st[1]

        s0 = [state_scr[sidx(st)] for st in streams]
        s0b = [_bf(x) for x in s0]
        ar = [_bf(jnp.concatenate([P("a_t", st), P("r_t", st)], axis=0)) for st in streams]
        bk = [jnp.concatenate([bd(P("b_t", st)), bd(P("k_t", st))], axis=0) for st in streams]
        pm = [_dot_nt(ar[i], bk[i]) for i in range(ns)]
        ars = [_dot_nt(ar[i], s0b[i]) for i in range(ns)]
        a_ab = [jnp.where(strict_p, p[0:L, 0:GW], 0.0) for p in pm]
        a_ak = [jnp.where(strict_p, p[0:L, GW:2 * GW], 0.0) for p in pm]
        a_rb = [jnp.where(incl_p, p[L:2 * L, 0:GW], 0.0) for p in pm]
        a_rk = [jnp.where(incl_p, p[L:2 * L, GW:2 * GW], 0.0) for p in pm]
        vbd = [bd(P("v", st)) for st in streams]
        akv = [_dot(_bf(jnp.concatenate([a_ak[i], a_rk[i]], axis=0)), vbd[i])
               for i in range(ns)]

        pw = [_dot(_bf(x), bd(x)) for x in a_ab]
        tinv = [eye_p + x for x in a_ab]
        n = 2
        while n < L:
            last = 2 * n >= L
            pwb = [bd(x) for x in pw]
            if last:
                tinv = [tinv[i] + _dot(_bf(tinv[i]), pwb[i]) for i in range(ns)]
            else:
                both = [_dot(_bf(jnp.concatenate([pw[i], tinv[i]], axis=0)), pwb[i])
                        for i in range(ns)]
                pw = [x[0:L] for x in both]
                tinv = [tinv[i] + both[i][L:2 * L] for i in range(ns)]
            n *= 2

        u = [_dot(_bf(tinv[i]), bd(ars[i][0:L] + akv[i][0:L])) for i in range(ns)]
        ys.extend(ars[i][L:2 * L] + akv[i][L:2 * L] + _dot(_bf(a_rb[i]), bd(u[i]))
                  for i in range(ns))
        for i, st in enumerate(streams):
            w_l = P("w_l", st)
            uv = _bf(jnp.concatenate([u[i], P("v", st)], axis=0))
            bkh = _bf(jnp.concatenate([P("b_t", st) * w_l, P("k_t", st) * w_l], axis=0))
            state_scr[sidx(st)] = s0[i] * w_l + jnp.where(st_mask, _dot_tn(uv, bkh), 0.0)

    def finish(bbs, pre, ys):
        nb = len(bbs)
        inv_n = 1.0 / HEAD
        y = [jnp.concatenate([ys[i * n_groups + g] for g in range(n_groups)], axis=1)
             for i in range(nb)]
        mean = [m * inv_n for m in _head_sums(y, gones, (True,) * nb)]
        yc = [y[i] - mean[i] for i in range(nb)]
        var = [s * inv_n for s in _head_sums([x * x for x in yc], gones, (False,) * nb)]
        for i, bb in enumerate(bbs):
            yn = yc[i] * lax.rsqrt(var[i] + LNX_EPS) * prm(_P_LW) + prm(_P_LB)
            gate = pre[i]["gate"]
            o_ref[bb] = _bf((yn + pre[i]["bonus"]) * (gate * _sigmoid(gate)))

    bbs = list(range(NB))
    pre = [prep(bb) for bb in bbs]
    ys = []
    chain(bbs, pre, ys)
    finish(bbs, pre, ys)


def _rwkv(u3, pvec, mu_wa, lora, gones, *, col_r, col_wa, d_a):
    B, S, _ = u3.shape
    L = CHUNK
    nb = RWKV_NB
    assert L == HEAD and d_a % GW == 0 and B % nb == 0 and S % L == 0
    cb = col_r // d_a
    blk = lambda off: pl.BlockSpec((nb, L, d_a), lambda b, c, off=off: (b, c, cb + off))
    full = lambda arr: pl.BlockSpec(arr.shape, lambda b, c: (0,) * arr.ndim)
    return pl.pallas_call(
        _rwkv_kernel,
        grid=(B // nb, S // L),
        in_specs=[
            blk(0), blk(1), blk(2), blk(3),
            pl.BlockSpec((nb, L, LANES), lambda b, c: (b, c, col_wa // LANES)),
            full(pvec), full(mu_wa), full(lora), full(gones),
        ],
        out_specs=pl.BlockSpec((nb, L, d_a), lambda b, c: (b, c, 0)),
        out_shape=jax.ShapeDtypeStruct((B, S, d_a), BF16),
        scratch_shapes=[
            pltpu.VMEM((nb * (d_a // GW), GW, GW), F32),
            pltpu.VMEM((nb, 8, d_a), F32),
        ],
        compiler_params=pltpu.CompilerParams(
            dimension_semantics=("parallel", "arbitrary"), vmem_limit_bytes=VMEM_LIMIT),
        name="rwkv7",
    )(u3, u3, u3, u3, u3, pvec, mu_wa, lora, gones)


def _sublane_max(x):
    for s in (4, 2, 1):
        x = jnp.maximum(x, pltpu.roll(x, s, axis=0))
    return x


def _fox_kernel(q_ref, k_ref, v_ref, g_ref, ak_ref, qg_ref, kg_ref, gm_ref,
                o_ref, kp_scr, vt_scr, qp_scr, sta_scr, stb_scr, m_scr, acc_scr, *, tq, tk):
    p = pl.program_id(1)
    S = k_ref.shape[1]
    nh = q_ref.shape[2] // HEAD
    g128 = gm_ref[...]
    lane = lax.broadcasted_iota(jnp.int32, (1, LANES), 1)
    feat = lane < HEAD

    def minus_lanes(hh):
        lo = AUG0 + 6 * (nh * p + hh)
        return (lane >= lo) & (lane < lo + 3)

    def plus_lanes(hh):
        lo = AUG0 + 6 * (nh * p + hh) + 3
        return (lane >= lo) & (lane < lo + 3)

    def head_tile(x, hh):
        t = x[:, (hh // 2) * LANES:(hh // 2 + 1) * LANES]
        return pltpu.roll(t, HEAD, axis=1) if hh % 2 else t

    for i in range(S // tk):
        rows = slice(i * tk, (i + 1) * tk)
        kb = k_ref[0, rows, :].astype(F32)
        (ssq,) = _head_sums([kb * kb], g128, (False,))
        kn = kb * lax.rsqrt(ssq * (1.0 / HEAD) + RMS_EPS) * kg_ref[...]
        aug = ak_ref[0, rows, :].astype(F32)
        for hh in range(nh):
            kp = jnp.where(feat, head_tile(kn, hh), jnp.where(plus_lanes(hh), 1.0, aug))
            kp_scr[hh, rows, :] = _bf(kp)
        vt = _bf(jnp.transpose(v_ref[0, rows, :].astype(F32)))
        for hh in range(nh):
            vt_scr[hh * VROWS:hh * VROWS + HEAD, rows] = vt[hh * HEAD:(hh + 1) * HEAD, :]
            vt_scr[hh * VROWS + HEAD:(hh + 1) * VROWS, rows] = jnp.ones((VROWS - HEAD, tk), BF16)

    def q_side(qi):
        rows = slice(qi * tq, (qi + 1) * tq)
        q = q_ref[0, rows, :].astype(F32)
        (ssq,) = _head_sums([q * q], g128, (False,))
        qn = q * lax.rsqrt(ssq * (1.0 / HEAD) + RMS_EPS) * (qg_ref[...] * (HEAD ** -0.5 * LOG2E))
        augq = ak_ref[0, rows, :].astype(F32)
        for hh in range(nh):
            qp_scr[qi % 2, hh] = _bf(jnp.where(
                feat, head_tile(qn, hh),
                jnp.where(minus_lanes(hh), 1.0, jnp.where(plus_lanes(hh), augq, 0.0))))
        m_scr[qi % 2] = jnp.full(m_scr.shape[1:], -1e30, F32)
        acc_scr[qi % 2] = jnp.zeros(acc_scr.shape[1:], F32)

    def scores(qi, j, st_ref):
        for hh in range(nh):
            st_ref[hh] = _dot_nt(kp_scr[hh, j * tk:(j + 1) * tk, :], qp_scr[qi % 2, hh])

    def softmax_pv(qi, j, st_ref):
        st = [st_ref[hh] for hh in range(nh)]
        if j == qi:
            keyi = lax.broadcasted_iota(jnp.int32, (tk, tq), 0)
            qryi = lax.broadcasted_iota(jnp.int32, (tk, tq), 1)
            st = [jnp.where(qryi >= keyi, x, -1e30) for x in st]
        st = [x.reshape(tk // 8, 8, tq) for x in st]
        m_old = [m_scr[qi % 2, hh] for hh in range(nh)]
        m_new = [jnp.maximum(m_old[hh], _sublane_max(jnp.max(st[hh], axis=0))) for hh in range(nh)]
        alpha = [jnp.exp2(m_old[hh] - m_new[hh]) for hh in range(nh)]
        pt = [_bf(jnp.exp2(st[hh] - m_new[hh][None]).reshape(tk, tq)) for hh in range(nh)]
        pv = [_dot(vt_scr[hh * VROWS:(hh + 1) * VROWS, j * tk:(j + 1) * tk], pt[hh])
              for hh in range(nh)]
        for hh in range(nh):
            m_scr[qi % 2, hh] = m_new[hh]
            acc = acc_scr[qi % 2, hh].reshape(VROWS // 8, 8, tq) * alpha[hh][None]
            acc_scr[qi % 2, hh] = acc.reshape(VROWS, tq) + pv[hh]

    def finish(qi):
        rows = slice(qi * tq, (qi + 1) * tq)
        ot = jnp.concatenate(
            [(acc_scr[qi % 2, hh, 0:HEAD, :].reshape(HEAD // 8, 8, tq)
              * (1.0 / acc_scr[qi % 2, hh, HEAD:HEAD + 8, :])[None]).reshape(HEAD, tq)
             for hh in range(nh)], axis=0)
        g = g_ref[0, rows, :].astype(F32)
        o_ref[0, rows, :] = _bf(jnp.transpose(ot) * (g * _sigmoid(g)))

    pairs = [(qi, j) for qi in range(S // tq) for j in range(qi + 1)]
    bufs = (sta_scr, stb_scr)
    for n, (qi, j) in enumerate(pairs):
        if j == 0:
            q_side(qi)
        scores(qi, j, bufs[n % 2])
        if n > 0:
            pqi, pj = pairs[n - 1]
            softmax_pv(pqi, pj, bufs[(n - 1) % 2])
            if pj == pqi:
                finish(pqi)
    qi, j = pairs[-1]
    softmax_pv(qi, j, bufs[(len(pairs) - 1) % 2])
    finish(qi)


def _fox(u3, aug, qg2, kg2, g128, *, col_q, d_b, tq, tk):
    B, S, _ = u3.shape
    fw = qg2.shape[1]
    nh = fw // HEAD
    assert S % tq == 0 and tq == tk and d_b % fw == 0
    cq, ck, cv, cg = ((col_q + i * d_b) // fw for i in range(4))
    return pl.pallas_call(
        functools.partial(_fox_kernel, tq=tq, tk=tk),
        grid=(B, d_b // fw),
        in_specs=[
            pl.BlockSpec((1, S, fw), lambda b, p: (b, 0, cq + p)),
            pl.BlockSpec((1, S, fw), lambda b, p: (b, 0, ck + p)),
            pl.BlockSpec((1, S, fw), lambda b, p: (b, 0, cv + p)),
            pl.BlockSpec((1, S, fw), lambda b, p: (b, 0, cg + p)),
            pl.BlockSpec((1, S, LANES), lambda b, p: (b, 0, 0)),
            pl.BlockSpec((1, fw), lambda b, p: (0, 0)),
            pl.BlockSpec((1, fw), lambda b, p: (0, 0)),
            pl.BlockSpec((LANES, LANES), lambda b, p: (0, 0)),
        ],
        out_specs=pl.BlockSpec((1, S, fw), lambda b, p: (b, 0, p)),
        out_shape=jax.ShapeDtypeStruct((B, S, d_b), BF16),
        scratch_shapes=[
            pltpu.VMEM((nh, S, LANES), BF16),
            pltpu.VMEM((nh * VROWS, S), BF16),
            pltpu.VMEM((2, nh, tq, LANES), BF16),
            pltpu.VMEM((nh, tk, tq), F32),
            pltpu.VMEM((nh, tk, tq), F32),
            pltpu.VMEM((2, nh, 8, tq), F32),
            pltpu.VMEM((2, nh, VROWS, tq), F32),
        ],
        compiler_params=pltpu.CompilerParams(
            dimension_semantics=("parallel", "parallel"), vmem_limit_bytes=VMEM_LIMIT),
        name="fox",
    )(u3, u3, u3, u3, aug, qg2, kg2, g128)


def _out_kernel(x_ref, ya_ref, yb_ref, ga_ref, gb_ref, woa_ref, wob_ref, wo_ref, fg_ref, o_ref,
                woa_s, wob_s, wo_s):
    @pl.when(pl.program_id(0) == 0)
    def _():
        woa_s[...] = _bf(woa_ref[...])
        wob_s[...] = _bf(wob_ref[...])
        wo_s[...] = _bf(wo_ref[...])

    for s in range(x_ref.shape[0] // OUT_SUB):
        rows = slice(s * OUT_SUB, (s + 1) * OUT_SUB)
        za = _dot(ya_ref[rows, :], woa_s[...])
        zb = _dot(yb_ref[rows, :], wob_s[...])
        merged = (_sigmoid(ga_ref[rows, :].astype(F32)) * za
                  + _sigmoid(gb_ref[rows, :].astype(F32)) * zb)
        o = x_ref[rows, :] + _dot(_bf(merged), wo_s[...])
        ms = jnp.mean(o * o, axis=-1, keepdims=True)
        o_ref[rows, :] = o * lax.rsqrt(ms + RMS_EPS) * fg_ref[...]


def _out(x2, ya2, yb2, u2, woa, wob, wo, fg, *, tm):
    T, D = x2.shape
    assert T % tm == 0 and tm % OUT_SUB == 0
    full = lambda arr: pl.BlockSpec(arr.shape, lambda i: (0,) * arr.ndim)
    return pl.pallas_call(
        _out_kernel,
        grid=(T // tm,),
        in_specs=[
            pl.BlockSpec((tm, D), lambda i: (i, 0)),
            pl.BlockSpec((tm, ya2.shape[1]), lambda i: (i, 0)),
            pl.BlockSpec((tm, yb2.shape[1]), lambda i: (i, 0)),
            pl.BlockSpec((tm, D), lambda i: (i, 0)),
            pl.BlockSpec((tm, D), lambda i: (i, 1)),
            full(woa), full(wob), full(wo), full(fg),
        ],
        out_specs=pl.BlockSpec((tm, D), lambda i: (i, 0)),
        out_shape=jax.ShapeDtypeStruct((T, D), F32),
        scratch_shapes=[pltpu.VMEM(w.shape, BF16) for w in (woa, wob, wo)],
        compiler_params=pltpu.CompilerParams(
            dimension_semantics=("arbitrary",), vmem_limit_bytes=VMEM_LIMIT),
        name="outstage",
    )(x2, ya2, yb2, u2, u2, woa, wob, wo, fg)


def _block_ones(width):
    i = jnp.arange(width) // HEAD
    return (i[:, None] == i[None, :]).astype(BF16)


def _layer(x2, B, S, norm_g, w_in, shift_mu, w_lora_up, w0, a_lora_up, a0, k_k, k_a, r_k,
           lnx_w, lnx_b, f_bias, q_norm_g, k_norm_g, w_out_a, w_out_b, w_out, out_gain):
    T, D = x2.shape
    d_a = w0.shape[0]
    d_b = w_out_b.shape[0]
    rank = w_lora_up.shape[0]
    h_b = f_bias.shape[0]
    rw = 4 * d_a + 2 * rank
    fx = 4 * d_b + h_b

    segs = [(rw + fx, 2 * D), (0, 3 * d_a), (3 * d_a + 2 * rank, d_a), (rw, 4 * d_b),
            (3 * d_a, 2 * rank), (rw + 4 * d_b, h_b)]
    w_perm_t = _regroup_rows(jnp.transpose(w_in), segs, pad_to=LANES, tc=LANES)
    col_r = 2 * D
    col_q = col_r + 4 * d_a
    col_wa = col_q + 4 * d_b

    u2, uf = _inproj(x2, norm_g.reshape(1, D), w_perm_t, tm=1024, tn=1280)
    u3 = u2.reshape(B, S, u2.shape[1])

    mu = shift_mu
    rows = [mu[:d_a], mu[d_a:2 * d_a], mu[2 * d_a:3 * d_a], mu[3 * d_a + 2 * rank:],
            w0, a0, k_k, k_a, r_k.reshape(-1), lnx_w, lnx_b]
    pvec = jnp.stack(rows + [jnp.zeros_like(w0)] * (16 - len(rows)), axis=0)
    mu_wa = mu[3 * d_a:3 * d_a + 2 * rank].reshape(1, 2 * rank)
    z = jnp.zeros((rank, d_a), F32)
    lora = _bf(jnp.concatenate(
        [jnp.concatenate([w_lora_up, z], axis=1), jnp.concatenate([z, a_lora_up], axis=1)], axis=0))
    ya = _rwkv(u3, pvec, mu_wa, lora, _block_ones(LANES), col_r=col_r, col_wa=col_wa, d_a=d_a)

    fb_pad = jnp.pad(f_bias, (0, LANES - h_b)).reshape(1, LANES)
    aug = _fprep(uf.reshape(B, S, LANES), fb_pad, _aug_selector(h_b))
    qg2 = jnp.tile(q_norm_g, FOX_HEADS).reshape(1, FOX_HEADS * HEAD)
    kg2 = jnp.tile(k_norm_g, FOX_HEADS).reshape(1, FOX_HEADS * HEAD)
    yb = _fox(u3, aug, qg2, kg2, _block_ones(LANES), col_q=col_q, d_b=d_b, tq=256, tk=256)

    return _out(x2, ya.reshape(T, d_a), yb.reshape(T, d_b), u2,
                w_out_a, w_out_b, w_out, out_gain.reshape(1, D), tm=1024)


def kernel(x, norm_g, w_in, shift_mu, w_lora_up, w0, a_lora_up, a0, k_k, k_a, r_k, lnx_w, lnx_b,
           f_bias, q_norm_g, k_norm_g, w_out_a, w_out_b, w_out, final_norm_g):
    B, S, D = x.shape
    depth = w_in.shape[0]
    assert depth == 1, "the fused output stage applies the final norm after the single layer"
    x2 = x.reshape(B * S, D)
    out = _layer(x2, B, S, norm_g[0], w_in[0], shift_mu[0], w_lora_up[0], w0[0], a_lora_up[0],
                 a0[0], k_k[0], k_a[0], r_k[0], lnx_w[0], lnx_b[0], f_bias[0], q_norm_g[0],
                 k_norm_g[0], w_out_a[0], w_out_b[0], w_out[0], final_norm_g)
    return out.reshape(B, S, D)
```

```python
import functools

import jax
import jax.numpy as jnp
from jax import lax
from jax.experimental import pallas as pl
from jax.experimental.pallas import tpu as pltpu

F32 = jnp.float32
BF16 = jnp.bfloat16

HEAD = 64
LANES = 128
RMS_EPS = 1e-6
LNX_EPS = 64e-5
CHUNK = 64
GROUP = 2
GW = GROUP * HEAD
RWKV_NB = 8
FOX_HEADS = 4
AUG0 = HEAD
NORM_ROWS = 512
OUT_SUB = 512
VROWS = HEAD + 16
LOG2E = 1.4426950408889634
DECAY_SCALE = -0.6065306597126334
VMEM_LIMIT = 56 * 1024 * 1024


def _bf(x):
    return x.astype(BF16)


def _dot(a, b):
    return jnp.dot(a, b, preferred_element_type=F32)


def _dot_nt(a, b):
    return lax.dot_general(a, b, (((1,), (1,)), ((), ())), preferred_element_type=F32)


def _dot_tn(a, b):
    return lax.dot_general(a, b, (((0,), (0,)), ((), ())), preferred_element_type=F32)


def _split2(x):
    hi = _bf(x)
    lo = _bf(x - hi.astype(F32))
    return hi, lo


def _split3(x):
    hi = _bf(x)
    r1 = x - hi.astype(F32)
    mid = _bf(r1)
    lo = _bf(r1 - mid.astype(F32))
    return hi, mid, lo


def _head_sums(xs, g, two_pass):
    m, w = xs[0].shape
    gw = g.shape[0]
    nt = w // gw
    parts = []
    for x, tp in zip(xs, two_pass):
        for piece in (_split2(x) if tp else (_bf(x),)):
            parts += [piece[:, t * gw:(t + 1) * gw] for t in range(nt)]
    r = _dot(jnp.concatenate(parts, axis=0), g)
    tile = lambda i: r[i * m:(i + 1) * m]
    outs, base = [], 0
    for tp in two_pass:
        if tp:
            cols = [tile(base + t) + tile(base + nt + t) for t in range(nt)]
        else:
            cols = [tile(base + t) for t in range(nt)]
        outs.append(jnp.concatenate(cols, axis=1))
        base += (2 if tp else 1) * nt
    return outs


def _sigmoid(x):
    return 1.0 / (1.0 + jnp.exp2(x * (-LOG2E)))


def _regroup_rows_kernel(w_ref, o_ref, *, segs):
    total = sum(width for _, width in segs)
    if total < o_ref.shape[0]:
        t0 = total // 16 * 16
        o_ref[t0:, :] = jnp.zeros((o_ref.shape[0] - t0, o_ref.shape[1]), BF16)
    dst = 0
    for src, width in segs:
        o_ref[dst:dst + width, :] = _bf(w_ref[src:src + width, :])
        dst += width


def _regroup_rows(wt, segs, *, pad_to, tc):
    rows, cols = wt.shape
    n_out = -(-sum(width for _, width in segs) // pad_to) * pad_to
    assert cols % tc == 0 and all(src % 8 == 0 for src, _ in segs)
    return pl.pallas_call(
        functools.partial(_regroup_rows_kernel, segs=tuple(segs)),
        grid=(cols // tc,),
        in_specs=[pl.BlockSpec((rows, tc), lambda i: (0, i))],
        out_specs=pl.BlockSpec((n_out, tc), lambda i: (0, i)),
        out_shape=jax.ShapeDtypeStruct((n_out, cols), BF16),
        compiler_params=pltpu.CompilerParams(
            dimension_semantics=("parallel",), vmem_limit_bytes=VMEM_LIMIT),
        name="wperm",
    )(wt)


def _inproj_kernel(x_ref, g_ref, w_ref, o_ref, of_ref, *, tn):
    n_cols = w_ref.shape[0]
    for c in range(x_ref.shape[0] // NORM_ROWS):
        rows = slice(c * NORM_ROWS, (c + 1) * NORM_ROWS)
        x = x_ref[rows, :]
        ms = jnp.mean(x * x, axis=-1, keepdims=True)
        h = _bf(x * lax.rsqrt(ms + RMS_EPS) * g_ref[...])
        for j in range(n_cols // tn):
            acc = _dot_nt(h, w_ref[j * tn:(j + 1) * tn, :])
            o_ref[rows, j * tn:(j + 1) * tn] = _bf(acc)
        of_ref[rows, :] = acc[:, tn - LANES:]


def _inproj(x2, norm_g, w_perm_t, *, tm, tn):
    T, D = x2.shape
    N = w_perm_t.shape[0]
    assert T % tm == 0 and N % tn == 0 and tm % NORM_ROWS == 0
    return pl.pallas_call(
        functools.partial(_inproj_kernel, tn=tn),
        grid=(T // tm,),
        in_specs=[
            pl.BlockSpec((tm, D), lambda i: (i, 0)),
            pl.BlockSpec((1, D), lambda i: (0, 0)),
            pl.BlockSpec((N, D), lambda i: (0, 0), pipeline_mode=pl.Buffered(1)),
        ],
        out_specs=[
            pl.BlockSpec((tm, N), lambda i: (i, 0)),
            pl.BlockSpec((tm, LANES), lambda i: (i, 0)),
        ],
        out_shape=[
            jax.ShapeDtypeStruct((T, N), BF16),
            jax.ShapeDtypeStruct((T, LANES), F32),
        ],
        compiler_params=pltpu.CompilerParams(
            dimension_semantics=("parallel",), vmem_limit_bytes=VMEM_LIMIT),
        name="inproj",
    )(x2, norm_g, w_perm_t)


def _fprep_kernel(f_ref, fb_ref, sel_ref, aug_ref, *, blk):
    S = f_ref.shape[1]
    row = lax.broadcasted_iota(jnp.int32, (blk, blk), 0)
    col = lax.broadcasted_iota(jnp.int32, (blk, blk), 1)
    tril = _bf(jnp.where(row >= col, 1.0, 0.0))
    local = []
    for i in range(S // blk):
        z = f_ref[0, i * blk:(i + 1) * blk, :] + fb_ref[...]
        lf = jnp.minimum(z, 0.0) - jnp.log1p(jnp.exp(-jnp.abs(z)))
        hi, mid, lo = _split3(lf)
        local.append(_dot(tril, hi) + _dot(tril, mid) + _dot(tril, lo))
    carry = jnp.zeros((1, LANES), F32)
    cums = []
    for x in local:
        cums.append(x + carry)
        carry = cums[-1][blk - 1:blk, :]
    for i, c in enumerate(cums):
        pieces = jnp.concatenate(_split3(c * LOG2E), axis=1)
        aug_ref[0, i * blk:(i + 1) * blk, :] = _bf(_dot(pieces, sel_ref[...]))


def _fprep(uf3, fb_pad, sel):
    B, S, _ = uf3.shape
    return pl.pallas_call(
        functools.partial(_fprep_kernel, blk=256),
        grid=(B,),
        in_specs=[
            pl.BlockSpec((1, S, LANES), lambda b: (b, 0, 0)),
            pl.BlockSpec((1, LANES), lambda b: (0, 0)),
            pl.BlockSpec(sel.shape, lambda b: (0, 0)),
        ],
        out_specs=pl.BlockSpec((1, S, LANES), lambda b: (b, 0, 0)),
        out_shape=jax.ShapeDtypeStruct((B, S, LANES), BF16),
        compiler_params=pltpu.CompilerParams(dimension_semantics=("parallel",)),
        name="fprep",
    )(uf3, fb_pad, sel)


def _aug_selector(n_heads):
    assert AUG0 + 6 * n_heads <= LANES
    r = jnp.arange(LANES)[:, None]
    c = jnp.arange(LANES)[None, :]
    blocks = []
    for i in range(3):
        m = jnp.where(c == AUG0 + 6 * r + i, -1.0, jnp.where(c == AUG0 + 6 * r + 3 + i, 1.0, 0.0))
        blocks.append(jnp.where(r < n_heads, m, 0.0))
    return _bf(jnp.concatenate(blocks, axis=0))


_P_MU_R, _P_MU_K, _P_MU_V, _P_MU_G, _P_W0, _P_A0, _P_KK, _P_KA, _P_RK, _P_LW, _P_LB = range(11)


def _shift_mix(x, carry_row, mu):
    rolled = pltpu.roll(x, shift=1, axis=0)
    head = rolled[0:8]
    row = lax.broadcasted_iota(jnp.int32, head.shape, 0)
    prev = jnp.concatenate([jnp.where(row == 0, carry_row, head), rolled[8:]], axis=0)
    return x + (prev - x) * mu


def _block_diag(x, half_masks):
    xb = _bf(x)
    n_tiles = xb.shape[1] // LANES
    zero = jnp.zeros((xb.shape[0], LANES), BF16)
    rows = []
    for h in range(xb.shape[1] // HEAD):
        t = h // 2
        piece = xb[:, t * LANES:(t + 1) * LANES] * half_masks[h % 2]
        rows.append(jnp.concatenate([piece if i == t else zero for i in range(n_tiles)], axis=1))
    return jnp.concatenate(rows, axis=0)


def _rwkv_kernel(r_ref, k_ref, v_ref, g_ref, wa_ref, pv_ref, muwa_ref, lora_ref, gm_ref,
                 o_ref, state_scr, carry_scr):
    c = pl.program_id(1)
    NB = r_ref.shape[0]
    L = r_ref.shape[1]
    DA = r_ref.shape[2]
    n_groups = DA // GW

    @pl.when(c == 0)
    def _():
        state_scr[...] = jnp.zeros_like(state_scr)
        carry_scr[...] = jnp.zeros_like(carry_scr)

    def prm(i):
        return pv_ref[i:i + 1, :]

    gones = gm_ref[...]
    row = lax.broadcasted_iota(jnp.int32, (L, L), 0)
    col = lax.broadcasted_iota(jnp.int32, (L, L), 1)
    tril = _bf(jnp.where(row >= col, 1.0, 0.0))
    lane128 = lax.broadcasted_iota(jnp.int32, (1, LANES), 1)
    half_masks = [_bf(jnp.where(lane128 // HEAD == i, 1.0, 0.0)) for i in range(2)]
    prow = lax.broadcasted_iota(jnp.int32, (L, GW), 0)
    pcol = lax.broadcasted_iota(jnp.int32, (L, GW), 1) & (HEAD - 1)
    incl_p = prow >= pcol
    strict_p = prow > pcol
    eye_p = jnp.where(prow == pcol, 1.0, 0.0)
    st_mask = (lax.broadcasted_iota(jnp.int32, (GW, GW), 0) // HEAD
               == lax.broadcasted_iota(jnp.int32, (GW, GW), 1) // HEAD)
    bd = lambda x: _block_diag(x, half_masks)

    def shift(bb):
        raw = [ref[bb].astype(F32) for ref in (r_ref, k_ref, v_ref, g_ref, wa_ref)]
        mus = [prm(_P_MU_R), prm(_P_MU_K), prm(_P_MU_V), prm(_P_MU_G), muwa_ref[...]]
        mixed = []
        for i, x in enumerate(raw):
            wdt = x.shape[1]
            mixed.append(_shift_mix(x, carry_scr[bb, i:i + 1, 0:wdt], mus[i]))
            carry_scr[bb, i:i + 1, 0:wdt] = x[L - 1:L, :]
        return mixed

    def lowrank(was):
        act = [_bf(jnp.where(lane128 < HEAD, jnp.tanh(wa), wa)) for wa in was]
        lo = _dot(jnp.concatenate(act, axis=0), lora_ref[...])
        return [lo[i * L:(i + 1) * L] for i in range(len(was))]

    def prep(mixed, lo):
        r, k, v, gate, _ = mixed
        ld = (DECAY_SCALE * LOG2E) * _sigmoid(prm(_P_W0) + lo[:, 0:DA])
        a = _sigmoid(prm(_P_A0) + lo[:, DA:2 * DA])

        kk = k * prm(_P_KK)
        k2 = k * (1.0 + (a - 1.0) * prm(_P_KA))
        ssq, bon = _head_sums([kk * kk, r * k2 * prm(_P_RK)], gones, (False, False))
        kk = kk * jnp.minimum(lax.rsqrt(ssq), 1e12)

        h3, m3, l3 = _split3(ld)
        cum = _dot(tril, h3) + _dot(tril, m3) + _dot(tril, l3)
        e_neg = jnp.exp2(-cum)
        w_l = jnp.exp2(cum[L - 1:L, :])
        a_t = -kk * jnp.exp2(cum - ld)
        r_t = r * jnp.exp2(cum)
        b_t = (kk * a) * e_neg
        k_t = k2 * e_neg
        return dict(v=v, gate=gate, bonus=bon * v, w_l=w_l, a_t=a_t, r_t=r_t, b_t=b_t, k_t=k_t)

    def chain(bbs, pre, ys):
        streams = [(i, g) for i in range(len(bbs)) for g in range(n_groups)]
        ns = len(streams)
        gsl = lambda g: slice(g * GW, (g + 1) * GW)
        P = lambda name, st: pre[st[0]][name][:, gsl(st[1])]
        sidx = lambda st: bbs[st[0]] * n_groups + st[1]

        s0 = [state_scr[sidx(st)] for st in streams]
        s0b = [_bf(x) for x in s0]
        ar = [_bf(jnp.concatenate([P("a_t", st), P("r_t", st)], axis=0)) for st in streams]
        bk = [jnp.concatenate([bd(P("b_t", st)), bd(P("k_t", st))], axis=0) for st in streams]
        pm = [_dot_nt(ar[i], bk[i]) for i in range(ns)]
        ars = [_dot_nt(ar[i], s0b[i]) for i in range(ns)]
        a_ab = [jnp.where(strict_p, p[0:L, 0:GW], 0.0) for p in pm]
        a_ak = [jnp.where(strict_p, p[0:L, GW:2 * GW], 0.0) for p in pm]
        a_rb = [jnp.where(incl_p, p[L:2 * L, 0:GW], 0.0) for p in pm]
        a_rk = [jnp.where(incl_p, p[L:2 * L, GW:2 * GW], 0.0) for p in pm]
        vbd = [bd(P("v", st)) for st in streams]
        akv = [_dot(_bf(jnp.concatenate([a_ak[i], a_rk[i]], axis=0)), vbd[i])
               for i in range(ns)]

        pw = [_dot(_bf(x), bd(x)) for x in a_ab]
        tinv = [eye_p + x for x in a_ab]
        n = 2
        while n < L:
            last = 2 * n >= L
            pwb = [bd(x) for x in pw]
            if last:
                tinv = [tinv[i] + _dot(_bf(tinv[i]), pwb[i]) for i in range(ns)]
            else:
                both = [_dot(_bf(jnp.concatenate([pw[i], tinv[i]], axis=0)), pwb[i])
                        for i in range(ns)]
                pw = [x[0:L] for x in both]
                tinv = [tinv[i] + both[i][L:2 * L] for i in range(ns)]
            n *= 2

        u = [_dot(_bf(tinv[i]), bd(ars[i][0:L] + akv[i][0:L])) for i in range(ns)]
        ys.extend(ars[i][L:2 * L] + akv[i][L:2 * L] + _dot(_bf(a_rb[i]), bd(u[i]))
                  for i in range(ns))
        for i, st in enumerate(streams):
            w_l = P("w_l", st)
            uv = _bf(jnp.concatenate([u[i], P("v", st)], axis=0))
            bkh = _bf(jnp.concatenate([P("b_t", st) * w_l, P("k_t", st) * w_l], axis=0))
            state_scr[sidx(st)] = s0[i] * w_l + jnp.where(st_mask, _dot_tn(uv, bkh), 0.0)

    def finish(bbs, pre, ys):
        nb = len(bbs)
        inv_n = 1.0 / HEAD
        y = [jnp.concatenate([ys[i * n_groups + g] for g in range(n_groups)], axis=1)
             for i in range(nb)]
        mean = [m * inv_n for m in _head_sums(y, gones, (True,) * nb)]
        yc = [y[i] - mean[i] for i in range(nb)]
        var = [s * inv_n for s in _head_sums([x * x for x in yc], gones, (False,) * nb)]
        for i, bb in enumerate(bbs):
            yn = yc[i] * lax.rsqrt(var[i] + LNX_EPS) * prm(_P_LW) + prm(_P_LB)
            gate = pre[i]["gate"]
            o_ref[bb] = _bf((yn + pre[i]["bonus"]) * (gate * _sigmoid(gate)))

    bbs = list(range(NB))
    mixed = [shift(bb) for bb in bbs]
    los = lowrank([m[4] for m in mixed])
    pre = [prep(mixed[bb], los[bb]) for bb in bbs]
    ys = []
    chain(bbs, pre, ys)
    finish(bbs, pre, ys)


def _rwkv(u3, pvec, mu_wa, lora, gones, *, col_r, col_wa, d_a):
    B, S, _ = u3.shape
    L = CHUNK
    nb = RWKV_NB
    assert L == HEAD and d_a % GW == 0 and B % nb == 0 and S % L == 0
    cb = col_r // d_a
    blk = lambda off: pl.BlockSpec((nb, L, d_a), lambda b, c, off=off: (b, c, cb + off))
    full = lambda arr: pl.BlockSpec(arr.shape, lambda b, c: (0,) * arr.ndim)
    return pl.pallas_call(
        _rwkv_kernel,
        grid=(B // nb, S // L),
        in_specs=[
            blk(0), blk(1), blk(2), blk(3),
            pl.BlockSpec((nb, L, LANES), lambda b, c: (b, c, col_wa // LANES)),
            full(pvec), full(mu_wa), full(lora), full(gones),
        ],
        out_specs=pl.BlockSpec((nb, L, d_a), lambda b, c: (b, c, 0)),
        out_shape=jax.ShapeDtypeStruct((B, S, d_a), BF16),
        scratch_shapes=[
            pltpu.VMEM((nb * (d_a // GW), GW, GW), F32),
            pltpu.VMEM((nb, 8, d_a), F32),
        ],
        compiler_params=pltpu.CompilerParams(
            dimension_semantics=("parallel", "arbitrary"), vmem_limit_bytes=VMEM_LIMIT),
        name="rwkv7",
    )(u3, u3, u3, u3, u3, pvec, mu_wa, lora, gones)


def _sublane_max(x):
    for s in (4, 2, 1):
        x = jnp.maximum(x, pltpu.roll(x, s, axis=0))
    return x


def _fox_kernel(q_ref, k_ref, v_ref, g_ref, ak_ref, qg_ref, kg_ref, gm_ref,
                o_ref, kp_scr, vt_scr, qp_scr, sta_scr, stb_scr, m_scr, acc_scr, *, tq, tk):
    p = pl.program_id(1)
    S = k_ref.shape[1]
    nh = q_ref.shape[2] // HEAD
    g128 = gm_ref[...]
    lane = lax.broadcasted_iota(jnp.int32, (1, LANES), 1)
    feat = lane < HEAD

    def minus_lanes(hh):
        lo = AUG0 + 6 * (nh * p + hh)
        return (lane >= lo) & (lane < lo + 3)

    def plus_lanes(hh):
        lo = AUG0 + 6 * (nh * p + hh) + 3
        return (lane >= lo) & (lane < lo + 3)

    def head_tile(x, hh):
        t = x[:, (hh // 2) * LANES:(hh // 2 + 1) * LANES]
        return pltpu.roll(t, HEAD, axis=1) if hh % 2 else t

    for i in range(S // tk):
        rows = slice(i * tk, (i + 1) * tk)
        kb = k_ref[0, rows, :].astype(F32)
        (ssq,) = _head_sums([kb * kb], g128, (False,))
        kn = kb * lax.rsqrt(ssq * (1.0 / HEAD) + RMS_EPS) * kg_ref[...]
        aug = ak_ref[0, rows, :].astype(F32)
        for hh in range(nh):
            kp = jnp.where(feat, head_tile(kn, hh), jnp.where(plus_lanes(hh), 1.0, aug))
            kp_scr[hh, rows, :] = _bf(kp)
        vt = _bf(jnp.transpose(v_ref[0, rows, :].astype(F32)))
        for hh in range(nh):
            vt_scr[hh * VROWS:hh * VROWS + HEAD, rows] = vt[hh * HEAD:(hh + 1) * HEAD, :]
            vt_scr[hh * VROWS + HEAD:(hh + 1) * VROWS, rows] = jnp.ones((VROWS - HEAD, tk), BF16)

    def q_side(qi):
        rows = slice(qi * tq, (qi + 1) * tq)
        q = q_ref[0, rows, :].astype(F32)
        (ssq,) = _head_sums([q * q], g128, (False,))
        qn = q * lax.rsqrt(ssq * (1.0 / HEAD) + RMS_EPS) * (qg_ref[...] * (HEAD ** -0.5 * LOG2E))
        augq = ak_ref[0, rows, :].astype(F32)
        for hh in range(nh):
            qp_scr[qi % 2, hh] = _bf(jnp.where(
                feat, head_tile(qn, hh),
                jnp.where(minus_lanes(hh), 1.0, jnp.where(plus_lanes(hh), augq, 0.0))))
        m_scr[qi % 2] = jnp.full(m_scr.shape[1:], -1e30, F32)
        acc_scr[qi % 2] = jnp.zeros(acc_scr.shape[1:], F32)

    def scores(qi, j, st_ref):
        for hh in range(nh):
            st_ref[hh] = _dot_nt(kp_scr[hh, j * tk:(j + 1) * tk, :], qp_scr[qi % 2, hh])

    def softmax_pv(qi, j, st_ref):
        st = [st_ref[hh] for hh in range(nh)]
        if j == qi:
            keyi = lax.broadcasted_iota(jnp.int32, (tk, tq), 0)
            qryi = lax.broadcasted_iota(jnp.int32, (tk, tq), 1)
            st = [jnp.where(qryi >= keyi, x, -1e30) for x in st]
        st = [x.reshape(tk // 8, 8, tq) for x in st]
        m_old = [m_scr[qi % 2, hh] for hh in range(nh)]
        m_new = [jnp.maximum(m_old[hh], _sublane_max(jnp.max(st[hh], axis=0))) for hh in range(nh)]
        alpha = [jnp.exp2(m_old[hh] - m_new[hh]) for hh in range(nh)]
        pt = [_bf(jnp.exp2(st[hh] - m_new[hh][None]).reshape(tk, tq)) for hh in range(nh)]
        pv = [_dot(vt_scr[hh * VROWS:(hh + 1) * VROWS, j * tk:(j + 1) * tk], pt[hh])
              for hh in range(nh)]
        for hh in range(nh):
            m_scr[qi % 2, hh] = m_new[hh]
            acc = acc_scr[qi % 2, hh].reshape(VROWS // 8, 8, tq) * alpha[hh][None]
            acc_scr[qi % 2, hh] = acc.reshape(VROWS, tq) + pv[hh]

    def finish(qi):
        rows = slice(qi * tq, (qi + 1) * tq)
        ot = jnp.concatenate(
            [(acc_scr[qi % 2, hh, 0:HEAD, :].reshape(HEAD // 8, 8, tq)
              * (1.0 / acc_scr[qi % 2, hh, HEAD:HEAD + 8, :])[None]).reshape(HEAD, tq)
             for hh in range(nh)], axis=0)
        g = g_ref[0, rows, :].astype(F32)
        o_ref[0, rows, :] = _bf(jnp.transpose(ot) * (g * _sigmoid(g)))

    pairs = [(qi, j) for qi in range(S // tq) for j in range(qi + 1)]
    bufs = (sta_scr, stb_scr)
    for n, (qi, j) in enumerate(pairs):
        if j == 0:
            q_side(qi)
        scores(qi, j, bufs[n % 2])
        if n > 0:
            pqi, pj = pairs[n - 1]
            softmax_pv(pqi, pj, bufs[(n - 1) % 2])
            if pj == pqi:
                finish(pqi)
    qi, j = pairs[-1]
    softmax_pv(qi, j, bufs[(len(pairs) - 1) % 2])
    finish(qi)


def _fox(u3, aug, qg2, kg2, g128, *, col_q, d_b, tq, tk):
    B, S, _ = u3.shape
    fw = qg2.shape[1]
    nh = fw // HEAD
    assert S % tq == 0 and tq == tk and d_b % fw == 0
    cq, ck, cv, cg = ((col_q + i * d_b) // fw for i in range(4))
    return pl.pallas_call(
        functools.partial(_fox_kernel, tq=tq, tk=tk),
        grid=(B, d_b // fw),
        in_specs=[
            pl.BlockSpec((1, S, fw), lambda b, p: (b, 0, cq + p)),
            pl.BlockSpec((1, S, fw), lambda b, p: (b, 0, ck + p)),
            pl.BlockSpec((1, S, fw), lambda b, p: (b, 0, cv + p)),
            pl.BlockSpec((1, S, fw), lambda b, p: (b, 0, cg + p)),
            pl.BlockSpec((1, S, LANES), lambda b, p: (b, 0, 0)),
            pl.BlockSpec((1, fw), lambda b, p: (0, 0)),
            pl.BlockSpec((1, fw), lambda b, p: (0, 0)),
            pl.BlockSpec((LANES, LANES), lambda b, p: (0, 0)),
        ],
        out_specs=pl.BlockSpec((1, S, fw), lambda b, p: (b, 0, p)),
        out_shape=jax.ShapeDtypeStruct((B, S, d_b), BF16),
        scratch_shapes=[
            pltpu.VMEM((nh, S, LANES), BF16),
            pltpu.VMEM((nh * VROWS, S), BF16),
            pltpu.VMEM((2, nh, tq, LANES), BF16),
            pltpu.VMEM((nh, tk, tq), F32),
            pltpu.VMEM((nh, tk, tq), F32),
            pltpu.VMEM((2, nh, 8, tq), F32),
            pltpu.VMEM((2, nh, VROWS, tq), F32),
        ],
        compiler_params=pltpu.CompilerParams(
            dimension_semantics=("parallel", "parallel"), vmem_limit_bytes=VMEM_LIMIT),
        name="fox",
    )(u3, u3, u3, u3, aug, qg2, kg2, g128)


def _out_kernel(x_ref, ya_ref, yb_ref, ga_ref, gb_ref, woa_ref, wob_ref, wo_ref, fg_ref, o_ref,
                woa_s, wob_s, wo_s):
    @pl.when(pl.program_id(0) == 0)
    def _():
        woa_s[...] = _bf(woa_ref[...])
        wob_s[...] = _bf(wob_ref[...])
        wo_s[...] = _bf(wo_ref[...])

    for s in range(x_ref.shape[0] // OUT_SUB):
        rows = slice(s * OUT_SUB, (s + 1) * OUT_SUB)
        za = _dot(ya_ref[rows, :], woa_s[...])
        zb = _dot(yb_ref[rows, :], wob_s[...])
        merged = (_sigmoid(ga_ref[rows, :].astype(F32)) * za
                  + _sigmoid(gb_ref[rows, :].astype(F32)) * zb)
        o = x_ref[rows, :] + _dot(_bf(merged), wo_s[...])
        ms = jnp.mean(o * o, axis=-1, keepdims=True)
        o_ref[rows, :] = o * lax.rsqrt(ms + RMS_EPS) * fg_ref[...]


def _out(x2, ya2, yb2, u2, woa, wob, wo, fg, *, tm):
    T, D = x2.shape
    assert T % tm == 0 and tm % OUT_SUB == 0
    full = lambda arr: pl.BlockSpec(arr.shape, lambda i: (0,) * arr.ndim)
    return pl.pallas_call(
        _out_kernel,
        grid=(T // tm,),
        in_specs=[
            pl.BlockSpec((tm, D), lambda i: (i, 0)),
            pl.BlockSpec((tm, ya2.shape[1]), lambda i: (i, 0)),
            pl.BlockSpec((tm, yb2.shape[1]), lambda i: (i, 0)),
            pl.BlockSpec((tm, D), lambda i: (i, 0)),
            pl.BlockSpec((tm, D), lambda i: (i, 1)),
            full(woa), full(wob), full(wo), full(fg),
        ],
        out_specs=pl.BlockSpec((tm, D), lambda i: (i, 0)),
        out_shape=jax.ShapeDtypeStruct((T, D), F32),
        scratch_shapes=[pltpu.VMEM(w.shape, BF16) for w in (woa, wob, wo)],
        compiler_params=pltpu.CompilerParams(
            dimension_semantics=("arbitrary",), vmem_limit_bytes=VMEM_LIMIT),
        name="outstage",
    )(x2, ya2, yb2, u2, u2, woa, wob, wo, fg)


def _block_ones(width):
    i = jnp.arange(width) // HEAD
    return (i[:, None] == i[None, :]).astype(BF16)


def _layer(x2, B, S, norm_g, w_in, shift_mu, w_lora_up, w0, a_lora_up, a0, k_k, k_a, r_k,
           lnx_w, lnx_b, f_bias, q_norm_g, k_norm_g, w_out_a, w_out_b, w_out, out_gain):
    T, D = x2.shape
    d_a = w0.shape[0]
    d_b = w_out_b.shape[0]
    rank = w_lora_up.shape[0]
    h_b = f_bias.shape[0]
    rw = 4 * d_a + 2 * rank
    fx = 4 * d_b + h_b

    segs = [(rw + fx, 2 * D), (0, 3 * d_a), (3 * d_a + 2 * rank, d_a), (rw, 4 * d_b),
            (3 * d_a, 2 * rank), (rw + 4 * d_b, h_b)]
    w_perm_t = _regroup_rows(jnp.transpose(w_in), segs, pad_to=LANES, tc=LANES)
    col_r = 2 * D
    col_q = col_r + 4 * d_a
    col_wa = col_q + 4 * d_b

    u2, uf = _inproj(x2, norm_g.reshape(1, D), w_perm_t, tm=1024, tn=1280)
    u3 = u2.reshape(B, S, u2.shape[1])

    mu = shift_mu
    rows = [mu[:d_a], mu[d_a:2 * d_a], mu[2 * d_a:3 * d_a], mu[3 * d_a + 2 * rank:],
            w0, a0, k_k, k_a, r_k.reshape(-1), lnx_w, lnx_b]
    pvec = jnp.stack(rows + [jnp.zeros_like(w0)] * (16 - len(rows)), axis=0)
    mu_wa = mu[3 * d_a:3 * d_a + 2 * rank].reshape(1, 2 * rank)
    z = jnp.zeros((rank, d_a), F32)
    lora = _bf(jnp.concatenate(
        [jnp.concatenate([w_lora_up, z], axis=1), jnp.concatenate([z, a_lora_up], axis=1)], axis=0))
    ya = _rwkv(u3, pvec, mu_wa, lora, _block_ones(LANES), col_r=col_r, col_wa=col_wa, d_a=d_a)

    fb_pad = jnp.pad(f_bias, (0, LANES - h_b)).reshape(1, LANES)
    aug = _fprep(uf.reshape(B, S, LANES), fb_pad, _aug_selector(h_b))
    qg2 = jnp.tile(q_norm_g, FOX_HEADS).reshape(1, FOX_HEADS * HEAD)
    kg2 = jnp.tile(k_norm_g, FOX_HEADS).reshape(1, FOX_HEADS * HEAD)
    yb = _fox(u3, aug, qg2, kg2, _block_ones(LANES), col_q=col_q, d_b=d_b, tq=256, tk=256)

    return _out(x2, ya.reshape(T, d_a), yb.reshape(T, d_b), u2,
                w_out_a, w_out_b, w_out, out_gain.reshape(1, D), tm=1024)


def kernel(x, norm_g, w_in, shift_mu, w_lora_up, w0, a_lora_up, a0, k_k, k_a, r_k, lnx_w, lnx_b,
           f_bias, q_norm_g, k_norm_g, w_out_a, w_out_b, w_out, final_norm_g):
    B, S, D = x.shape
    depth = w_in.shape[0]
    assert depth == 1, "the fused output stage applies the final norm after the single layer"
    x2 = x.reshape(B * S, D)
    out = _layer(x2, B, S, norm_g[0], w_in[0], shift_mu[0], w_lora_up[0], w0[0], a_lora_up[0],
                 a0[0], k_k[0], k_a[0], r_k[0], lnx_w[0], lnx_b[0], f_bias[0], q_norm_g[0],
                 k_norm_g[0], w_out_a[0], w_out_b[0], w_out[0], final_norm_g)
    return out.reshape(B, S, D)
```

```python
import functools

import jax
import jax.numpy as jnp
from jax import lax
from jax.experimental import pallas as pl
from jax.experimental.pallas import tpu as pltpu

F32 = jnp.float32
BF16 = jnp.bfloat16

HEAD = 64
LANES = 128
RMS_EPS = 1e-6
LNX_EPS = 64e-5
CHUNK = 64
GROUP = 2
GW = GROUP * HEAD
RWKV_NB = 8
FOX_HEADS = 4
AUG0 = HEAD
NORM_ROWS = 512
OUT_SUB = 512
VROWS = HEAD + 16
LOG2E = 1.4426950408889634
DECAY_SCALE = -0.6065306597126334
VMEM_LIMIT = 56 * 1024 * 1024


def _bf(x):
    return x.astype(BF16)


def _dot(a, b):
    return jnp.dot(a, b, preferred_element_type=F32)


def _dot_nt(a, b):
    return lax.dot_general(a, b, (((1,), (1,)), ((), ())), preferred_element_type=F32)


def _dot_tn(a, b):
    return lax.dot_general(a, b, (((0,), (0,)), ((), ())), preferred_element_type=F32)


def _split2(x):
    hi = _bf(x)
    lo = _bf(x - hi.astype(F32))
    return hi, lo


def _split3(x):
    hi = _bf(x)
    r1 = x - hi.astype(F32)
    mid = _bf(r1)
    lo = _bf(r1 - mid.astype(F32))
    return hi, mid, lo


def _head_sums(xs, g, two_pass):
    m, w = xs[0].shape
    gw = g.shape[0]
    nt = w // gw
    parts = []
    for x, tp in zip(xs, two_pass):
        for piece in (_split2(x) if tp else (_bf(x),)):
            parts += [piece[:, t * gw:(t + 1) * gw] for t in range(nt)]
    r = _dot(jnp.concatenate(parts, axis=0), g)
    tile = lambda i: r[i * m:(i + 1) * m]
    outs, base = [], 0
    for tp in two_pass:
        if tp:
            cols = [tile(base + t) + tile(base + nt + t) for t in range(nt)]
        else:
            cols = [tile(base + t) for t in range(nt)]
        outs.append(jnp.concatenate(cols, axis=1))
        base += (2 if tp else 1) * nt
    return outs


def _sigmoid(x):
    return 1.0 / (1.0 + jnp.exp2(x * (-LOG2E)))


def _regroup_rows_kernel(w_ref, o_ref, *, segs):
    total = sum(width for _, width in segs)
    if total < o_ref.shape[0]:
        t0 = total // 16 * 16
        o_ref[t0:, :] = jnp.zeros((o_ref.shape[0] - t0, o_ref.shape[1]), BF16)
    dst = 0
    for src, width in segs:
        o_ref[dst:dst + width, :] = _bf(w_ref[src:src + width, :])
        dst += width


def _regroup_rows(wt, segs, *, pad_to, tc):
    rows, cols = wt.shape
    n_out = -(-sum(width for _, width in segs) // pad_to) * pad_to
    assert cols % tc == 0 and all(src % 8 == 0 for src, _ in segs)
    return pl.pallas_call(
        functools.partial(_regroup_rows_kernel, segs=tuple(segs)),
        grid=(cols // tc,),
        in_specs=[pl.BlockSpec((rows, tc), lambda i: (0, i))],
        out_specs=pl.BlockSpec((n_out, tc), lambda i: (0, i)),
        out_shape=jax.ShapeDtypeStruct((n_out, cols), BF16),
        compiler_params=pltpu.CompilerParams(
            dimension_semantics=("parallel",), vmem_limit_bytes=VMEM_LIMIT),
        name="wperm",
    )(wt)


def _inproj_kernel(x_ref, g_ref, w_ref, o_ref, of_ref, *, tn):
    n_cols = w_ref.shape[0]
    for c in range(x_ref.shape[0] // NORM_ROWS):
        rows = slice(c * NORM_ROWS, (c + 1) * NORM_ROWS)
        x = x_ref[rows, :]
        ms = jnp.mean(x * x, axis=-1, keepdims=True)
        h = _bf(x * lax.rsqrt(ms + RMS_EPS) * g_ref[...])
        for j in range(n_cols // tn):
            acc = _dot_nt(h, w_ref[j * tn:(j + 1) * tn, :])
            o_ref[rows, j * tn:(j + 1) * tn] = _bf(acc)
        of_ref[rows, :] = acc[:, tn - LANES:]


def _inproj(x2, norm_g, w_perm_t, *, tm, tn):
    T, D = x2.shape
    N = w_perm_t.shape[0]
    assert T % tm == 0 and N % tn == 0 and tm % NORM_ROWS == 0
    return pl.pallas_call(
        functools.partial(_inproj_kernel, tn=tn),
        grid=(T // tm,),
        in_specs=[
            pl.BlockSpec((tm, D), lambda i: (i, 0)),
            pl.BlockSpec((1, D), lambda i: (0, 0)),
            pl.BlockSpec((N, D), lambda i: (0, 0), pipeline_mode=pl.Buffered(1)),
        ],
        out_specs=[
            pl.BlockSpec((tm, N), lambda i: (i, 0)),
            pl.BlockSpec((tm, LANES), lambda i: (i, 0)),
        ],
        out_shape=[
            jax.ShapeDtypeStruct((T, N), BF16),
            jax.ShapeDtypeStruct((T, LANES), F32),
        ],
        compiler_params=pltpu.CompilerParams(
            dimension_semantics=("parallel",), vmem_limit_bytes=VMEM_LIMIT),
        name="inproj",
    )(x2, norm_g, w_perm_t)


def _fprep_kernel(f_ref, fb_ref, sel_ref, aug_ref, *, blk):
    S = f_ref.shape[1]
    row = lax.broadcasted_iota(jnp.int32, (blk, blk), 0)
    col = lax.broadcasted_iota(jnp.int32, (blk, blk), 1)
    tril = _bf(jnp.where(row >= col, 1.0, 0.0))
    local = []
    for i in range(S // blk):
        z = f_ref[0, i * blk:(i + 1) * blk, :] + fb_ref[...]
        lf = jnp.minimum(z, 0.0) - jnp.log1p(jnp.exp(-jnp.abs(z)))
        hi, mid, lo = _split3(lf)
        local.append(_dot(tril, hi) + _dot(tril, mid) + _dot(tril, lo))
    carry = jnp.zeros((1, LANES), F32)
    cums = []
    for x in local:
        cums.append(x + carry)
        carry = cums[-1][blk - 1:blk, :]
    for i, c in enumerate(cums):
        pieces = jnp.concatenate(_split3(c * LOG2E), axis=1)
        aug_ref[0, i * blk:(i + 1) * blk, :] = _bf(_dot(pieces, sel_ref[...]))


def _fprep(uf3, fb_pad, sel):
    B, S, _ = uf3.shape
    return pl.pallas_call(
        functools.partial(_fprep_kernel, blk=256),
        grid=(B,),
        in_specs=[
            pl.BlockSpec((1, S, LANES), lambda b: (b, 0, 0)),
            pl.BlockSpec((1, LANES), lambda b: (0, 0)),
            pl.BlockSpec(sel.shape, lambda b: (0, 0)),
        ],
        out_specs=pl.BlockSpec((1, S, LANES), lambda b: (b, 0, 0)),
        out_shape=jax.ShapeDtypeStruct((B, S, LANES), BF16),
        compiler_params=pltpu.CompilerParams(dimension_semantics=("parallel",)),
        name="fprep",
    )(uf3, fb_pad, sel)


def _aug_selector(n_heads):
    assert AUG0 + 6 * n_heads <= LANES
    r = jnp.arange(LANES)[:, None]
    c = jnp.arange(LANES)[None, :]
    blocks = []
    for i in range(3):
        m = jnp.where(c == AUG0 + 6 * r + i, -1.0, jnp.where(c == AUG0 + 6 * r + 3 + i, 1.0, 0.0))
        blocks.append(jnp.where(r < n_heads, m, 0.0))
    return _bf(jnp.concatenate(blocks, axis=0))


_P_MU_R, _P_MU_K, _P_MU_V, _P_MU_G, _P_W0, _P_A0, _P_KK, _P_KA, _P_RK, _P_LW, _P_LB = range(11)


def _shift_mix(x, carry_row, mu):
    rolled = pltpu.roll(x, shift=1, axis=0)
    head = rolled[0:8]
    row = lax.broadcasted_iota(jnp.int32, head.shape, 0)
    prev = jnp.concatenate([jnp.where(row == 0, carry_row, head), rolled[8:]], axis=0)
    return x + (prev - x) * mu


def _block_diag(x, half_masks):
    xb = _bf(x)
    n_tiles = xb.shape[1] // LANES
    zero = jnp.zeros((xb.shape[0], LANES), BF16)
    rows = []
    for h in range(xb.shape[1] // HEAD):
        t = h // 2
        piece = xb[:, t * LANES:(t + 1) * LANES] * half_masks[h % 2]
        rows.append(jnp.concatenate([piece if i == t else zero for i in range(n_tiles)], axis=1))
    return jnp.concatenate(rows, axis=0)


def _rwkv_kernel(r_ref, k_ref, v_ref, g_ref, wa_ref, pv_ref, muwa_ref, lora_ref, gm_ref,
                 o_ref, state_scr, carry_scr):
    c = pl.program_id(1)
    NB = r_ref.shape[0]
    L = r_ref.shape[1]
    DA = r_ref.shape[2]
    n_groups = DA // GW

    @pl.when(c == 0)
    def _():
        state_scr[...] = jnp.zeros_like(state_scr)
        carry_scr[...] = jnp.zeros_like(carry_scr)

    def prm(i):
        return pv_ref[i:i + 1, :]

    gones = gm_ref[...]
    row = lax.broadcasted_iota(jnp.int32, (L, L), 0)
    col = lax.broadcasted_iota(jnp.int32, (L, L), 1)
    tril = _bf(jnp.where(row >= col, 1.0, 0.0))
    lane128 = lax.broadcasted_iota(jnp.int32, (1, LANES), 1)
    half_masks = [_bf(jnp.where(lane128 // HEAD == i, 1.0, 0.0)) for i in range(2)]
    prow = lax.broadcasted_iota(jnp.int32, (L, GW), 0)
    pcol = lax.broadcasted_iota(jnp.int32, (L, GW), 1) & (HEAD - 1)
    incl_p = prow >= pcol
    strict_p = prow > pcol
    eye_p = jnp.where(prow == pcol, 1.0, 0.0)
    st_mask = (lax.broadcasted_iota(jnp.int32, (GW, GW), 0) // HEAD
               == lax.broadcasted_iota(jnp.int32, (GW, GW), 1) // HEAD)
    bd = lambda x: _block_diag(x, half_masks)

    def shift(bb):
        raw = [ref[bb].astype(F32) for ref in (r_ref, k_ref, v_ref, g_ref, wa_ref)]
        mus = [prm(_P_MU_R), prm(_P_MU_K), prm(_P_MU_V), prm(_P_MU_G), muwa_ref[...]]
        mixed = []
        for i, x in enumerate(raw):
            wdt = x.shape[1]
            mixed.append(_shift_mix(x, carry_scr[bb, i:i + 1, 0:wdt], mus[i]))
            carry_scr[bb, i:i + 1, 0:wdt] = x[L - 1:L, :]
        return mixed

    def lowrank(was):
        act = [_bf(jnp.where(lane128 < HEAD, jnp.tanh(wa), wa)) for wa in was]
        lo = _dot(jnp.concatenate(act, axis=0), lora_ref[...])
        return [lo[i * L:(i + 1) * L] for i in range(len(was))]

    def prep(mixed, lo):
        r, k, v, gate, _ = mixed
        ld = (DECAY_SCALE * LOG2E) * _sigmoid(prm(_P_W0) + lo[:, 0:DA])
        a = _sigmoid(prm(_P_A0) + lo[:, DA:2 * DA])

        kk = k * prm(_P_KK)
        k2 = k * (1.0 + (a - 1.0) * prm(_P_KA))
        ssq, bon = _head_sums([kk * kk, r * k2 * prm(_P_RK)], gones, (False, False))
        kk = kk * jnp.minimum(lax.rsqrt(ssq), 1e12)

        h3, m3, l3 = _split3(ld)
        cum = _dot(tril, h3) + _dot(tril, m3) + _dot(tril, l3)
        e_neg = jnp.exp2(-cum)
        w_l = jnp.exp2(cum[L - 1:L, :])
        a_t = -kk * jnp.exp2(cum - ld)
        r_t = r * jnp.exp2(cum)
        b_t = (kk * a) * e_neg
        k_t = k2 * e_neg
        return dict(v=_bf(v), gate=gate, bonus=bon * v, w_l=w_l,
                    ar=_bf(jnp.concatenate([a_t, r_t], axis=0)), b_t=_bf(b_t), k_t=_bf(k_t),
                    bkh=_bf(jnp.concatenate([b_t * w_l, k_t * w_l], axis=0)))

    def chain(bbs, pre, ys):
        streams = [(i, g) for i in range(len(bbs)) for g in range(n_groups)]
        ns = len(streams)
        gsl = lambda g: slice(g * GW, (g + 1) * GW)
        P = lambda name, st: pre[st[0]][name][:, gsl(st[1])]
        sidx = lambda st: bbs[st[0]] * n_groups + st[1]

        s0 = [state_scr[sidx(st)] for st in streams]
        s0b = [_bf(x) for x in s0]
        ar = [P("ar", st) for st in streams]
        bk = [jnp.concatenate([bd(P("b_t", st)), bd(P("k_t", st))], axis=0) for st in streams]
        pm = [_dot_nt(ar[i], bk[i]) for i in range(ns)]
        ars = [_dot_nt(ar[i], s0b[i]) for i in range(ns)]
        a_ab = [jnp.where(strict_p, p[0:L, 0:GW], 0.0) for p in pm]
        a_ak = [jnp.where(strict_p, p[0:L, GW:2 * GW], 0.0) for p in pm]
        a_rb = [jnp.where(incl_p, p[L:2 * L, 0:GW], 0.0) for p in pm]
        a_rk = [jnp.where(incl_p, p[L:2 * L, GW:2 * GW], 0.0) for p in pm]
        vbd = [bd(P("v", st)) for st in streams]
        akv = [_dot(_bf(jnp.concatenate([a_ak[i], a_rk[i]], axis=0)), vbd[i])
               for i in range(ns)]

        pw = [_dot(_bf(x), bd(x)) for x in a_ab]
        tinv = [eye_p + x for x in a_ab]
        n = 2
        while n < L:
            last = 2 * n >= L
            pwb = [bd(x) for x in pw]
            if last:
                tinv = [tinv[i] + _dot(_bf(tinv[i]), pwb[i]) for i in range(ns)]
            else:
                both = [_dot(_bf(jnp.concatenate([pw[i], tinv[i]], axis=0)), pwb[i])
                        for i in range(ns)]
                pw = [x[0:L] for x in both]
                tinv = [tinv[i] + both[i][L:2 * L] for i in range(ns)]
            n *= 2

        u = [_dot(_bf(tinv[i]), bd(ars[i][0:L] + akv[i][0:L])) for i in range(ns)]
        ys.extend(ars[i][L:2 * L] + akv[i][L:2 * L] + _dot(_bf(a_rb[i]), bd(u[i]))
                  for i in range(ns))
        for i, st in enumerate(streams):
            w_l = P("w_l", st)
            uv = jnp.concatenate([_bf(u[i]), P("v", st)], axis=0)
            state_scr[sidx(st)] = (s0[i] * w_l
                                   + jnp.where(st_mask, _dot_tn(uv, P("bkh", st)), 0.0))

    def finish(bbs, pre, ys):
        nb = len(bbs)
        inv_n = 1.0 / HEAD
        y = [jnp.concatenate([ys[i * n_groups + g] for g in range(n_groups)], axis=1)
             for i in range(nb)]
        mean = [m * inv_n for m in _head_sums(y, gones, (True,) * nb)]
        yc = [y[i] - mean[i] for i in range(nb)]
        var = [s * inv_n for s in _head_sums([x * x for x in yc], gones, (False,) * nb)]
        for i, bb in enumerate(bbs):
            yn = yc[i] * lax.rsqrt(var[i] + LNX_EPS) * prm(_P_LW) + prm(_P_LB)
            gate = pre[i]["gate"]
            o_ref[bb] = _bf((yn + pre[i]["bonus"]) * (gate * _sigmoid(gate)))

    bbs = list(range(NB))
    mixed = [shift(bb) for bb in bbs]
    los = lowrank([m[4] for m in mixed])
    pre = [prep(mixed[bb], los[bb]) for bb in bbs]
    ys = []
    chain(bbs, pre, ys)
    finish(bbs, pre, ys)


def _rwkv(u3, pvec, mu_wa, lora, gones, *, col_r, col_wa, d_a):
    B, S, _ = u3.shape
    L = CHUNK
    nb = RWKV_NB
    assert L == HEAD and d_a % GW == 0 and B % nb == 0 and S % L == 0
    cb = col_r // d_a
    blk = lambda off: pl.BlockSpec((nb, L, d_a), lambda b, c, off=off: (b, c, cb + off))
    full = lambda arr: pl.BlockSpec(arr.shape, lambda b, c: (0,) * arr.ndim)
    return pl.pallas_call(
        _rwkv_kernel,
        grid=(B // nb, S // L),
        in_specs=[
            blk(0), blk(1), blk(2), blk(3),
            pl.BlockSpec((nb, L, LANES), lambda b, c: (b, c, col_wa // LANES)),
            full(pvec), full(mu_wa), full(lora), full(gones),
        ],
        out_specs=pl.BlockSpec((nb, L, d_a), lambda b, c: (b, c, 0)),
        out_shape=jax.ShapeDtypeStruct((B, S, d_a), BF16),
        scratch_shapes=[
            pltpu.VMEM((nb * (d_a // GW), GW, GW), F32),
            pltpu.VMEM((nb, 8, d_a), F32),
        ],
        compiler_params=pltpu.CompilerParams(
            dimension_semantics=("parallel", "arbitrary"), vmem_limit_bytes=VMEM_LIMIT),
        name="rwkv7",
    )(u3, u3, u3, u3, u3, pvec, mu_wa, lora, gones)


def _sublane_max(x):
    for s in (4, 2, 1):
        x = jnp.maximum(x, pltpu.roll(x, s, axis=0))
    return x


def _fox_kernel(q_ref, k_ref, v_ref, g_ref, ak_ref, qg_ref, kg_ref, gm_ref,
                o_ref, kp_scr, vt_scr, qp_scr, sta_scr, stb_scr, m_scr, acc_scr, *, tq, tk):
    p = pl.program_id(1)
    S = k_ref.shape[1]
    nh = q_ref.shape[2] // HEAD
    g128 = gm_ref[...]
    lane = lax.broadcasted_iota(jnp.int32, (1, LANES), 1)
    feat = lane < HEAD

    def minus_lanes(hh):
        lo = AUG0 + 6 * (nh * p + hh)
        return (lane >= lo) & (lane < lo + 3)

    def plus_lanes(hh):
        lo = AUG0 + 6 * (nh * p + hh) + 3
        return (lane >= lo) & (lane < lo + 3)

    def head_tile(x, hh):
        t = x[:, (hh // 2) * LANES:(hh // 2 + 1) * LANES]
        return pltpu.roll(t, HEAD, axis=1) if hh % 2 else t

    for i in range(S // tk):
        rows = slice(i * tk, (i + 1) * tk)
        kb = k_ref[0, rows, :].astype(F32)
        (ssq,) = _head_sums([kb * kb], g128, (False,))
        kn = kb * lax.rsqrt(ssq * (1.0 / HEAD) + RMS_EPS) * kg_ref[...]
        aug = ak_ref[0, rows, :].astype(F32)
        for hh in range(nh):
            kp = jnp.where(feat, head_tile(kn, hh), jnp.where(plus_lanes(hh), 1.0, aug))
            kp_scr[hh, rows, :] = _bf(kp)
        vt = _bf(jnp.transpose(v_ref[0, rows, :].astype(F32)))
        for hh in range(nh):
            vt_scr[hh * VROWS:hh * VROWS + HEAD, rows] = vt[hh * HEAD:(hh + 1) * HEAD, :]
            vt_scr[hh * VROWS + HEAD:(hh + 1) * VROWS, rows] = jnp.ones((VROWS - HEAD, tk), BF16)

    def q_side(qi):
        rows = slice(qi * tq, (qi + 1) * tq)
        q = q_ref[0, rows, :].astype(F32)
        (ssq,) = _head_sums([q * q], g128, (False,))
        qn = q * lax.rsqrt(ssq * (1.0 / HEAD) + RMS_EPS) * (qg_ref[...] * (HEAD ** -0.5 * LOG2E))
        augq = ak_ref[0, rows, :].astype(F32)
        for hh in range(nh):
            qp_scr[qi % 2, hh] = _bf(jnp.where(
                feat, head_tile(qn, hh),
                jnp.where(minus_lanes(hh), 1.0, jnp.where(plus_lanes(hh), augq, 0.0))))
        m_scr[qi % 2] = jnp.full(m_scr.shape[1:], -1e30, F32)
        acc_scr[qi % 2] = jnp.zeros(acc_scr.shape[1:], F32)

    def scores(qi, j, st_ref):
        for hh in range(nh):
            st_ref[hh] = _dot_nt(kp_scr[hh, j * tk:(j + 1) * tk, :], qp_scr[qi % 2, hh])

    def softmax_pv(qi, j, st_ref):
        st = [st_ref[hh] for hh in range(nh)]
        if j == qi:
            keyi = lax.broadcasted_iota(jnp.int32, (tk, tq), 0)
            qryi = lax.broadcasted_iota(jnp.int32, (tk, tq), 1)
            st = [jnp.where(qryi >= keyi, x, -1e30) for x in st]
        st = [x.reshape(tk // 8, 8, tq) for x in st]
        m_old = [m_scr[qi % 2, hh] for hh in range(nh)]
        m_new = [jnp.maximum(m_old[hh], _sublane_max(jnp.max(st[hh], axis=0))) for hh in range(nh)]
        alpha = [jnp.exp2(m_old[hh] - m_new[hh]) for hh in range(nh)]
        pt = [_bf(jnp.exp2(st[hh] - m_new[hh][None]).reshape(tk, tq)) for hh in range(nh)]
        pv = [_dot(vt_scr[hh * VROWS:(hh + 1) * VROWS, j * tk:(j + 1) * tk], pt[hh])
              for hh in range(nh)]
        for hh in range(nh):
            m_scr[qi % 2, hh] = m_new[hh]
            acc = acc_scr[qi % 2, hh].reshape(VROWS // 8, 8, tq) * alpha[hh][None]
            acc_scr[qi % 2, hh] = acc.reshape(VROWS, tq) + pv[hh]

    def finish(qi):
        rows = slice(qi * tq, (qi + 1) * tq)
        ot = jnp.concatenate(
            [(acc_scr[qi % 2, hh, 0:HEAD, :].reshape(HEAD // 8, 8, tq)
              * (1.0 / acc_scr[qi % 2, hh, HEAD:HEAD + 8, :])[None]).reshape(HEAD, tq)
             for hh in range(nh)], axis=0)
        g = g_ref[0, rows, :].astype(F32)
        o_ref[0, rows, :] = _bf(jnp.transpose(ot) * (g * _sigmoid(g)))

    pairs = [(qi, j) for qi in range(S // tq) for j in range(qi + 1)]
    bufs = (sta_scr, stb_scr)
    for n, (qi, j) in enumerate(pairs):
        if j == 0:
            q_side(qi)
        scores(qi, j, bufs[n % 2])
        if n > 0:
            pqi, pj = pairs[n - 1]
            softmax_pv(pqi, pj, bufs[(n - 1) % 2])
            if pj == pqi:
                finish(pqi)
    qi, j = pairs[-1]
    softmax_pv(qi, j, bufs[(len(pairs) - 1) % 2])
    finish(qi)


def _fox(u3, aug, qg2, kg2, g128, *, col_q, d_b, tq, tk):
    B, S, _ = u3.shape
    fw = qg2.shape[1]
    nh = fw // HEAD
    assert S % tq == 0 and tq == tk and d_b % fw == 0
    cq, ck, cv, cg = ((col_q + i * d_b) // fw for i in range(4))
    return pl.pallas_call(
        functools.partial(_fox_kernel, tq=tq, tk=tk),
        grid=(B, d_b // fw),
        in_specs=[
            pl.BlockSpec((1, S, fw), lambda b, p: (b, 0, cq + p)),
            pl.BlockSpec((1, S, fw), lambda b, p: (b, 0, ck + p)),
            pl.BlockSpec((1, S, fw), lambda b, p: (b, 0, cv + p)),
            pl.BlockSpec((1, S, fw), lambda b, p: (b, 0, cg + p)),
            pl.BlockSpec((1, S, LANES), lambda b, p: (b, 0, 0)),
            pl.BlockSpec((1, fw), lambda b, p: (0, 0)),
            pl.BlockSpec((1, fw), lambda b, p: (0, 0)),
            pl.BlockSpec((LANES, LANES), lambda b, p: (0, 0)),
        ],
        out_specs=pl.BlockSpec((1, S, fw), lambda b, p: (b, 0, p)),
        out_shape=jax.ShapeDtypeStruct((B, S, d_b), BF16),
        scratch_shapes=[
            pltpu.VMEM((nh, S, LANES), BF16),
            pltpu.VMEM((nh * VROWS, S), BF16),
            pltpu.VMEM((2, nh, tq, LANES), BF16),
            pltpu.VMEM((nh, tk, tq), F32),
            pltpu.VMEM((nh, tk, tq), F32),
            pltpu.VMEM((2, nh, 8, tq), F32),
            pltpu.VMEM((2, nh, VROWS, tq), F32),
        ],
        compiler_params=pltpu.CompilerParams(
            dimension_semantics=("parallel", "parallel"), vmem_limit_bytes=VMEM_LIMIT),
        name="fox",
    )(u3, u3, u3, u3, aug, qg2, kg2, g128)


def _out_kernel(x_ref, ya_ref, yb_ref, ga_ref, gb_ref, woa_ref, wob_ref, wo_ref, fg_ref, o_ref,
                woa_s, wob_s, wo_s):
    @pl.when(pl.program_id(0) == 0)
    def _():
        woa_s[...] = _bf(woa_ref[...])
        wob_s[...] = _bf(wob_ref[...])
        wo_s[...] = _bf(wo_ref[...])

    for s in range(x_ref.shape[0] // OUT_SUB):
        rows = slice(s * OUT_SUB, (s + 1) * OUT_SUB)
        za = _dot(ya_ref[rows, :], woa_s[...])
        zb = _dot(yb_ref[rows, :], wob_s[...])
        merged = (_sigmoid(ga_ref[rows, :].astype(F32)) * za
                  + _sigmoid(gb_ref[rows, :].astype(F32)) * zb)
        o = x_ref[rows, :] + _dot(_bf(merged), wo_s[...])
        ms = jnp.mean(o * o, axis=-1, keepdims=True)
        o_ref[rows, :] = o * lax.rsqrt(ms + RMS_EPS) * fg_ref[...]


def _out(x2, ya2, yb2, u2, woa, wob, wo, fg, *, tm):
    T, D = x2.shape
    assert T % tm == 0 and tm % OUT_SUB == 0
    full = lambda arr: pl.BlockSpec(arr.shape, lambda i: (0,) * arr.ndim)
    return pl.pallas_call(
        _out_kernel,
        grid=(T // tm,),
        in_specs=[
            pl.BlockSpec((tm, D), lambda i: (i, 0)),
            pl.BlockSpec((tm, ya2.shape[1]), lambda i: (i, 0)),
            pl.BlockSpec((tm, yb2.shape[1]), lambda i: (i, 0)),
            pl.BlockSpec((tm, D), lambda i: (i, 0)),
            pl.BlockSpec((tm, D), lambda i: (i, 1)),
            full(woa), full(wob), full(wo), full(fg),
        ],
        out_specs=pl.BlockSpec((tm, D), lambda i: (i, 0)),
        out_shape=jax.ShapeDtypeStruct((T, D), F32),
        scratch_shapes=[pltpu.VMEM(w.shape, BF16) for w in (woa, wob, wo)],
        compiler_params=pltpu.CompilerParams(
            dimension_semantics=("arbitrary",), vmem_limit_bytes=VMEM_LIMIT),
        name="outstage",
    )(x2, ya2, yb2, u2, u2, woa, wob, wo, fg)


def _block_ones(width):
    i = jnp.arange(width) // HEAD
    return (i[:, None] == i[None, :]).astype(BF16)


def _layer(x2, B, S, norm_g, w_in, shift_mu, w_lora_up, w0, a_lora_up, a0, k_k, k_a, r_k,
           lnx_w, lnx_b, f_bias, q_norm_g, k_norm_g, w_out_a, w_out_b, w_out, out_gain):
    T, D = x2.shape
    d_a = w0.shape[0]
    d_b = w_out_b.shape[0]
    rank = w_lora_up.shape[0]
    h_b = f_bias.shape[0]
    rw = 4 * d_a + 2 * rank
    fx = 4 * d_b + h_b

    segs = [(rw + fx, 2 * D), (0, 3 * d_a), (3 * d_a + 2 * rank, d_a), (rw, 4 * d_b),
            (3 * d_a, 2 * rank), (rw + 4 * d_b, h_b)]
    w_perm_t = _regroup_rows(jnp.transpose(w_in), segs, pad_to=LANES, tc=LANES)
    col_r = 2 * D
    col_q = col_r + 4 * d_a
    col_wa = col_q + 4 * d_b

    u2, uf = _inproj(x2, norm_g.reshape(1, D), w_perm_t, tm=1024, tn=1280)
    u3 = u2.reshape(B, S, u2.shape[1])

    mu = shift_mu
    rows = [mu[:d_a], mu[d_a:2 * d_a], mu[2 * d_a:3 * d_a], mu[3 * d_a + 2 * rank:],
            w0, a0, k_k, k_a, r_k.reshape(-1), lnx_w, lnx_b]
    pvec = jnp.stack(rows + [jnp.zeros_like(w0)] * (16 - len(rows)), axis=0)
    mu_wa = mu[3 * d_a:3 * d_a + 2 * rank].reshape(1, 2 * rank)
    z = jnp.zeros((rank, d_a), F32)
    lora = _bf(jnp.concatenate(
        [jnp.concatenate([w_lora_up, z], axis=1), jnp.concatenate([z, a_lora_up], axis=1)], axis=0))
    ya = _rwkv(u3, pvec, mu_wa, lora, _block_ones(LANES), col_r=col_r, col_wa=col_wa, d_a=d_a)

    fb_pad = jnp.pad(f_bias, (0, LANES - h_b)).reshape(1, LANES)
    aug = _fprep(uf.reshape(B, S, LANES), fb_pad, _aug_selector(h_b))
    qg2 = jnp.tile(q_norm_g, FOX_HEADS).reshape(1, FOX_HEADS * HEAD)
    kg2 = jnp.tile(k_norm_g, FOX_HEADS).reshape(1, FOX_HEADS * HEAD)
    yb = _fox(u3, aug, qg2, kg2, _block_ones(LANES), col_q=col_q, d_b=d_b, tq=256, tk=256)

    return _out(x2, ya.reshape(T, d_a), yb.reshape(T, d_b), u2,
                w_out_a, w_out_b, w_out, out_gain.reshape(1, D), tm=1024)


def kernel(x, norm_g, w_in, shift_mu, w_lora_up, w0, a_lora_up, a0, k_k, k_a, r_k, lnx_w, lnx_b,
           f_bias, q_norm_g, k_norm_g, w_out_a, w_out_b, w_out, final_norm_g):
    B, S, D = x.shape
    depth = w_in.shape[0]
    assert depth == 1, "the fused output stage applies the final norm after the single layer"
    x2 = x.reshape(B * S, D)
    out = _layer(x2, B, S, norm_g[0], w_in[0], shift_mu[0], w_lora_up[0], w0[0], a_lora_up[0],
                 a0[0], k_k[0], k_a[0], r_k[0], lnx_w[0], lnx_b[0], f_bias[0], q_norm_g[0],
                 k_norm_g[0], w_out_a[0], w_out_b[0], w_out[0], final_norm_g)
    return out.reshape(B, S, D)
```
